```python
import math, functools
import jax, jax.numpy as jnp
from jax import lax
import numpy as np

D_MODEL = 1024
BATCH = 2
SEQ = 8192
DEPTH = 1
DEC_BATCH = 128
DEC_SEQ = 1
PAST_LEN = 2048
PAGE_SIZE = 128

N_HEADS = 8
HEAD_DIM = 64
V_DIM = 2 * HEAD_DIM
QK_WIDTH = N_HEADS * 2 * HEAD_DIM
ATTN_WIDTH = N_HEADS * V_DIM
ROPE_THETA = 10000.0
Q_BLOCK = 128
POOL_WINDOWS = (2, 4, 8, 16)
POOL_GROUPS = len(POOL_WINDOWS)
POOL_GROUP_DIM = 128
POOL_WIDTH = POOL_GROUPS * POOL_GROUP_DIM
POOL_OUT_DIM = D_MODEL // POOL_GROUPS
POOL_BUF = max(POOL_WINDOWS) - 1
N_BRANCHES = 2
IN_COLS = 2 * QK_WIDTH + ATTN_WIDTH + POOL_WIDTH + N_BRANCHES * D_MODEL
N_GROUPS = 4
EXPERTS_PER_GROUP = 4
N_EXPERTS = N_GROUPS * EXPERTS_PER_GROUP
TOP_K_IN_GROUP = 2
EXPERT_FF = D_MODEL // 4
LN_EPS = 1e-5
SUBLN_EPS = 1e-5
N_MOD = 6

kernel_name = "hybrid_pool_diffattn_hmoe_decode_step"


def layer_norm(x, g, b):
    xf = x.astype(jnp.float32)
    mu = jnp.mean(xf, axis=-1, keepdims=True)
    var = jnp.mean(jnp.square(xf - mu), axis=-1, keepdims=True)
    return ((xf - mu) * lax.rsqrt(var + LN_EPS) * g.astype(jnp.float32) + b.astype(jnp.float32)).astype(x.dtype)


def head_rmsnorm(o, g):
    of = o.astype(jnp.float32)
    y = of * lax.rsqrt(jnp.mean(jnp.square(of), axis=-1, keepdims=True) + SUBLN_EPS) * g.astype(jnp.float32)
    return y.astype(o.dtype)


def rope(x, pos):
    half = HEAD_DIM // 2
    inv = ROPE_THETA ** (-jnp.arange(half, dtype=jnp.float32) / half)
    ang = pos.astype(jnp.float32)[:, None] * inv[None, :]
    cos = jnp.cos(ang)[:, None, None, :]
    sin = jnp.sin(ang)[:, None, None, :]
    xf = x.astype(jnp.float32)
    x1, x2 = xf[..., :half], xf[..., half:]
    return jnp.concatenate([x1 * cos - x2 * sin, x2 * cos + x1 * sin], axis=-1).astype(x.dtype)


def diff_weights(scores, lam):
    p = jax.nn.softmax(scores, axis=-1)
    return p[:, :, 0] - lam * p[:, :, 1]


def attend_prompt(q, k, v, lam):
    b, s = q.shape[0], q.shape[1]
    nb = s // Q_BLOCK
    qb = q.reshape(b, nb, Q_BLOCK, N_HEADS, 2, HEAD_DIM).transpose(1, 0, 2, 3, 4, 5)
    k_pos = jnp.arange(s)
    scale = HEAD_DIM ** -0.5

    def one_block(args):
        i, q_i = args
        sc = jnp.einsum('bqhmd,bkhmd->bhmqk', q_i, k, preferred_element_type=jnp.float32) * scale
        q_pos = i * Q_BLOCK + jnp.arange(Q_BLOCK)
        mask = k_pos[None, :] <= q_pos[:, None]
        sc = jnp.where(mask, sc, -jnp.inf)
        w = diff_weights(sc, lam)
        return jnp.einsum('bhqk,bkhv->bqhv', w.astype(v.dtype), v)

    out = lax.map(one_block, (jnp.arange(nb), qb))
    return out.transpose(1, 0, 2, 3, 4).reshape(b, s, N_HEADS, V_DIM)


def attend_sample(q, k_new, v_new, lam, k_past, v_past):
    s1 = q.shape[1]
    p = k_past.shape[1]
    scale = HEAD_DIM ** -0.5
    sc_past = jnp.einsum('bqhmd,bkhmd->bhmqk', q, k_past, preferred_element_type=jnp.float32) * scale
    sc_new = jnp.einsum('bqhmd,bkhmd->bhmqk', q, k_new, preferred_element_type=jnp.float32) * scale
    causal = jnp.arange(s1)[None, :] <= jnp.arange(s1)[:, None]
    sc_new = jnp.where(causal, sc_new, -jnp.inf)
    w = diff_weights(jnp.concatenate([sc_past, sc_new], axis=-1), lam).astype(v_new.dtype)
    return (jnp.einsum('bhqk,bkhv->bqhv', w[..., :p], v_past)
            + jnp.einsum('bhqk,bkhv->bqhv', w[..., p:], v_new))


def pool_mixer(u_buf, u_new, start_pos, w_pool, pool_scale):
    bt, s = u_new.shape[0], u_new.shape[1]
    u_ext = jnp.concatenate([u_buf, u_new], axis=1)
    uf = u_ext.astype(jnp.float32)
    cs = jnp.concatenate([jnp.zeros((bt, 1, POOL_WIDTH), jnp.float32), jnp.cumsum(uf, axis=1)], axis=1)
    pos = start_pos + jnp.arange(s)
    outs = []
    for g, win in enumerate(POOL_WINDOWS):
        sl = slice(g * POOL_GROUP_DIM, (g + 1) * POOL_GROUP_DIM)
        top = cs[:, POOL_BUF + 1:POOL_BUF + 1 + s, sl]
        bot = cs[:, POOL_BUF + 1 - win:POOL_BUF + 1 - win + s, sl]
        cnt = jnp.minimum(win, pos + 1).astype(jnp.float32)[None, :, None]
        outs.append((top - bot) / cnt - uf[:, POOL_BUF:, sl])
    pooled = jnp.concatenate(outs, axis=-1).reshape(bt, s, POOL_GROUPS, POOL_GROUP_DIM).astype(u_new.dtype)
    mixed = jnp.einsum('bsgc,gcd->bsgd', pooled, w_pool).reshape(bt, s, D_MODEL) * pool_scale
    return mixed, u_ext[:, -POOL_BUF:]


def hier_moe(h, w_group, b_group, w_router, b_router, w_e_gate, w_e_up, w_e_down):
    bt, s, d = h.shape
    t = h.reshape(bt * s, d)
    n_tok = t.shape[0]
    g_logits = (t @ w_group + b_group).astype(jnp.float32)
    g_prob = jax.nn.softmax(g_logits, axis=-1)
    g_idx = jnp.argmax(g_logits, axis=-1)
    p_g = jnp.take_along_axis(g_prob, g_idx[:, None], axis=-1)
    e_logits = (t @ w_router + b_router).astype(jnp.float32).reshape(n_tok, N_GROUPS, EXPERTS_PER_GROUP)
    e_sel = jnp.take_along_axis(e_logits, g_idx[:, None, None], axis=1)[:, 0]
    top_v, top_i = lax.top_k(e_sel, TOP_K_IN_GROUP)
    top_w = jax.nn.softmax(top_v, axis=-1) * p_g
    within = jnp.sum(jax.nn.one_hot(top_i, EXPERTS_PER_GROUP, dtype=jnp.float32) * top_w[..., None], axis=1)
    combine = (jax.nn.one_hot(g_idx, N_GROUPS, dtype=jnp.float32)[:, :, None] * within[:, None, :])
    combine = combine.reshape(n_tok, N_EXPERTS).astype(h.dtype)
    hg = jnp.einsum('td,edf->tef', t, w_e_gate)
    hu = jnp.einsum('td,edf->tef', t, w_e_up)
    act = jax.nn.silu(hg) * hu * combine[:, :, None]
    return jnp.einsum('tef,efd->td', act, w_e_down).reshape(bt, s, d)


def decoder_layer(x, c, start_pos, attend, pool_buf, lam_init, alpha,
                  w_ada, b_ada, w_in, lambda_q1, lambda_k1, lambda_q2, lambda_k2, subln_g,
                  w_pool, pool_scale, w_out, ln1_g, ln1_b,
                  w_group, b_group, w_router, b_router, w_e_gate, w_e_up, w_e_down, ln2_g, ln2_b):
    bt, s = x.shape[0], x.shape[1]
    mod = (jax.nn.silu(c) @ w_ada + b_ada)[:, None, :]
    sh1, sc1, g1, sh2, sc2, g2 = jnp.split(mod, N_MOD, axis=-1)
    h = x * (1 + sc1) + sh1
    proj = h @ w_in
    q, k, v, u, gates = jnp.split(
        proj, [QK_WIDTH, 2 * QK_WIDTH, 2 * QK_WIDTH + ATTN_WIDTH, 2 * QK_WIDTH + ATTN_WIDTH + POOL_WIDTH], axis=-1)
    pos = start_pos + jnp.arange(s)
    q = rope(q.reshape(bt, s, N_HEADS, 2, HEAD_DIM), pos)
    k = rope(k.reshape(bt, s, N_HEADS, 2, HEAD_DIM), pos)
    v = v.reshape(bt, s, N_HEADS, V_DIM)
    lam = (jnp.exp(jnp.sum(lambda_q1.astype(jnp.float32) * lambda_k1.astype(jnp.float32)))
           - jnp.exp(jnp.sum(lambda_q2.astype(jnp.float32) * lambda_k2.astype(jnp.float32))) + lam_init)
    o = attend(q, k, v, lam)
    attn_out = (head_rmsnorm(o, subln_g) * (1.0 - lam_init)).reshape(bt, s, ATTN_WIDTH)
    pool_out, new_pool = pool_mixer(pool_buf, u, start_pos, w_pool, pool_scale)
    g_pool, g_attn = jnp.split(jax.nn.sigmoid(gates), N_BRANCHES, axis=-1)
    merged = g_pool * pool_out + g_attn * attn_out
    x = layer_norm(alpha * x + (1 + g1) * (merged @ w_out), ln1_g, ln1_b)
    h2 = x * (1 + sc2) + sh2
    x = layer_norm(alpha * x + (1 + g2) * hier_moe(h2, w_group, b_group, w_router, b_router,
                                                     w_e_gate, w_e_up, w_e_down), ln2_g, ln2_b)
    return x, k, v, new_pool


def setup_inputs(seed: int = 0) -> dict:
    key = jax.random.key(seed)
    ks = jax.random.split(key, 32)
    n_pages = PAST_LEN // PAGE_SIZE
    n_used = DEC_BATCH * n_pages
    n_phys = (n_used * 5) // 4
    beta = (8.0 * DEPTH) ** -0.25
    f32 = jnp.float32
    nrm = lambda k, shp, s: jax.random.normal(k, shp, f32) * s
    page_table = jax.random.permutation(ks[0], n_phys)[:n_used].reshape(DEC_BATCH, n_pages).astype(jnp.int32)
    return {
        "x_prompt": nrm(ks[1], (BATCH, SEQ, D_MODEL), 1.0),
        "x_sample": nrm(ks[2], (DEC_BATCH, DEC_SEQ, D_MODEL), 1.0),
        "cache_k": nrm(ks[3], (DEPTH, n_phys, PAGE_SIZE, N_HEADS, 2, HEAD_DIM), 1.0),
        "cache_v": nrm(ks[4], (DEPTH, n_phys, PAGE_SIZE, N_HEADS, V_DIM), 1.0),
        "state_pool": nrm(ks[5], (DEPTH, DEC_BATCH, POOL_BUF, POOL_WIDTH), 1.0),
        "page_table": page_table,
        "c_prompt": nrm(ks[6], (BATCH, D_MODEL), 1.0),
        "c_sample": nrm(ks[7], (DEC_BATCH, D_MODEL), 1.0),
        "w_ada": nrm(ks[8], (DEPTH, D_MODEL, N_MOD * D_MODEL), 0.25 * D_MODEL ** -0.5),
        "b_ada": nrm(ks[9], (DEPTH, N_MOD * D_MODEL), 0.01),
        "w_in": nrm(ks[10], (DEPTH, D_MODEL, IN_COLS), D_MODEL ** -0.5),
        "lambda_q1": nrm(ks[11], (DEPTH, HEAD_DIM), 0.1),
        "lambda_k1": nrm(ks[12], (DEPTH, HEAD_DIM), 0.1),
        "lambda_q2": nrm(ks[13], (DEPTH, HEAD_DIM), 0.1),
        "lambda_k2": nrm(ks[14], (DEPTH, HEAD_DIM), 0.1),
        "subln_g": 1.0 + nrm(ks[15], (DEPTH, V_DIM), 0.01),
        "w_pool": nrm(ks[16], (DEPTH, POOL_GROUPS, POOL_GROUP_DIM, POOL_OUT_DIM), POOL_GROUP_DIM ** -0.5),
        "pool_scale": 1.0 + nrm(ks[17], (DEPTH, D_MODEL), 0.01),
        "w_out": nrm(ks[18], (DEPTH, D_MODEL, D_MODEL), beta * D_MODEL ** -0.5),
        "ln1_g": 1.0 + nrm(ks[19], (DEPTH, D_MODEL), 0.01),
        "ln1_b": nrm(ks[20], (DEPTH, D_MODEL), 0.01),
        "w_group": nrm(ks[21], (DEPTH, D_MODEL, N_GROUPS), D_MODEL ** -0.5),
        "b_group": nrm(ks[22], (DEPTH, N_GROUPS), 0.01),
        "w_router": nrm(ks[23], (DEPTH, D_MODEL, N_EXPERTS), D_MODEL ** -0.5),
        "b_router": nrm(ks[24], (DEPTH, N_EXPERTS), 0.01),
        "w_e_gate": nrm(ks[25], (DEPTH, N_EXPERTS, D_MODEL, EXPERT_FF), D_MODEL ** -0.5),
        "w_e_up": nrm(ks[26], (DEPTH, N_EXPERTS, D_MODEL, EXPERT_FF), D_MODEL ** -0.5),
        "w_e_down": nrm(ks[27], (DEPTH, N_EXPERTS, EXPERT_FF, D_MODEL), beta * EXPERT_FF ** -0.5),
        "ln2_g": 1.0 + nrm(ks[28], (DEPTH, D_MODEL), 0.01),
        "ln2_b": nrm(ks[29], (DEPTH, D_MODEL), 0.01),
    }


def reference(x_prompt, x_sample, cache_k, cache_v, state_pool, page_table, c_prompt, c_sample,
              w_ada, b_ada, w_in, lambda_q1, lambda_k1, lambda_q2, lambda_k2, subln_g,
              w_pool, pool_scale, w_out, ln1_g, ln1_b,
              w_group, b_group, w_router, b_router, w_e_gate, w_e_up, w_e_down, ln2_g, ln2_b):
    alpha = (2.0 * DEPTH) ** 0.25
    db = x_sample.shape[0]
    past_len = page_table.shape[1] * PAGE_SIZE
    xp, xs = x_prompt, x_sample
    kp_rows, vp_rows, pool_p, ks_rows, vs_rows, pool_s = [], [], [], [], [], []
    for l in range(DEPTH):
        lam_init = 0.8 - 0.6 * math.exp(-0.3 * l)
        weights = (w_ada[l], b_ada[l], w_in[l], lambda_q1[l], lambda_k1[l], lambda_q2[l], lambda_k2[l], subln_g[l],
                   w_pool[l], pool_scale[l], w_out[l], ln1_g[l], ln1_b[l],
                   w_group[l], b_group[l], w_router[l], b_router[l], w_e_gate[l], w_e_up[l], w_e_down[l],
                   ln2_g[l], ln2_b[l])
        zero_buf = jnp.zeros((xp.shape[0], POOL_BUF, POOL_WIDTH), xp.dtype)
        xp, kp, vp, pp = decoder_layer(xp, c_prompt, 0, attend_prompt, zero_buf, lam_init, alpha, *weights)
        k_past = cache_k[l][page_table].reshape(db, past_len, N_HEADS, 2, HEAD_DIM)
        v_past = cache_v[l][page_table].reshape(db, past_len, N_HEADS, V_DIM)
        attend_s = functools.partial(attend_sample, k_past=k_past, v_past=v_past)
        xs, kn, vn, ps = decoder_layer(xs, c_sample, past_len, attend_s, state_pool[l], lam_init, alpha, *weights)
        kp_rows.append(kp); vp_rows.append(vp); pool_p.append(pp)
        ks_rows.append(kn); vs_rows.append(vn); pool_s.append(ps)
    return (xp, xs, jnp.stack(kp_rows), jnp.stack(vp_rows), jnp.stack(pool_p),
            jnp.stack(ks_rows), jnp.stack(vs_rows), jnp.stack(pool_s))
```

```python
import functools
import math

import jax
import jax.numpy as jnp
from jax import lax
from jax.experimental import pallas as pl
from jax.experimental.pallas import tpu as pltpu

F32 = jnp.float32
BF16 = jnp.bfloat16

LANES = 128
VMEM_LIMIT = 56 << 20

N_MOD = 6
N_HEADS = 8
HEAD_DIM = 64
V_DIM = 2 * HEAD_DIM
ROPE_THETA = 10000.0
POOL_WINDOWS = (2, 4, 8, 16)
POOL_GROUP_DIM = 128
POOL_BUF = max(POOL_WINDOWS) - 1
N_GROUPS = 4
EXPERTS_PER_GROUP = 4
N_EXPERTS = N_GROUPS * EXPERTS_PER_GROUP
LN_EPS = 1e-5
SUBLN_EPS = 1e-5
ROUTER_ROWS = 32

NT_DIMS = (((1,), (1,)), ((), ()))


def _params(*sem):
    return pltpu.CompilerParams(dimension_semantics=sem, vmem_limit_bytes=VMEM_LIMIT)


def _sigmoid(x):
    return 1.0 / (1.0 + jnp.exp(-x))


def _layer_norm(x, g, b):
    mu = jnp.mean(x, axis=-1, keepdims=True)
    xc = x - mu
    var = jnp.mean(xc * xc, axis=-1, keepdims=True)
    return xc * lax.rsqrt(var + LN_EPS) * g + b


def _lambda(lam_ref, lam_init):
    lp = lam_ref[...]
    a = jnp.sum(lp[0:1] * lp[1:2], axis=1, keepdims=True)
    b = jnp.sum(lp[2:3] * lp[3:4], axis=1, keepdims=True)
    return jnp.exp(a) - jnp.exp(b) + lam_init


def _ada_kernel(c_ref, w_ref, b_ref, o_ref):
    c = c_ref[...]
    s = (c * _sigmoid(c)).astype(BF16)
    o_ref[...] = jnp.dot(s, w_ref[...].astype(BF16), preferred_element_type=F32) + b_ref[...]


def _ada(c_all, w_ada, b_ada):
    n, d = c_all.shape
    cols = w_ada.shape[1]
    return pl.pallas_call(
        _ada_kernel,
        grid=(cols // d,),
        in_specs=[pl.BlockSpec((n, d), lambda j: (0, 0)),
                  pl.BlockSpec((d, d), lambda j: (0, j)),
                  pl.BlockSpec((1, d), lambda j: (0, j))],
        out_specs=pl.BlockSpec((n, d), lambda j: (0, j)),
        out_shape=jax.ShapeDtypeStruct((n, cols), F32),
        compiler_params=_params("arbitrary"),
        name="ada_mod",
    )(c_all, w_ada, b_ada.reshape(1, cols))


def _inproj_kernel(x_ref, sc_ref, sh_ref, w_ref, wkt_ref, cos_ref, sa_ref, sb_ref, cost_ref, sint_ref,
                   q_ref, ktf_ref, ktb_ref, vf_ref, vb_ref, u_ref, g_ref, *maybe_krow_ref, d, pw):
    h = (x_ref[...] * (1.0 + sc_ref[...]) + sh_ref[...]).astype(BF16)
    cos, sa, sb = cos_ref[...], sa_ref[...], sb_ref[...]
    half = HEAD_DIM // 2

    def rope(blk):
        return blk * cos + pltpu.roll(blk, LANES - half, 1) * sa + pltpu.roll(blk, half, 1) * sb

    q = jnp.dot(h, w_ref[:, 0:d], preferred_element_type=F32)
    for c in range(d // LANES):
        sl = slice(c * LANES, (c + 1) * LANES)
        q_ref[:, sl] = (rope(q[:, sl]) * (HEAD_DIM ** -0.5)).astype(BF16)
    kt = lax.dot_general(wkt_ref[...], h, NT_DIMS, preferred_element_type=F32)
    cost, sint = cost_ref[...], sint_ref[...]
    for g in range(d // HEAD_DIM):
        x1 = kt[g * HEAD_DIM:g * HEAD_DIM + half]
        x2 = kt[g * HEAD_DIM + half:(g + 1) * HEAD_DIM]
        r1 = x1 * cost - x2 * sint
        r2 = x2 * cost + x1 * sint
        ktf_ref[g * HEAD_DIM:g * HEAD_DIM + half, :] = r1
        ktf_ref[g * HEAD_DIM + half:(g + 1) * HEAD_DIM, :] = r2
        ktb_ref[g * HEAD_DIM:g * HEAD_DIM + half, :] = r1.astype(BF16)
        ktb_ref[g * HEAD_DIM + half:(g + 1) * HEAD_DIM, :] = r2.astype(BF16)
    if maybe_krow_ref:
        k = jnp.dot(h, w_ref[:, d:2 * d], preferred_element_type=F32)
        for c in range(d // LANES):
            sl = slice(c * LANES, (c + 1) * LANES)
            maybe_krow_ref[0][:, sl] = rope(k[:, sl])
    v = jnp.dot(h, w_ref[:, 2 * d:3 * d], preferred_element_type=F32)
    vf_ref[...] = v
    vb_ref[...] = v.astype(BF16)
    u_ref[...] = jnp.dot(h, w_ref[:, 3 * d:3 * d + pw], preferred_element_type=F32)
    g_ref[...] = _sigmoid(jnp.dot(h, w_ref[:, 3 * d + pw:], preferred_element_type=F32))


def _inproj(x, sc_spec, sh_spec, sc_arr, sh_arr, w_in_bf, wkt_bf, tabs, tab_spec, tabs_t, tab_t_spec,
            tm, n_seq, emit_k_rows):
    t, d = x.shape
    cols = w_in_bf.shape[1]
    pw = cols - 5 * d
    per_seq = t // n_seq
    nblk = per_seq // tm
    row = lambda width: pl.BlockSpec((tm, width), lambda i: (i, 0))
    kt_spec = pl.BlockSpec((None, d, tm), lambda i: (i // nblk, 0, i % nblk))
    out_specs = [row(d), kt_spec, kt_spec, row(d), row(d), row(pw), row(2 * d)]
    out_shape = [jax.ShapeDtypeStruct((t, d), BF16),
                 jax.ShapeDtypeStruct((n_seq, d, per_seq), F32), jax.ShapeDtypeStruct((n_seq, d, per_seq), BF16),
                 jax.ShapeDtypeStruct((t, d), F32), jax.ShapeDtypeStruct((t, d), BF16),
                 jax.ShapeDtypeStruct((t, pw), F32), jax.ShapeDtypeStruct((t, 2 * d), F32)]
    if emit_k_rows:
        out_specs.append(row(d))
        out_shape.append(jax.ShapeDtypeStruct((t, d), F32))
    return pl.pallas_call(
        functools.partial(_inproj_kernel, d=d, pw=pw),
        grid=(t // tm,),
        in_specs=[row(d), sc_spec, sh_spec,
                  pl.BlockSpec((d, cols), lambda i: (0, 0), pipeline_mode=pl.Buffered(1)),
                  pl.BlockSpec((d, d), lambda i: (0, 0), pipeline_mode=pl.Buffered(1)),
                  tab_spec, tab_spec, tab_spec, tab_t_spec, tab_t_spec],
        out_specs=out_specs,
        out_shape=out_shape,
        compiler_params=_params("arbitrary"),
        name="in_proj",
    )(x, sc_arr, sh_arr, w_in_bf, wkt_bf, *tabs, *tabs_t)


def _rope_tables(pos):
    half = HEAD_DIM // 2
    inv = ROPE_THETA ** (-jnp.arange(half, dtype=F32) / half)
    ang = pos.astype(F32)[:, None] * inv[None, :]
    c, s = jnp.cos(ang), jnp.sin(ang)
    reps = LANES // half
    cos = jnp.tile(c, (1, reps))
    sin = jnp.tile(s, (1, reps))
    lo = (jnp.arange(LANES) % HEAD_DIM) < half
    return (cos, jnp.where(lo, -sin, 0.0), jnp.where(lo, 0.0, sin)), (c.T, s.T)


def _attn_kernel(lam_ref, q_ref, kt_ref, v_ref, g_ref, o_ref, q2_sc, m_sc, l_sc, acc_sc,
                 *, tq, lam_init):
    qi = pl.program_id(2)
    q = q_ref[...].astype(F32)
    lo = lax.broadcasted_iota(jnp.int32, q.shape, 1) < HEAD_DIM
    q2_sc[0:tq, :] = jnp.where(lo, q, 0.0).astype(BF16)
    q2_sc[tq:2 * tq, :] = jnp.where(lo, 0.0, q).astype(BF16)
    m_sc[...] = jnp.full(m_sc.shape, -jnp.inf, F32)
    l_sc[...] = jnp.zeros(l_sc.shape, F32)
    acc_sc[...] = jnp.zeros(acc_sc.shape, F32)

    def step(ki, masked):
        start = pl.multiple_of(ki * tq, tq)
        kt = kt_ref[:, pl.ds(start, tq)]
        v = v_ref[pl.ds(start, tq), :]
        s = jnp.dot(q2_sc[...], kt, preferred_element_type=F32)
        if masked:
            row = lax.broadcasted_iota(jnp.int32, s.shape, 0)
            col = lax.broadcasted_iota(jnp.int32, s.shape, 1)
            row = jnp.where(row >= tq, row - tq, row)
            s = jnp.where(col <= row, s, -jnp.inf)
        m_old = m_sc[...]
        m_new = jnp.maximum(m_old, jnp.max(s, axis=1, keepdims=True))
        alpha = jnp.exp(m_old - m_new)
        p = jnp.exp(s - m_new)
        l_sc[...] = alpha * l_sc[...] + jnp.sum(p, axis=1, keepdims=True)
        acc_sc[...] = alpha * acc_sc[...] + jnp.dot(p.astype(BF16), v, preferred_element_type=F32)
        m_sc[...] = m_new

    def body(ki, carry):
        step(ki, False)
        return carry

    lax.fori_loop(0, qi, body, 0)
    step(qi, True)

    lam = _lambda(lam_ref, lam_init)
    o = acc_sc[0:tq, :] / l_sc[0:tq, :] - lam * (acc_sc[tq:2 * tq, :] / l_sc[tq:2 * tq, :])
    ms = jnp.mean(o * o, axis=1, keepdims=True)
    o_ref[...] = o * lax.rsqrt(ms + SUBLN_EPS) * g_ref[...] * (1.0 - lam_init)


def _attn_prompt(q_bf, kt_bf, v_bf, lam_p, subln_g, lam_init, tq):
    b, s, d = q_bf.shape
    nh = d // V_DIM
    return pl.pallas_call(
        functools.partial(_attn_kernel, tq=tq, lam_init=lam_init),
        grid=(b, nh, s // tq),
        in_specs=[pl.BlockSpec(lam_p.shape, lambda bi, h, qi: (0, 0)),
                  pl.BlockSpec((None, tq, V_DIM), lambda bi, h, qi: (bi, qi, h)),
                  pl.BlockSpec((None, V_DIM, s), lambda bi, h, qi: (bi, h, 0)),
                  pl.BlockSpec((None, s, V_DIM), lambda bi, h, qi: (bi, 0, h)),
                  pl.BlockSpec((1, V_DIM), lambda bi, h, qi: (0, 0))],
        out_specs=pl.BlockSpec((None, tq, V_DIM), lambda bi, h, qi: (bi, qi, h)),
        out_shape=jax.ShapeDtypeStruct((b, s, d), F32),
        scratch_shapes=[pltpu.VMEM((2 * tq, V_DIM), BF16),
                        pltpu.VMEM((2 * tq, 1), F32),
                        pltpu.VMEM((2 * tq, 1), F32),
                        pltpu.VMEM((2 * tq, V_DIM), F32)],
        compiler_params=_params("arbitrary", "arbitrary", "arbitrary"),
        name="attn_prompt",
    )(lam_p, q_bf, kt_bf, v_bf, subln_g.reshape(1, V_DIM))


def _attn_sample_kernel(pt_ref, lam_ref, q_ref, kn_ref, vn_ref, g_ref, *refs, pg, lam_init):
    del pt_ref
    k_refs, v_refs = refs[:pg], refs[pg:2 * pg]
    o_ref, qbd_sc, m_sc, l_sc, acc_sc = refs[2 * pg:]
    j = pl.program_id(1)
    nrow, d = qbd_sc.shape
    row = lax.broadcasted_iota(jnp.int32, (nrow, d), 0)
    col = lax.broadcasted_iota(jnp.int32, (nrow, d), 1)

    @pl.when(j == 0)
    def _():
        sel = ((col >> 7) == (row & (N_HEADS - 1))) & (((col >> 6) & 1) == (row >> 3))
        qb = jnp.broadcast_to(q_ref[...].astype(F32), (nrow, d))
        qbd_sc[...] = jnp.where(sel, qb, 0.0).astype(BF16)
        m_sc[...] = jnp.full(m_sc.shape, -jnp.inf, F32)
        l_sc[...] = jnp.zeros(l_sc.shape, F32)
        acc_sc[...] = jnp.zeros(acc_sc.shape, F32)

    for t in range(pg):
        kt = k_refs[t][...].astype(BF16)
        v = v_refs[t][...].astype(BF16)
        s = jnp.dot(qbd_sc[...], kt, preferred_element_type=F32)
        m_old = m_sc[...]
        m_new = jnp.maximum(m_old, jnp.max(s, axis=1, keepdims=True))
        alpha = jnp.exp(m_old - m_new)
        p = jnp.exp(s - m_new)
        l_sc[...] = alpha * l_sc[...] + jnp.sum(p, axis=1, keepdims=True)
        acc_sc[...] = alpha * acc_sc[...] + jnp.dot(p.astype(BF16), v, preferred_element_type=F32)
        m_sc[...] = m_new

    @pl.when(j == pl.num_programs(1) - 1)
    def _():
        s_new = jnp.sum(qbd_sc[...].astype(F32) * kn_ref[...], axis=1, keepdims=True)
        m_old = m_sc[...]
        m_new = jnp.maximum(m_old, s_new)
        alpha = jnp.exp(m_old - m_new)
        p_new = jnp.exp(s_new - m_new)
        l = alpha * l_sc[...] + p_new
        acc = alpha * acc_sc[...] + p_new * vn_ref[...]
        lam = _lambda(lam_ref, lam_init)
        o = acc[0:N_HEADS] / l[0:N_HEADS] - lam * (acc[N_HEADS:] / l[N_HEADS:])
        own = ((lax.broadcasted_iota(jnp.int32, o.shape, 1) >> 7)
               == lax.broadcasted_iota(jnp.int32, o.shape, 0))
        o = jnp.where(own, o, 0.0)
        ms = jnp.sum(o * o, axis=1, keepdims=True) * (1.0 / V_DIM)
        o = o * lax.rsqrt(ms + SUBLN_EPS)
        o_ref[...] = jnp.sum(o, axis=0, keepdims=True) * g_ref[...] * (1.0 - lam_init)


def _attn_sample(q_bf, k_new, v_new, cache_k, cache_v, page_table, lam_p, subln_g, lam_init, pg):
    db, d = q_bf.shape
    n_pages = page_table.shape[1]
    n_phys, page = cache_k.shape[0], cache_k.shape[1]
    ck = jnp.transpose(cache_k, (0, 2, 3, 4, 1)).reshape(n_phys, d, page)
    cv = cache_v.reshape(n_phys, page, d)
    pt = page_table.reshape(-1)

    def page_spec(t, shape):
        return pl.BlockSpec((None,) + shape, lambda b, j, pt_ref: (pt_ref[b * n_pages + j * pg + t], 0, 0))

    row_spec = pl.BlockSpec((None, 1, d), lambda b, j, pt_ref: (b, 0, 0))
    grid_spec = pltpu.PrefetchScalarGridSpec(
        num_scalar_prefetch=1,
        grid=(db, n_pages // pg),
        in_specs=[pl.BlockSpec(lam_p.shape, lambda b, j, pt_ref: (0, 0)),
                  row_spec, row_spec, row_spec,
                  pl.BlockSpec((1, d), lambda b, j, pt_ref: (0, 0))]
                 + [page_spec(t, (d, page)) for t in range(pg)] + [page_spec(t, (page, d)) for t in range(pg)],
        out_specs=row_spec,
        scratch_shapes=[pltpu.VMEM((2 * N_HEADS, d), BF16),
                        pltpu.VMEM((2 * N_HEADS, 1), F32),
                        pltpu.VMEM((2 * N_HEADS, 1), F32),
                        pltpu.VMEM((2 * N_HEADS, d), F32)],
    )
    out = pl.pallas_call(
        functools.partial(_attn_sample_kernel, pg=pg, lam_init=lam_init),
        grid_spec=grid_spec,
        out_shape=jax.ShapeDtypeStruct((db, 1, d), F32),
        compiler_params=_params("arbitrary", "arbitrary"),
        name="attn_sample",
    )(pt, lam_p, q_bf.reshape(db, 1, d), k_new.reshape(db, 1, d), v_new.reshape(db, 1, d),
      jnp.tile(subln_g, N_HEADS).reshape(1, d), *([ck] * pg), *([cv] * pg))
    return out.reshape(db, d)


def _pool_prompt_kernel(u_ref, halo_ref, o_ref, ext_sc, *, tm):
    i = pl.program_id(1)
    hb = halo_ref.shape[0]

    @pl.when(i == 0)
    def _():
        ext_sc[0:hb, :] = jnp.zeros((hb, ext_sc.shape[1]), F32)

    @pl.when(i > 0)
    def _():
        ext_sc[0:hb, :] = halo_ref[...]

    ext_sc[hb:hb + tm, :] = u_ref[...]
    pos = i * tm + lax.broadcasted_iota(jnp.int32, (tm, POOL_GROUP_DIM), 0)
    for g, win in enumerate(POOL_WINDOWS):
        sl = slice(g * POOL_GROUP_DIM, (g + 1) * POOL_GROUP_DIM)
        cur = ext_sc[hb:hb + tm, sl]
        tot = cur
        for j in range(1, win):
            tot = tot + ext_sc[hb - j:hb - j + tm, sl]
        cnt = jnp.minimum(win, pos + 1).astype(F32)
        o_ref[:, sl] = tot / cnt - cur


def _pool_prompt(u, tm):
    b, s, pw = u.shape
    hb = 16
    return pl.pallas_call(
        functools.partial(_pool_prompt_kernel, tm=tm),
        grid=(b, s // tm),
        in_specs=[pl.BlockSpec((None, tm, pw), lambda bi, i: (bi, i, 0)),
                  pl.BlockSpec((None, hb, pw), lambda bi, i: (bi, jnp.maximum(i * (tm // hb) - 1, 0), 0))],
        out_specs=pl.BlockSpec((None, tm, pw), lambda bi, i: (bi, i, 0)),
        out_shape=jax.ShapeDtypeStruct((b, s, pw), F32),
        scratch_shapes=[pltpu.VMEM((hb + tm, pw), F32)],
        compiler_params=_params("arbitrary", "arbitrary"),
        name="pool_prompt",
    )(u, u)


def _pool_sample_kernel(st_ref, u_ref, o_ref, np_ref, *, start_pos):
    nb = st_ref.shape[0]
    u = u_ref[...]
    for g, win in enumerate(POOL_WINDOWS):
        sl = slice(g * POOL_GROUP_DIM, (g + 1) * POOL_GROUP_DIM)
        cur = u[:, sl]
        tot = cur
        for j in range(1, win):
            tot = tot + st_ref[nb - j, :, sl]
        o_ref[:, sl] = tot / float(min(win, start_pos + 1)) - cur
    for j in range(nb - 1):
        np_ref[j] = st_ref[j + 1]
    np_ref[nb - 1] = u


def _pool_sample(state, u, start_pos):
    db, nb, pw = state.shape
    full = lambda shape: pl.BlockSpec(shape, lambda i: (0,) * len(shape))
    pooled, new_pool = pl.pallas_call(
        functools.partial(_pool_sample_kernel, start_pos=start_pos),
        grid=(1,),
        in_specs=[full((nb, db, pw)), full((db, pw))],
        out_specs=[full((db, pw)), full((nb, db, pw))],
        out_shape=[jax.ShapeDtypeStruct((db, pw), F32), jax.ShapeDtypeStruct((nb, db, pw), F32)],
        compiler_params=_params("arbitrary"),
        name="pool_sample",
    )(jnp.transpose(state, (1, 0, 2)), u)
    return pooled, jnp.transpose(new_pool, (1, 0, 2))


def _route(lt):
    g = [lt[j:j + 1] for j in range(N_GROUPS)]
    gmax = functools.reduce(jnp.maximum, g)
    gidx = jnp.full(gmax.shape, N_GROUPS - 1, jnp.int32)
    for j in range(N_GROUPS - 2, -1, -1):
        gidx = jnp.where(g[j] == gmax, j, gidx)
    p_g = 1.0 / functools.reduce(jnp.add, [jnp.exp(gj - gmax) for gj in g])
    e = []
    for k in range(EXPERTS_PER_GROUP):
        sel = lt[N_GROUPS + k:N_GROUPS + k + 1]
        for j in range(1, N_GROUPS):
            r = N_GROUPS + j * EXPERTS_PER_GROUP + k
            sel = jnp.where(gidx == j, lt[r:r + 1], sel)
        e.append(sel)

    def first_argmax(vals):
        vmax = functools.reduce(jnp.maximum, vals)
        idx = jnp.full(vmax.shape, len(vals) - 1, jnp.int32)
        for k in range(len(vals) - 2, -1, -1):
            idx = jnp.where(vals[k] == vmax, k, idx)
        return vmax, idx

    v1, i1 = first_argmax(e)
    rest = [jnp.where(i1 == k, -jnp.inf, e[k]) for k in range(EXPERTS_PER_GROUP)]
    v2, i2 = first_argmax(rest)
    t = jnp.exp(v2 - v1)
    w1 = p_g / (1.0 + t)
    w2 = p_g * t / (1.0 + t)
    within = [jnp.where(i1 == k, w1, 0.0) + jnp.where(i2 == k, w2, 0.0) for k in range(EXPERTS_PER_GROUP)]
    rows = [jnp.where(gidx == j, within[k], 0.0) for j in range(N_GROUPS) for k in range(EXPERTS_PER_GROUP)]
    return jnp.concatenate(rows, axis=0)


def _merge_kernel(pooled_ref, gates_ref, attn_ref, x_ref, g1_ref, sc2_ref, sh2_ref,
                  wpool_ref, pscale_ref, wout_ref, lng_ref, lnb_ref, wr_ref, br_ref,
                  x1_ref, h2_ref, comb_ref, *, alpha):
    d = x_ref.shape[1]
    tm = x_ref.shape[0]
    pooled = pooled_ref[...].astype(BF16)
    parts = [jnp.dot(pooled[:, g * POOL_GROUP_DIM:(g + 1) * POOL_GROUP_DIM], wpool_ref[g],
                     preferred_element_type=F32) for g in range(len(POOL_WINDOWS))]
    pool_out = jnp.concatenate(parts, axis=1) * pscale_ref[...]
    merged = gates_ref[:, 0:d] * pool_out + gates_ref[:, d:2 * d] * attn_ref[...]
    y = jnp.dot(merged.astype(BF16), wout_ref[...], preferred_element_type=F32)
    x1 = _layer_norm(alpha * x_ref[...] + (1.0 + g1_ref[...]) * y, lng_ref[...], lnb_ref[...])
    x1_ref[...] = x1
    h2 = x1 * (1.0 + sc2_ref[...]) + sh2_ref[...]
    h2_ref[...] = h2.astype(BF16)
    lt = lax.dot_general(wr_ref[...], h2, NT_DIMS, preferred_element_type=F32,
                         precision=lax.Precision.HIGHEST) + br_ref[...]
    comb_t = _route(lt)
    comb_t = jnp.concatenate([comb_t, jnp.zeros((LANES - N_EXPERTS, tm), F32)], axis=0)
    comb_ref[...] = comb_t.T


def _merge(pooled, gates, attn, x, mod_specs, mod_arrs, w_pool_bf, pool_scale, w_out_bf,
           ln_g, ln_b, w_r, b_r, alpha, tm):
    t, d = x.shape
    pw = pooled.shape[1]
    row = lambda width: pl.BlockSpec((tm, width), lambda i: (i, 0))
    const = lambda shape: pl.BlockSpec(shape, lambda i: (0,) * len(shape))
    return pl.pallas_call(
        functools.partial(_merge_kernel, alpha=alpha),
        grid=(t // tm,),
        in_specs=[row(pw), row(2 * d), row(d), row(d), *mod_specs,
                  const(w_pool_bf.shape), const((1, d)), const((d, d)), const((1, d)), const((1, d)),
                  const(w_r.shape), const(b_r.shape)],
        out_specs=[row(d), row(d), row(LANES)],
        out_shape=[jax.ShapeDtypeStruct((t, d), F32), jax.ShapeDtypeStruct((t, d), BF16),
                   jax.ShapeDtypeStruct((t, LANES), F32)],
        compiler_params=_params("arbitrary"),
        name="merge",
    )(pooled, gates, attn, x, *mod_arrs, w_pool_bf, pool_scale.reshape(1, d), w_out_bf,
      ln_g.reshape(1, d), ln_b.reshape(1, d), w_r, b_r)


def _moe_kernel(h2_ref, comb_ref, wg_ref, wu_ref, wd_ref, x1_ref, g2_ref, lng_ref, lnb_ref,
                o_ref, acc_sc, *, alpha):
    j = pl.program_id(1)
    eb = wg_ref.shape[0]

    @pl.when(j == 0)
    def _():
        acc_sc[...] = jnp.zeros(acc_sc.shape, F32)

    h2 = h2_ref[...]
    comb = comb_ref[...]
    lane = lax.broadcasted_iota(jnp.int32, comb.shape, 1)
    for e in range(eb):
        hg = jnp.dot(h2, wg_ref[e], preferred_element_type=F32)
        hu = jnp.dot(h2, wu_ref[e], preferred_element_type=F32)
        c = jnp.sum(jnp.where(lane == j * eb + e, comb, 0.0), axis=1, keepdims=True)
        act = (hg * _sigmoid(hg)) * hu * c
        acc_sc[...] += jnp.dot(act.astype(BF16), wd_ref[e], preferred_element_type=F32)

    @pl.when(j == pl.num_programs(1) - 1)
    def _():
        y = alpha * x1_ref[...] + (1.0 + g2_ref[...]) * acc_sc[...]
        o_ref[...] = _layer_norm(y, lng_ref[...], lnb_ref[...])


def _moe(h2, comb, wg_bf, wu_bf, wd_bf, x1, g2_spec, g2_arr, ln_g, ln_b, alpha, tm, eb):
    t, d = x1.shape
    ne, _, ff = wg_bf.shape
    row = lambda width: pl.BlockSpec((tm, width), lambda i, j: (i, 0))
    const = lambda shape: pl.BlockSpec(shape, lambda i, j: (0,) * len(shape))
    return pl.pallas_call(
        functools.partial(_moe_kernel, alpha=alpha),
        grid=(t // tm, ne // eb),
        in_specs=[row(d), row(LANES),
                  pl.BlockSpec((eb, d, ff), lambda i, j: (j, 0, 0)),
                  pl.BlockSpec((eb, d, ff), lambda i, j: (j, 0, 0)),
                  pl.BlockSpec((eb, ff, d), lambda i, j: (j, 0, 0)),
                  row(d), g2_spec, const((1, d)), const((1, d))],
        out_specs=row(d),
        out_shape=jax.ShapeDtypeStruct((t, d), F32),
        scratch_shapes=[pltpu.VMEM((tm, d), F32)],
        compiler_params=_params("arbitrary", "arbitrary"),
        name="moe_ffn",
    )(h2, comb, wg_bf, wu_bf, wd_bf, x1, g2_arr, ln_g.reshape(1, d), ln_b.reshape(1, d))


def kernel(x_prompt, x_sample, cache_k, cache_v, state_pool, page_table, c_prompt, c_sample, w_ada, b_ada, w_in, lambda_q1, lambda_k1, lambda_q2, lambda_k2, subln_g, w_pool, pool_scale, w_out, ln1_g, ln1_b, w_group, b_group, w_router, b_router, w_e_gate, w_e_up, w_e_down, ln2_g, ln2_b):
    depth = w_in.shape[0]
    bsz, seq, d = x_prompt.shape
    db, dseq, _ = x_sample.shape
    assert dseq == 1, "sample group decodes one token per sequence"
    n_pages, page = page_table.shape[1], cache_k.shape[2]
    past_len = n_pages * page
    alpha = (2.0 * depth) ** 0.25
    tp, ts = bsz * seq, db * dseq

    TM_PROJ, TQ, TM_POOL, TM_MERGE, TM_MOE, E_BLK, PAGES_PER_STEP = 256, 512, 512, 512, 1024, 4, 4

    xp = x_prompt.reshape(tp, d)
    xs = x_sample.reshape(ts, d)
    n_c = bsz + db
    n_c_pad = -(-n_c // 8) * 8
    c_all = jnp.concatenate([c_prompt, c_sample, jnp.zeros((n_c_pad - n_c, d), F32)], axis=0)
    tabs_p = _rope_tables(jnp.arange(seq))
    tabs_s = _rope_tables(jnp.full((ts,), past_len))

    outs = {k: [] for k in ("kp", "vp", "pp", "ks", "vs", "ps")}
    for l in range(depth):
        lam_init = 0.8 - 0.6 * math.exp(-0.3 * l)
        mod = _ada(c_all, w_ada[l], b_ada[l])
        mod_p = mod[:bsz].reshape(bsz * N_MOD, 1, d)
        mod_s = mod[bsz:n_c]
        w_in_bf = w_in[l].astype(BF16)
        wkt_bf = w_in_bf[:, d:2 * d].T
        w_pool_bf = w_pool[l].astype(BF16)
        w_out_bf = w_out[l].astype(BF16)
        wg_bf, wu_bf, wd_bf = (w[l].astype(BF16) for w in (w_e_gate, w_e_up, w_e_down))
        lam_p = jnp.stack([lambda_q1[l], lambda_k1[l], lambda_q2[l], lambda_k2[l]])
        n_r = N_GROUPS + N_EXPERTS
        w_r = jnp.concatenate([w_group[l], w_router[l]], axis=1).T
        w_r = jnp.concatenate([w_r, jnp.zeros((ROUTER_ROWS - n_r, d), F32)], axis=0)
        b_r = jnp.concatenate([b_group[l], b_router[l], jnp.zeros((ROUTER_ROWS - n_r,), F32)]).reshape(ROUTER_ROWS, 1)

        def pmod(k, tm, two_d=False):
            per_b = seq // tm
            if two_d:
                return pl.BlockSpec((None, 1, d), lambda i, j: ((i // per_b) * N_MOD + k, 0, 0))
            return pl.BlockSpec((None, 1, d), lambda i: ((i // per_b) * N_MOD + k, 0, 0))

        nblk = seq // TM_PROJ
        tab_spec = pl.BlockSpec((TM_PROJ, LANES), lambda i: (i % nblk, 0))
        tab_t_spec = pl.BlockSpec((HEAD_DIM // 2, TM_PROJ), lambda i: (0, i % nblk))
        q_bf, kt_f, kt_bf, v_f, v_bf, u, gates = _inproj(
            xp, pmod(1, TM_PROJ), pmod(0, TM_PROJ), mod_p, mod_p, w_in_bf, wkt_bf,
            tabs_p[0], tab_spec, tabs_p[1], tab_t_spec, TM_PROJ, bsz, False)
        attn = _attn_prompt(q_bf.reshape(bsz, seq, d), kt_bf, v_bf.reshape(bsz, seq, d),
                            lam_p, subln_g[l], lam_init, TQ).reshape(tp, d)
        u3 = u.reshape(bsz, seq, -1)
        pooled = _pool_prompt(u3, TM_POOL).reshape(tp, -1)
        x1, h2, comb = _merge(pooled, gates, attn, xp,
                              [pmod(2, TM_MERGE), pmod(4, TM_MERGE), pmod(3, TM_MERGE)], [mod_p] * 3,
                              w_pool_bf, pool_scale[l], w_out_bf, ln1_g[l], ln1_b[l], w_r, b_r, alpha, TM_MERGE)
        xp = _moe(h2, comb, wg_bf, wu_bf, wd_bf, x1, pmod(5, TM_MOE, True), mod_p,
                  ln2_g[l], ln2_b[l], alpha, TM_MOE, E_BLK)
        outs["kp"].append(jnp.transpose(kt_f.reshape(bsz, N_HEADS, 2, HEAD_DIM, seq), (0, 4, 1, 2, 3)))
        outs["vp"].append(v_f.reshape(bsz, seq, N_HEADS, V_DIM))
        outs["pp"].append(u3[:, seq - POOL_BUF:])

        def smod(k, two_d=False):
            if two_d:
                return pl.BlockSpec((ts, d), lambda i, j: (0, k))
            return pl.BlockSpec((ts, d), lambda i: (0, k))

        tab_spec_s = pl.BlockSpec((ts, LANES), lambda i: (0, 0))
        tab_t_spec_s = pl.BlockSpec((HEAD_DIM // 2, ts), lambda i: (0, 0))
        q_bf, kt_f, _, v_f, _, u, gates, k_rows = _inproj(
            xs, smod(1), smod(0), mod_s, mod_s, w_in_bf, wkt_bf,
            tabs_s[0], tab_spec_s, tabs_s[1], tab_t_spec_s, ts, 1, True)
        attn = _attn_sample(q_bf, k_rows, v_f, cache_k[l], cache_v[l], page_table, lam_p, subln_g[l],
                            lam_init, PAGES_PER_STEP)
        pooled, new_pool = _pool_sample(state_pool[l], u, past_len)
        x1, h2, comb = _merge(pooled, gates, attn, xs, [smod(2), smod(4), smod(3)], [mod_s] * 3,
                              w_pool_bf, pool_scale[l], w_out_bf, ln1_g[l], ln1_b[l], w_r, b_r, alpha, ts)
        xs = _moe(h2, comb, wg_bf, wu_bf, wd_bf, x1, smod(5, True), mod_s,
                  ln2_g[l], ln2_b[l], alpha, ts, E_BLK)
        outs["ks"].append(jnp.transpose(kt_f.reshape(N_HEADS, 2, HEAD_DIM, db, dseq), (3, 4, 0, 1, 2)))
        outs["vs"].append(v_f.reshape(db, dseq, N_HEADS, V_DIM))
        outs["ps"].append(new_pool)

    return (xp.reshape(bsz, seq, d), xs.reshape(db, dseq, d),
            jnp.stack(outs["kp"]), jnp.stack(outs["vp"]), jnp.stack(outs["pp"]),
            jnp.stack(outs["ks"]), jnp.stack(outs["vs"]), jnp.stack(outs["ps"]))
```

```python
import functools
import math

import jax
import jax.numpy as jnp
from jax import lax
from jax.experimental import pallas as pl
from jax.experimental.pallas import tpu as pltpu

F32 = jnp.float32
BF16 = jnp.bfloat16

LANES = 128
VMEM_LIMIT = 56 << 20

N_MOD = 6
N_HEADS = 8
HEAD_DIM = 64
V_DIM = 2 * HEAD_DIM
ROPE_THETA = 10000.0
POOL_WINDOWS = (2, 4, 8, 16)
POOL_GROUP_DIM = 128
POOL_BUF = max(POOL_WINDOWS) - 1
N_GROUPS = 4
EXPERTS_PER_GROUP = 4
N_EXPERTS = N_GROUPS * EXPERTS_PER_GROUP
LN_EPS = 1e-5
SUBLN_EPS = 1e-5
ROUTER_ROWS = 32

NT_DIMS = (((1,), (1,)), ((), ()))


def _params(*sem):
    return pltpu.CompilerParams(dimension_semantics=sem, vmem_limit_bytes=VMEM_LIMIT)


def _sigmoid(x):
    return 1.0 / (1.0 + jnp.exp(-x))


def _layer_norm(x, g, b):
    mu = jnp.mean(x, axis=-1, keepdims=True)
    xc = x - mu
    var = jnp.mean(xc * xc, axis=-1, keepdims=True)
    return xc * lax.rsqrt(var + LN_EPS) * g + b


def _lambda(lam_ref, lam_init):
    lp = lam_ref[...]
    a = jnp.sum(lp[0:1] * lp[1:2], axis=1, keepdims=True)
    b = jnp.sum(lp[2:3] * lp[3:4], axis=1, keepdims=True)
    return jnp.exp(a) - jnp.exp(b) + lam_init


def _ada_kernel(c_ref, w_ref, b_ref, o_ref):
    c = c_ref[...]
    o_ref[...] = jnp.dot(c * _sigmoid(c), w_ref[...], preferred_element_type=F32,
                         precision=lax.Precision.HIGHEST) + b_ref[...]


def _ada(c_all, w_ada, b_ada):
    n, d = c_all.shape
    cols = w_ada.shape[1]
    return pl.pallas_call(
        _ada_kernel,
        grid=(cols // d,),
        in_specs=[pl.BlockSpec((n, d), lambda j: (0, 0)),
                  pl.BlockSpec((d, d), lambda j: (0, j)),
                  pl.BlockSpec((1, d), lambda j: (0, j))],
        out_specs=pl.BlockSpec((n, d), lambda j: (0, j)),
        out_shape=jax.ShapeDtypeStruct((n, cols), F32),
        compiler_params=_params("arbitrary"),
        name="ada_mod",
    )(c_all, w_ada, b_ada.reshape(1, cols))


def _inproj_kernel(x_ref, sc_ref, sh_ref, w_ref, wt_ref, cos_ref, sa_ref, sb_ref, cost_ref, sint_ref,
                   *out_refs, d, pw, prompt):
    mm_dtype = w_ref.dtype
    prec = lax.Precision.HIGHEST if mm_dtype == F32 else None
    dot = functools.partial(jnp.dot, preferred_element_type=F32, precision=prec)
    dot_nt = functools.partial(lax.dot_general, dimension_numbers=NT_DIMS, preferred_element_type=F32,
                               precision=prec)
    h = (x_ref[...] * (1.0 + sc_ref[...]) + sh_ref[...]).astype(mm_dtype)
    half = HEAD_DIM // 2
    scale = HEAD_DIM ** -0.5 * (math.log2(math.e) if prompt else 1.0)
    cost, sint = cost_ref[...], sint_ref[...]

    def rope_rows(xt):
        out = []
        for g in range(d // HEAD_DIM):
            lo = slice(g * HEAD_DIM, g * HEAD_DIM + half)
            hi = slice(g * HEAD_DIM + half, (g + 1) * HEAD_DIM)
            x1, x2 = xt[lo], xt[hi]
            out.append((lo, x1 * cost - x2 * sint))
            out.append((hi, x2 * cost + x1 * sint))
        return out

    qt = rope_rows(dot_nt(wt_ref[0:d, :], h))
    kt = rope_rows(dot_nt(wt_ref[d:2 * d, :], h))
    v = dot(h, w_ref[:, 2 * d:3 * d])
    if prompt:
        qt_ref, ktf_ref, kb_ref, vf_ref, vtb_ref, u_ref, g_ref = out_refs
        for sl, r in qt:
            qt_ref[sl, :] = (r * scale).astype(BF16)
        for sl, r in kt:
            ktf_ref[sl, :] = r
        cos, sa, sb = cos_ref[...], sa_ref[...], sb_ref[...]
        k = dot(h, w_ref[:, d:2 * d])
        for c in range(d // LANES):
            blk = k[:, c * LANES:(c + 1) * LANES]
            kb_ref[:, c * LANES:(c + 1) * LANES] = (
                blk * cos + pltpu.roll(blk, LANES - half, 1) * sa + pltpu.roll(blk, half, 1) * sb).astype(BF16)
        vtb_ref[...] = dot_nt(wt_ref[2 * d:3 * d, :], h).astype(BF16)
    else:
        qt_ref, ktf_ref, sn_ref, vf_ref, u_ref, g_ref = out_refs
        for idx, ((sl, rq), (_, rk)) in enumerate(zip(qt, kt)):
            qs = rq * scale
            qt_ref[sl, :] = qs
            ktf_ref[sl, :] = rk
            part = jnp.sum(qs * rk, axis=0, keepdims=True)
            g, is_hi = idx // 2, idx % 2
            head, which = g // 2, g % 2
            if is_hi:
                sn_ref[which, head:head + 1, :] += part
            else:
                sn_ref[which, head:head + 1, :] = part
    vf_ref[...] = v
    u_ref[...] = dot(h, w_ref[:, 3 * d:3 * d + pw])
    g_ref[...] = _sigmoid(dot(h, w_ref[:, 3 * d + pw:]))


def _inproj(x, sc_spec, sh_spec, sc_arr, sh_arr, w_in, wt, tabs, tab_spec, tabs_t, tab_t_spec,
            tm, n_seq, prompt):
    t, d = x.shape
    cols = w_in.shape[1]
    pw = cols - 5 * d
    per_seq = t // n_seq
    nblk = per_seq // tm
    row = lambda width: pl.BlockSpec((tm, width), lambda i: (i, 0))
    ft_spec = pl.BlockSpec((None, d, tm), lambda i: (i // nblk, 0, i % nblk))
    ft_shape = lambda dt: jax.ShapeDtypeStruct((n_seq, d, per_seq), dt)
    tok_shape = lambda width, dt: jax.ShapeDtypeStruct((t, width), dt)
    if prompt:
        out_specs = [ft_spec, ft_spec, row(d), row(d), ft_spec, row(pw), row(2 * d)]
        out_shape = [ft_shape(BF16), ft_shape(F32), tok_shape(d, BF16), tok_shape(d, F32), ft_shape(BF16),
                     tok_shape(pw, F32), tok_shape(2 * d, F32)]
    else:
        assert n_seq == 1 and nblk == 1
        out_specs = [ft_spec, ft_spec, pl.BlockSpec((2, N_HEADS, tm), lambda i: (0, 0, 0)), row(d),
                     row(pw), row(2 * d)]
        out_shape = [ft_shape(F32), ft_shape(F32), jax.ShapeDtypeStruct((2, N_HEADS, t), F32),
                     tok_shape(d, F32), tok_shape(pw, F32), tok_shape(2 * d, F32)]
    return pl.pallas_call(
        functools.partial(_inproj_kernel, d=d, pw=pw, prompt=prompt),
        grid=(t // tm,),
        in_specs=[row(d), sc_spec, sh_spec,
                  pl.BlockSpec((d, cols), lambda i: (0, 0), pipeline_mode=pl.Buffered(1)),
                  pl.BlockSpec(wt.shape, lambda i: (0, 0), pipeline_mode=pl.Buffered(1)),
                  tab_spec, tab_spec, tab_spec, tab_t_spec, tab_t_spec],
        out_specs=out_specs,
        out_shape=out_shape,
        compiler_params=_params("arbitrary"),
        name="in_proj",
    )(x, sc_arr, sh_arr, w_in, wt, *tabs, *tabs_t)


def _rope_tables(pos):
    half = HEAD_DIM // 2
    inv = ROPE_THETA ** (-jnp.arange(half, dtype=F32) / half)
    ang = pos.astype(F32)[:, None] * inv[None, :]
    c, s = jnp.cos(ang), jnp.sin(ang)
    reps = LANES // half
    cos = jnp.tile(c, (1, reps))
    sin = jnp.tile(s, (1, reps))
    lo = (jnp.arange(LANES) % HEAD_DIM) < half
    return (cos, jnp.where(lo, -sin, 0.0), jnp.where(lo, 0.0, sin)), (c.T, s.T)


def _attn_kernel(lam_ref, qt_ref, k_ref, vt_ref, g_ref, o_ref, q2_sc, sta_sc, stb_sc, m_sc, l_sc, acc_sc,
                 *, tq, lam_init):
    qi = pl.program_id(2)
    zero = jnp.zeros((HEAD_DIM, tq), BF16)
    q2_sc[0:HEAD_DIM, 0:tq] = qt_ref[0:HEAD_DIM, :]
    q2_sc[HEAD_DIM:V_DIM, 0:tq] = zero
    q2_sc[0:HEAD_DIM, tq:2 * tq] = zero
    q2_sc[HEAD_DIM:V_DIM, tq:2 * tq] = qt_ref[HEAD_DIM:V_DIM, :]
    m_sc[...] = jnp.full(m_sc.shape, -jnp.inf, F32)
    l_sc[...] = jnp.zeros(l_sc.shape, F32)
    acc_sc[...] = jnp.zeros(acc_sc.shape, F32)

    def scores(ki, st_ref):
        start = pl.multiple_of(ki * tq, tq)
        st_ref[...] = jnp.dot(k_ref[pl.ds(start, tq), :], q2_sc[...], preferred_element_type=F32)

    def consume(ki, st_ref, masked):
        st = st_ref[...]
        if masked:
            key = lax.broadcasted_iota(jnp.int32, st.shape, 0)
            qry = lax.broadcasted_iota(jnp.int32, st.shape, 1)
            qry = jnp.where(qry >= tq, qry - tq, qry)
            st = jnp.where(key <= qry, st, -jnp.inf)
        m_old = m_sc[...]
        m_new = jnp.maximum(m_old, jnp.max(st, axis=0, keepdims=True))
        alpha = jnp.exp2(m_old - m_new)
        p = jnp.exp2(st - m_new)
        l_sc[...] = alpha * l_sc[...] + jnp.sum(p, axis=0, keepdims=True)
        start = pl.multiple_of(ki * tq, tq)
        vt = vt_ref[:, pl.ds(start, tq)]
        acc_sc[...] = alpha * acc_sc[...] + jnp.dot(vt, p.astype(BF16), preferred_element_type=F32)
        m_sc[...] = m_new

    scores(0, sta_sc)

    def pair(pi, carry):
        k0 = 2 * pi
        scores(k0 + 1, stb_sc)
        consume(k0, sta_sc, False)
        scores(k0 + 2, sta_sc)
        consume(k0 + 1, stb_sc, False)
        return carry

    lax.fori_loop(0, qi // 2, pair, 0)

    @pl.when(qi % 2 == 0)
    def _():
        consume(qi, sta_sc, True)

    @pl.when(qi % 2 == 1)
    def _():
        scores(qi, stb_sc)
        consume(qi - 1, sta_sc, False)
        consume(qi, stb_sc, True)

    lam = _lambda(lam_ref, lam_init)
    inv_l = 1.0 / l_sc[...]
    ot = acc_sc[:, 0:tq] * inv_l[:, 0:tq] - lam * (acc_sc[:, tq:2 * tq] * inv_l[:, tq:2 * tq])
    ms = jnp.mean(ot * ot, axis=0, keepdims=True)
    ot = ot * lax.rsqrt(ms + SUBLN_EPS) * g_ref[...] * (1.0 - lam_init)
    o_ref[...] = ot.T


def _attn_prompt(qt_bf, k_bf, vt_bf, lam_p, subln_g, lam_init, tq):
    b, d, s = qt_bf.shape
    nh = d // V_DIM
    return pl.pallas_call(
        functools.partial(_attn_kernel, tq=tq, lam_init=lam_init),
        grid=(b, nh, s // tq),
        in_specs=[pl.BlockSpec(lam_p.shape, lambda bi, h, qi: (0, 0)),
                  pl.BlockSpec((None, V_DIM, tq), lambda bi, h, qi: (bi, h, qi)),
                  pl.BlockSpec((None, s, V_DIM), lambda bi, h, qi: (bi, 0, h)),
                  pl.BlockSpec((None, V_DIM, s), lambda bi, h, qi: (bi, h, 0)),
                  pl.BlockSpec((V_DIM, 1), lambda bi, h, qi: (0, 0))],
        out_specs=pl.BlockSpec((None, tq, V_DIM), lambda bi, h, qi: (bi, qi, h)),
        out_shape=jax.ShapeDtypeStruct((b, s, d), F32),
        scratch_shapes=[pltpu.VMEM((V_DIM, 2 * tq), BF16),
                        pltpu.VMEM((tq, 2 * tq), F32),
                        pltpu.VMEM((tq, 2 * tq), F32),
                        pltpu.VMEM((1, 2 * tq), F32),
                        pltpu.VMEM((1, 2 * tq), F32),
                        pltpu.VMEM((V_DIM, 2 * tq), F32)],
        compiler_params=_params("arbitrary", "arbitrary", "arbitrary"),
        name="attn_prompt",
    )(lam_p, qt_bf, k_bf, vt_bf, subln_g.reshape(V_DIM, 1))


def _attn_sample_kernel(pt_ref, lam_ref, qt_ref, sn_ref, vn_ref, g_ref, *refs, n_pages, lam_init):
    del pt_ref
    kt_refs, v_refs = refs[:n_pages], refs[n_pages:2 * n_pages]
    o_ref, s_sc = refs[2 * n_pages:]
    b = pl.program_id(0)
    d, n_tok = qt_ref.shape
    page = kt_refs[0].shape[1]

    onehot = jnp.where(lax.broadcasted_iota(jnp.int32, (n_tok, page), 0) == b, 1.0, 0.0)
    qb = jnp.dot(qt_ref[...], onehot, preferred_element_type=F32,
                 precision=lax.Precision.HIGHEST)
    for p in range(n_pages):
        prod = (kt_refs[p][...] * qb).reshape(N_HEADS, V_DIM, page)
        s_sc[0, :, p * page:(p + 1) * page] = jnp.sum(prod[:, 0:HEAD_DIM, :], axis=1)
        s_sc[1, :, p * page:(p + 1) * page] = jnp.sum(prod[:, HEAD_DIM:V_DIM, :], axis=1)

    own = lax.broadcasted_iota(jnp.int32, (N_HEADS, n_tok), 1) == b

    def softmax_parts(which):
        s_new = jnp.max(jnp.where(own, sn_ref[which], -jnp.inf), axis=1, keepdims=True)
        s = s_sc[which]
        m = jnp.maximum(jnp.max(s, axis=1, keepdims=True), s_new)
        p = jnp.exp(s - m)
        p_new = jnp.exp(s_new - m)
        inv_l = 1.0 / (jnp.sum(p, axis=1, keepdims=True) + p_new)
        return p * inv_l, p_new * inv_l

    lam = _lambda(lam_ref, lam_init)
    w1, wn1 = softmax_parts(0)
    w2, wn2 = softmax_parts(1)
    w = w1 - lam * w2
    acc = (wn1 - lam * wn2) * vn_ref[...]
    for p in range(n_pages):
        for t in range(page):
            c = p * page + t
            acc = acc + w[:, c:c + 1] * v_refs[p][t]
    ms = jnp.mean(acc * acc, axis=1, keepdims=True)
    o_ref[...] = acc * lax.rsqrt(ms + SUBLN_EPS) * g_ref[...] * (1.0 - lam_init)


def _attn_sample(qt, s_new, v_new, cache_k, cache_v, page_table, lam_p, subln_g, lam_init):
    d, db = qt.shape
    n_pages = page_table.shape[1]
    n_phys, page = cache_k.shape[0], cache_k.shape[1]
    ck = jnp.transpose(cache_k, (0, 2, 3, 4, 1)).reshape(n_phys, d, page)
    pt = page_table.reshape(-1)

    def page_spec(p, shape):
        zeros = (0,) * len(shape)
        return pl.BlockSpec((None,) + shape, lambda b, pt_ref: (pt_ref[b * n_pages + p],) + zeros)

    const = lambda shape: pl.BlockSpec(shape, lambda b, pt_ref: (0,) * len(shape))
    head_spec = pl.BlockSpec((None, N_HEADS, V_DIM), lambda b, pt_ref: (b, 0, 0))
    grid_spec = pltpu.PrefetchScalarGridSpec(
        num_scalar_prefetch=1,
        grid=(db,),
        in_specs=[const(lam_p.shape), const((d, db)), const((2, N_HEADS, db)), head_spec, const((1, V_DIM))]
                 + [page_spec(p, (d, page)) for p in range(n_pages)]
                 + [page_spec(p, (page, N_HEADS, V_DIM)) for p in range(n_pages)],
        out_specs=head_spec,
        scratch_shapes=[pltpu.VMEM((2, N_HEADS, n_pages * page), F32)],
    )
    out = pl.pallas_call(
        functools.partial(_attn_sample_kernel, n_pages=n_pages, lam_init=lam_init),
        grid_spec=grid_spec,
        out_shape=jax.ShapeDtypeStruct((db, N_HEADS, V_DIM), F32),
        compiler_params=_params("arbitrary"),
        name="attn_sample",
    )(pt, lam_p, qt, s_new, v_new.reshape(db, N_HEADS, V_DIM), subln_g.reshape(1, V_DIM),
      *([ck] * n_pages), *([cache_v] * n_pages))
    return out.reshape(db, d)


def _pool_prompt_kernel(u_ref, halo_ref, o_ref, ext_sc, *, tm):
    i = pl.program_id(1)
    hb = halo_ref.shape[0]

    @pl.when(i == 0)
    def _():
        ext_sc[0:hb, :] = jnp.zeros((hb, ext_sc.shape[1]), F32)

    @pl.when(i > 0)
    def _():
        ext_sc[0:hb, :] = halo_ref[...]

    ext_sc[hb:hb + tm, :] = u_ref[...]
    pos = i * tm + lax.broadcasted_iota(jnp.int32, (tm, POOL_GROUP_DIM), 0)
    for g, win in enumerate(POOL_WINDOWS):
        sl = slice(g * POOL_GROUP_DIM, (g + 1) * POOL_GROUP_DIM)
        cur = ext_sc[hb:hb + tm, sl]
        tot = cur
        for j in range(1, win):
            tot = tot + ext_sc[hb - j:hb - j + tm, sl]
        cnt = jnp.minimum(win, pos + 1).astype(F32)
        o_ref[:, sl] = tot / cnt - cur


def _pool_prompt(u, tm):
    b, s, pw = u.shape
    hb = 16
    return pl.pallas_call(
        functools.partial(_pool_prompt_kernel, tm=tm),
        grid=(b, s // tm),
        in_specs=[pl.BlockSpec((None, tm, pw), lambda bi, i: (bi, i, 0)),
                  pl.BlockSpec((None, hb, pw), lambda bi, i: (bi, jnp.maximum(i * (tm // hb) - 1, 0), 0))],
        out_specs=pl.BlockSpec((None, tm, pw), lambda bi, i: (bi, i, 0)),
        out_shape=jax.ShapeDtypeStruct((b, s, pw), F32),
        scratch_shapes=[pltpu.VMEM((hb + tm, pw), F32)],
        compiler_params=_params("arbitrary", "arbitrary"),
        name="pool_prompt",
    )(u, u)


def _pool_sample_kernel(st_ref, u_ref, o_ref, np_ref, *, start_pos):
    nb = st_ref.shape[0]
    u = u_ref[...]
    for g, win in enumerate(POOL_WINDOWS):
        sl = slice(g * POOL_GROUP_DIM, (g + 1) * POOL_GROUP_DIM)
        cur = u[:, sl]
        tot = cur
        for j in range(1, win):
            tot = tot + st_ref[nb - j, :, sl]
        o_ref[:, sl] = tot / float(min(win, start_pos + 1)) - cur
    for j in range(nb - 1):
        np_ref[j] = st_ref[j + 1]
    np_ref[nb - 1] = u


def _pool_sample(state, u, start_pos):
    db, nb, pw = state.shape
    full = lambda shape: pl.BlockSpec(shape, lambda i: (0,) * len(shape))
    pooled, new_pool = pl.pallas_call(
        functools.partial(_pool_sample_kernel, start_pos=start_pos),
        grid=(1,),
        in_specs=[full((nb, db, pw)), full((db, pw))],
        out_specs=[full((db, pw)), full((nb, db, pw))],
        out_shape=[jax.ShapeDtypeStruct((db, pw), F32), jax.ShapeDtypeStruct((nb, db, pw), F32)],
        compiler_params=_params("arbitrary"),
        name="pool_sample",
    )(jnp.transpose(state, (1, 0, 2)), u)
    return pooled, jnp.transpose(new_pool, (1, 0, 2))


def _route(lt):
    g = [lt[j:j + 1] for j in range(N_GROUPS)]
    gmax = functools.reduce(jnp.maximum, g)
    gidx = jnp.full(gmax.shape, N_GROUPS - 1, jnp.int32)
    for j in range(N_GROUPS - 2, -1, -1):
        gidx = jnp.where(g[j] == gmax, j, gidx)
    p_g = 1.0 / functools.reduce(jnp.add, [jnp.exp(gj - gmax) for gj in g])
    e = []
    for k in range(EXPERTS_PER_GROUP):
        sel = lt[N_GROUPS + k:N_GROUPS + k + 1]
        for j in range(1, N_GROUPS):
            r = N_GROUPS + j * EXPERTS_PER_GROUP + k
            sel = jnp.where(gidx == j, lt[r:r + 1], sel)
        e.append(sel)

    def first_argmax(vals):
        vmax = functools.reduce(jnp.maximum, vals)
        idx = jnp.full(vmax.shape, len(vals) - 1, jnp.int32)
        for k in range(len(vals) - 2, -1, -1):
            idx = jnp.where(vals[k] == vmax, k, idx)
        return vmax, idx

    v1, i1 = first_argmax(e)
    rest = [jnp.where(i1 == k, -jnp.inf, e[k]) for k in range(EXPERTS_PER_GROUP)]
    v2, i2 = first_argmax(rest)
    t = jnp.exp(v2 - v1)
    w1 = p_g / (1.0 + t)
    w2 = p_g * t / (1.0 + t)
    within = [jnp.where(i1 == k, w1, 0.0) + jnp.where(i2 == k, w2, 0.0) for k in range(EXPERTS_PER_GROUP)]
    rows = [jnp.where(gidx == j, within[k], 0.0) for j in range(N_GROUPS) for k in range(EXPERTS_PER_GROUP)]
    return jnp.concatenate(rows, axis=0)


def _merge_kernel(pooled_ref, gates_ref, attn_ref, x_ref, g1_ref, sc2_ref, sh2_ref,
                  wpool_ref, pscale_ref, wout_ref, lng_ref, lnb_ref, wr_ref, br_ref,
                  x1_ref, h2_ref, comb_ref, *, alpha):
    d = x_ref.shape[1]
    tm = x_ref.shape[0]
    mm_dtype = wout_ref.dtype
    dot = functools.partial(jnp.dot, preferred_element_type=F32,
                            precision=lax.Precision.HIGHEST if mm_dtype == F32 else None)
    pooled = pooled_ref[...].astype(mm_dtype)
    parts = [dot(pooled[:, g * POOL_GROUP_DIM:(g + 1) * POOL_GROUP_DIM], wpool_ref[g])
             for g in range(len(POOL_WINDOWS))]
    pool_out = jnp.concatenate(parts, axis=1) * pscale_ref[...]
    merged = gates_ref[:, 0:d] * pool_out + gates_ref[:, d:2 * d] * attn_ref[...]
    y = dot(merged.astype(mm_dtype), wout_ref[...])
    x1 = _layer_norm(alpha * x_ref[...] + (1.0 + g1_ref[...]) * y, lng_ref[...], lnb_ref[...])
    x1_ref[...] = x1
    h2 = x1 * (1.0 + sc2_ref[...]) + sh2_ref[...]
    h2_ref[...] = h2.astype(BF16)
    lt = lax.dot_general(wr_ref[...], h2, NT_DIMS, preferred_element_type=F32,
                         precision=lax.Precision.HIGHEST) + br_ref[...]
    comb_t = _route(lt)
    comb_t = jnp.concatenate([comb_t, jnp.zeros((LANES - N_EXPERTS, tm), F32)], axis=0)
    comb_ref[...] = comb_t.T


def _merge(pooled, gates, attn, x, mod_specs, mod_arrs, w_pool, pool_scale, w_out,
           ln_g, ln_b, w_r, b_r, alpha, tm):
    t, d = x.shape
    pw = pooled.shape[1]
    row = lambda width: pl.BlockSpec((tm, width), lambda i: (i, 0))
    const = lambda shape: pl.BlockSpec(shape, lambda i: (0,) * len(shape))
    return pl.pallas_call(
        functools.partial(_merge_kernel, alpha=alpha),
        grid=(t // tm,),
        in_specs=[row(pw), row(2 * d), row(d), row(d), *mod_specs,
                  const(w_pool.shape), const((1, d)), const((d, d)), const((1, d)), const((1, d)),
                  const(w_r.shape), const(b_r.shape)],
        out_specs=[row(d), row(d), row(LANES)],
        out_shape=[jax.ShapeDtypeStruct((t, d), F32), jax.ShapeDtypeStruct((t, d), BF16),
                   jax.ShapeDtypeStruct((t, LANES), F32)],
        compiler_params=_params("arbitrary"),
        name="merge",
    )(pooled, gates, attn, x, *mod_arrs, w_pool, pool_scale.reshape(1, d), w_out,
      ln_g.reshape(1, d), ln_b.reshape(1, d), w_r, b_r)


def _moe_kernel(h2_ref, comb_ref, wg_ref, wu_ref, wd_ref, x1_ref, g2_ref, lng_ref, lnb_ref,
                o_ref, acc_sc, *, alpha):
    j = pl.program_id(1)
    eb = wg_ref.shape[0]

    @pl.when(j == 0)
    def _():
        acc_sc[...] = jnp.zeros(acc_sc.shape, F32)

    h2 = h2_ref[...]
    comb = comb_ref[...]
    lane = lax.broadcasted_iota(jnp.int32, comb.shape, 1)
    for e in range(eb):
        hg = jnp.dot(h2, wg_ref[e], preferred_element_type=F32)
        hu = jnp.dot(h2, wu_ref[e], preferred_element_type=F32)
        c = jnp.sum(jnp.where(lane == j * eb + e, comb, 0.0), axis=1, keepdims=True)
        act = (hg * _sigmoid(hg)) * hu * c
        acc_sc[...] += jnp.dot(act.astype(BF16), wd_ref[e], preferred_element_type=F32)

    @pl.when(j == pl.num_programs(1) - 1)
    def _():
        y = alpha * x1_ref[...] + (1.0 + g2_ref[...]) * acc_sc[...]
        o_ref[...] = _layer_norm(y, lng_ref[...], lnb_ref[...])


def _moe(h2, comb, wg_bf, wu_bf, wd_bf, x1, g2_spec, g2_arr, ln_g, ln_b, alpha, tm, eb):
    t, d = x1.shape
    ne, _, ff = wg_bf.shape
    row = lambda width: pl.BlockSpec((tm, width), lambda i, j: (i, 0))
    const = lambda shape: pl.BlockSpec(shape, lambda i, j: (0,) * len(shape))
    return pl.pallas_call(
        functools.partial(_moe_kernel, alpha=alpha),
        grid=(t // tm, ne // eb),
        in_specs=[row(d), row(LANES),
                  pl.BlockSpec((eb, d, ff), lambda i, j: (j, 0, 0)),
                  pl.BlockSpec((eb, d, ff), lambda i, j: (j, 0, 0)),
                  pl.BlockSpec((eb, ff, d), lambda i, j: (j, 0, 0)),
                  row(d), g2_spec, const((1, d)), const((1, d))],
        out_specs=row(d),
        out_shape=jax.ShapeDtypeStruct((t, d), F32),
        scratch_shapes=[pltpu.VMEM((tm, d), F32)],
        compiler_params=_params("arbitrary", "arbitrary"),
        name="moe_ffn",
    )(h2, comb, wg_bf, wu_bf, wd_bf, x1, g2_arr, ln_g.reshape(1, d), ln_b.reshape(1, d))


def kernel(x_prompt, x_sample, cache_k, cache_v, state_pool, page_table, c_prompt, c_sample, w_ada, b_ada, w_in, lambda_q1, lambda_k1, lambda_q2, lambda_k2, subln_g, w_pool, pool_scale, w_out, ln1_g, ln1_b, w_group, b_group, w_router, b_router, w_e_gate, w_e_up, w_e_down, ln2_g, ln2_b):
    depth = w_in.shape[0]
    bsz, seq, d = x_prompt.shape
    db, dseq, _ = x_sample.shape
    assert dseq == 1, "sample group decodes one token per sequence"
    n_pages, page = page_table.shape[1], cache_k.shape[2]
    past_len = n_pages * page
    alpha = (2.0 * depth) ** 0.25
    tp, ts = bsz * seq, db * dseq

    TM_PROJ, TQ, TM_POOL, TM_MERGE, TM_MOE, E_BLK = 256, 512, 512, 512, 1024, 4

    xp = x_prompt.reshape(tp, d)
    xs = x_sample.reshape(ts, d)
    n_c = bsz + db
    n_c_pad = -(-n_c // 8) * 8
    c_all = jnp.concatenate([c_prompt, c_sample, jnp.zeros((n_c_pad - n_c, d), F32)], axis=0)
    tabs_p = _rope_tables(jnp.arange(seq))
    tabs_s = _rope_tables(jnp.full((ts,), past_len))

    outs = {k: [] for k in ("kp", "vp", "pp", "ks", "vs", "ps")}
    for l in range(depth):
        lam_init = 0.8 - 0.6 * math.exp(-0.3 * l)
        mod = _ada(c_all, w_ada[l], b_ada[l])
        mod_p = mod[:bsz].reshape(bsz * N_MOD, 1, d)
        mod_s = mod[bsz:n_c]
        w_in_bf = w_in[l].astype(BF16)
        wt_bf = w_in_bf[:, 0:3 * d].T
        w_pool_bf = w_pool[l].astype(BF16)
        w_out_bf = w_out[l].astype(BF16)
        wg_bf, wu_bf, wd_bf = (w[l].astype(BF16) for w in (w_e_gate, w_e_up, w_e_down))
        lam_p = jnp.stack([lambda_q1[l], lambda_k1[l], lambda_q2[l], lambda_k2[l]])
        n_r = N_GROUPS + N_EXPERTS
        w_r = jnp.concatenate([w_group[l], w_router[l]], axis=1).T
        w_r = jnp.concatenate([w_r, jnp.zeros((ROUTER_ROWS - n_r, d), F32)], axis=0)
        b_r = jnp.concatenate([b_group[l], b_router[l], jnp.zeros((ROUTER_ROWS - n_r,), F32)]).reshape(ROUTER_ROWS, 1)

        def pmod(k, tm, two_d=False):
            per_b = seq // tm
            if two_d:
                return pl.BlockSpec((None, 1, d), lambda i, j: ((i // per_b) * N_MOD + k, 0, 0))
            return pl.BlockSpec((None, 1, d), lambda i: ((i // per_b) * N_MOD + k, 0, 0))

        nblk = seq // TM_PROJ
        tab_spec = pl.BlockSpec((TM_PROJ, LANES), lambda i: (i % nblk, 0))
        tab_t_spec = pl.BlockSpec((HEAD_DIM // 2, TM_PROJ), lambda i: (0, i % nblk))
        qt_bf, kt_f, k_bf, v_f, vt_bf, u, gates = _inproj(
            xp, pmod(1, TM_PROJ), pmod(0, TM_PROJ), mod_p, mod_p, w_in_bf, wt_bf,
            tabs_p[0], tab_spec, tabs_p[1], tab_t_spec, TM_PROJ, bsz, True)
        attn = _attn_prompt(qt_bf, k_bf.reshape(bsz, seq, d), vt_bf,
                            lam_p, subln_g[l], lam_init, TQ).reshape(tp, d)
        u3 = u.reshape(bsz, seq, -1)
        pooled = _pool_prompt(u3, TM_POOL).reshape(tp, -1)
        x1, h2, comb = _merge(pooled, gates, attn, xp,
                              [pmod(2, TM_MERGE), pmod(4, TM_MERGE), pmod(3, TM_MERGE)], [mod_p] * 3,
                              w_pool_bf, pool_scale[l], w_out_bf, ln1_g[l], ln1_b[l], w_r, b_r, alpha, TM_MERGE)
        xp = _moe(h2, comb, wg_bf, wu_bf, wd_bf, x1, pmod(5, TM_MOE, True), mod_p,
                  ln2_g[l], ln2_b[l], alpha, TM_MOE, E_BLK)
        outs["kp"].append(jnp.transpose(kt_f.reshape(bsz, N_HEADS, 2, HEAD_DIM, seq), (0, 4, 1, 2, 3)))
        outs["vp"].append(v_f.reshape(bsz, seq, N_HEADS, V_DIM))
        outs["pp"].append(u3[:, seq - POOL_BUF:])

        def smod(k, two_d=False):
            if two_d:
                return pl.BlockSpec((ts, d), lambda i, j: (0, k))
            return pl.BlockSpec((ts, d), lambda i: (0, k))

        tab_spec_s = pl.BlockSpec((ts, LANES), lambda i: (0, 0))
        tab_t_spec_s = pl.BlockSpec((HEAD_DIM // 2, ts), lambda i: (0, 0))
        qt_s, kt_f, s_new, v_f, u, gates = _inproj(
            xs, smod(1), smod(0), mod_s, mod_s, w_in[l], w_in[l][:, 0:2 * d].T,
            tabs_s[0], tab_spec_s, tabs_s[1], tab_t_spec_s, ts, 1, False)
        attn = _attn_sample(qt_s[0], s_new, v_f, cache_k[l], cache_v[l], page_table, lam_p, subln_g[l],
                            lam_init)
        pooled, new_pool = _pool_sample(state_pool[l], u, past_len)
        x1, h2, comb = _merge(pooled, gates, attn, xs, [smod(2), smod(4), smod(3)], [mod_s] * 3,
                              w_pool[l], pool_scale[l], w_out[l], ln1_g[l], ln1_b[l], w_r, b_r, alpha, ts)
        xs = _moe(h2, comb, wg_bf, wu_bf, wd_bf, x1, smod(5, True), mod_s,
                  ln2_g[l], ln2_b[l], alpha, ts, E_BLK)
        outs["ks"].append(jnp.transpose(kt_f.reshape(N_HEADS, 2, HEAD_DIM, db, dseq), (3, 4, 0, 1, 2)))
        outs["vs"].append(v_f.reshape(db, dseq, N_HEADS, V_DIM))
        outs["ps"].append(new_pool)

    return (xp.reshape(bsz, seq, d), xs.reshape(db, dseq, d),
            jnp.stack(outs["kp"]), jnp.stack(outs["vp"]), jnp.stack(outs["pp"]),
            jnp.stack(outs["ks"]), jnp.stack(outs["vs"]), jnp.stack(outs["ps"]))
```

```python
import functools
import math

import jax
import jax.numpy as jnp
from jax import lax
from jax.experimental import pallas as pl
from jax.experimental.pallas import tpu as pltpu

F32 = jnp.float32
BF16 = jnp.bfloat16

LANES = 128
VMEM_LIMIT = 56 << 20

N_MOD = 6
N_HEADS = 8
HEAD_DIM = 64
V_DIM = 2 * HEAD_DIM
ROPE_THETA = 10000.0
POOL_WINDOWS = (2, 4, 8, 16)
POOL_GROUP_DIM = 128
POOL_BUF = max(POOL_WINDOWS) - 1
N_GROUPS = 4
EXPERTS_PER_GROUP = 4
N_EXPERTS = N_GROUPS * EXPERTS_PER_GROUP
LN_EPS = 1e-5
SUBLN_EPS = 1e-5
ROUTER_ROWS = 32

NT_DIMS = (((1,), (1,)), ((), ()))


def _params(*sem):
    return pltpu.CompilerParams(dimension_semantics=sem, vmem_limit_bytes=VMEM_LIMIT)


def _sigmoid(x):
    return 1.0 / (1.0 + jnp.exp(-x))


def _layer_norm(x, g, b):
    mu = jnp.mean(x, axis=-1, keepdims=True)
    xc = x - mu
    var = jnp.mean(xc * xc, axis=-1, keepdims=True)
    return xc * lax.rsqrt(var + LN_EPS) * g + b


def _lambda(lam_ref, lam_init):
    lp = lam_ref[...]
    a = jnp.sum(lp[0:1] * lp[1:2], axis=1, keepdims=True)
    b = jnp.sum(lp[2:3] * lp[3:4], axis=1, keepdims=True)
    return jnp.exp(a) - jnp.exp(b) + lam_init


def _ada_kernel(c_ref, w_ref, b_ref, o_ref):
    c = c_ref[...]
    o_ref[...] = jnp.dot(c * _sigmoid(c), w_ref[...], preferred_element_type=F32,
                         precision=lax.Precision.HIGHEST) + b_ref[...]


def _ada(c_all, w_ada, b_ada):
    n, d = c_all.shape
    cols = w_ada.shape[1]
    return pl.pallas_call(
        _ada_kernel,
        grid=(cols // d,),
        in_specs=[pl.BlockSpec((n, d), lambda j: (0, 0)),
                  pl.BlockSpec((d, d), lambda j: (0, j)),
                  pl.BlockSpec((1, d), lambda j: (0, j))],
        out_specs=pl.BlockSpec((n, d), lambda j: (0, j)),
        out_shape=jax.ShapeDtypeStruct((n, cols), F32),
        compiler_params=_params("arbitrary"),
        name="ada_mod",
    )(c_all, w_ada, b_ada.reshape(1, cols))


def _inproj_kernel(x_ref, sc_ref, sh_ref, w_ref, cost_ref, sint_ref, *out_refs, d, pw, prompt):
    mm_dtype = w_ref.dtype
    prec = lax.Precision.HIGHEST if mm_dtype == F32 else None
    dot = functools.partial(jnp.dot, preferred_element_type=F32, precision=prec)
    h =(x_ref[...] * (1.0 + sc_ref[...]) + sh_ref[...]).astype(mm_dtype)
    half = HEAD_DIM // 2
    scale = HEAD_DIM ** -0.5 * (math.log2(math.e) if prompt else 1.0)
    cost, sint = cost_ref[...], sint_ref[...]

    def rope_rows(xt):
        out = []
        for g in range(d // HEAD_DIM):
            lo = slice(g * HEAD_DIM, g * HEAD_DIM + half)
            hi = slice(g * HEAD_DIM + half, (g + 1) * HEAD_DIM)
            x1, x2 = xt[lo], xt[hi]
            out.append((lo, x1 * cost - x2 * sint))
            out.append((hi, x2 * cost + x1 * sint))
        return out

    qt = rope_rows(dot(h, w_ref[:, 0:d]).T)
    kt = rope_rows(dot(h, w_ref[:, d:2 * d]).T)
    v = dot(h, w_ref[:, 2 * d:3 * d])
    if prompt:
        qt_ref, ktf_ref, kb_ref, vf_ref, vtb_ref, u_ref, g_ref = out_refs
        for sl, r in qt:
            qt_ref[sl, :] = (r * scale).astype(BF16)
        for sl, r in kt:
            ktf_ref[sl, :] = r
        kb_ref[...] = ktf_ref[...].T.astype(BF16)
        vtb_ref[...] = v.T.astype(BF16)
        g_dtype = BF16
    else:
        g_dtype = F32
        qt_ref, ktf_ref, sn_ref, vf_ref, u_ref, g_ref = out_refs
        for idx, ((sl, rq), (_, rk)) in enumerate(zip(qt, kt)):
            qs = rq * scale
            qt_ref[sl, :] = qs
            ktf_ref[sl, :] = rk
            part = jnp.sum(qs * rk, axis=0, keepdims=True)
            g, is_hi = idx // 2, idx % 2
            head, which = g // 2, g % 2
            if is_hi:
                sn_ref[which, head:head + 1, :] += part
            else:
                sn_ref[which, head:head + 1, :] = part
    vf_ref[...] = v
    u_ref[...] = dot(h, w_ref[:, 3 * d:3 * d + pw])
    g_ref[...] = _sigmoid(dot(h, w_ref[:, 3 * d + pw:])).astype(g_dtype)


def _inproj(x, sc_spec, sh_spec, sc_arr, sh_arr, w_in, tabs_t, tab_t_spec, tm, n_seq, prompt):
    t, d = x.shape
    cols = w_in.shape[1]
    pw = cols - 5 * d
    per_seq = t // n_seq
    nblk = per_seq // tm
    row = lambda width: pl.BlockSpec((tm, width), lambda i: (i, 0))
    ft_spec = pl.BlockSpec((None, d, tm), lambda i: (i // nblk, 0, i % nblk))
    ft_shape = lambda dt: jax.ShapeDtypeStruct((n_seq, d, per_seq), dt)
    tok_shape = lambda width, dt: jax.ShapeDtypeStruct((t, width), dt)
    if prompt:
        out_specs = [ft_spec, ft_spec, row(d), row(d), ft_spec, row(pw), row(2 * d)]
        out_shape = [ft_shape(BF16), ft_shape(F32), tok_shape(d, BF16), tok_shape(d, F32), ft_shape(BF16),
                     tok_shape(pw, F32), tok_shape(2 * d, BF16)]
    else:
        assert n_seq == 1 and nblk == 1
        out_specs = [ft_spec, ft_spec, pl.BlockSpec((2, N_HEADS, tm), lambda i: (0, 0, 0)), row(d),
                     row(pw), row(2 * d)]
        out_shape = [ft_shape(F32), ft_shape(F32), jax.ShapeDtypeStruct((2, N_HEADS, t), F32),
                     tok_shape(d, F32), tok_shape(pw, F32), tok_shape(2 * d, F32)]
    return pl.pallas_call(
        functools.partial(_inproj_kernel, d=d, pw=pw, prompt=prompt),
        grid=(t // tm,),
        in_specs=[row(d), sc_spec, sh_spec,
                  pl.BlockSpec((d, cols), lambda i: (0, 0), pipeline_mode=pl.Buffered(1)),
                  tab_t_spec, tab_t_spec],
        out_specs=out_specs,
        out_shape=out_shape,
        compiler_params=_params("arbitrary"),
        name="in_proj",
    )(x, sc_arr, sh_arr, w_in, *tabs_t)


def _rope_tables(pos):
    half = HEAD_DIM // 2
    inv = ROPE_THETA ** (-jnp.arange(half, dtype=F32) / half)
    ang = inv[:, None] * pos.astype(F32)[None, :]
    return jnp.cos(ang), jnp.sin(ang)


def _attn_kernel(lam_ref, qt_ref, k_ref, vt_ref, g_ref, o_ref, q2_sc, sta_sc, stb_sc, m_sc, l_sc, acc_sc,
                 *, tq, lam_init):
    qi = pl.program_id(2)
    zero = jnp.zeros((HEAD_DIM, tq), BF16)
    q2_sc[0:HEAD_DIM, 0:tq] = qt_ref[0:HEAD_DIM, :]
    q2_sc[HEAD_DIM:V_DIM, 0:tq] = zero
    q2_sc[0:HEAD_DIM, tq:2 * tq] = zero
    q2_sc[HEAD_DIM:V_DIM, tq:2 * tq] = qt_ref[HEAD_DIM:V_DIM, :]
    m_sc[...] = jnp.full(m_sc.shape, -jnp.inf, F32)
    l_sc[...] = jnp.zeros(l_sc.shape, F32)
    acc_sc[...] = jnp.zeros(acc_sc.shape, F32)

    def scores(ki, st_ref):
        start = pl.multiple_of(ki * tq, tq)
        st_ref[...] = jnp.dot(k_ref[pl.ds(start, tq), :], q2_sc[...], preferred_element_type=F32)

    def consume(ki, st_ref, masked):
        st = st_ref[...]
        if masked:
            key = lax.broadcasted_iota(jnp.int32, st.shape, 0)
            qry = lax.broadcasted_iota(jnp.int32, st.shape, 1)
            qry = jnp.where(qry >= tq, qry - tq, qry)
            st = jnp.where(key <= qry, st, -jnp.inf)
        m_old = m_sc[...]
        m_new = jnp.maximum(m_old, jnp.max(st, axis=0, keepdims=True))
        alpha = jnp.exp2(m_old - m_new)
        p = jnp.exp2(st - m_new).astype(BF16)
        start = pl.multiple_of(ki * tq, tq)
        vt = vt_ref[:, pl.ds(start, tq)]
        vt1 = jnp.concatenate([vt, jnp.ones((16, tq), BF16)], axis=0)
        r = jnp.dot(vt1, p, preferred_element_type=F32)
        l_sc[...] = alpha * l_sc[...] + r[V_DIM:V_DIM + 1]
        acc_sc[...] = alpha * acc_sc[...] + r[0:V_DIM]
        m_sc[...] = m_new

    scores(0, sta_sc)

    def pair(pi, carry):
        k0 = 2 * pi
        scores(k0 + 1, stb_sc)
        consume(k0, sta_sc, False)
        scores(k0 + 2, sta_sc)
        consume(k0 + 1, stb_sc, False)
        return carry

    lax.fori_loop(0, qi // 2, pair, 0)

    @pl.when(qi % 2 == 0)
    def _():
        consume(qi, sta_sc, True)

    @pl.when(qi % 2 == 1)
    def _():
        scores(qi, stb_sc)
        consume(qi - 1, sta_sc, False)
        consume(qi, stb_sc, True)

    lam = _lambda(lam_ref, lam_init)
    inv_l = 1.0 / l_sc[...]
    ot = acc_sc[:, 0:tq] * inv_l[:, 0:tq] - lam * (acc_sc[:, tq:2 * tq] * inv_l[:, tq:2 * tq])
    ms = jnp.mean(ot * ot, axis=0, keepdims=True)
    ot = ot * lax.rsqrt(ms + SUBLN_EPS) * g_ref[...] * (1.0 - lam_init)
    o_ref[...] = ot.T.astype(o_ref.dtype)


def _attn_prompt(qt_bf, k_bf, vt_bf, lam_p, subln_g, lam_init, tq):
    b, d, s = qt_bf.shape
    nh = d // V_DIM
    return pl.pallas_call(
        functools.partial(_attn_kernel, tq=tq, lam_init=lam_init),
        grid=(b, nh, s // tq),
        in_specs=[pl.BlockSpec(lam_p.shape, lambda bi, h, qi: (0, 0)),
                  pl.BlockSpec((None, V_DIM, tq), lambda bi, h, qi: (bi, h, qi)),
                  pl.BlockSpec((None, s, V_DIM), lambda bi, h, qi: (bi, 0, h)),
                  pl.BlockSpec((None, V_DIM, s), lambda bi, h, qi: (bi, h, 0)),
                  pl.BlockSpec((V_DIM, 1), lambda bi, h, qi: (0, 0))],
        out_specs=pl.BlockSpec((None, tq, V_DIM), lambda bi, h, qi: (bi, qi, h)),
        out_shape=jax.ShapeDtypeStruct((b, s, d), BF16),
        scratch_shapes=[pltpu.VMEM((V_DIM, 2 * tq), BF16),
                        pltpu.VMEM((tq, 2 * tq), F32),
                        pltpu.VMEM((tq, 2 * tq), F32),
                        pltpu.VMEM((1, 2 * tq), F32),
                        pltpu.VMEM((1, 2 * tq), F32),
                        pltpu.VMEM((V_DIM, 2 * tq), F32)],
        compiler_params=_params("arbitrary", "arbitrary", "arbitrary"),
        name="attn_prompt",
    )(lam_p, qt_bf, k_bf, vt_bf, subln_g.reshape(V_DIM, 1))


def _attn_sample_kernel(pt_ref, lam_ref, qt_ref, sn_ref, vn_ref, g_ref, *refs, n_pages, lam_init):
    del pt_ref
    kt_refs, v_refs = refs[:n_pages], refs[n_pages:2 * n_pages]
    o_ref, s_sc = refs[2 * n_pages:]
    b = pl.program_id(0)
    d, n_tok = qt_ref.shape
    page = kt_refs[0].shape[1]

    onehot = jnp.where(lax.broadcasted_iota(jnp.int32, (n_tok, page), 0) == b, 1.0, 0.0)
    qb = jnp.dot(qt_ref[...], onehot, preferred_element_type=F32,
                 precision=lax.Precision.HIGHEST)
    for p in range(n_pages):
        prod = (kt_refs[p][...] * qb).reshape(N_HEADS, V_DIM, page)
        s_sc[0, :, p * page:(p + 1) * page] = jnp.sum(prod[:, 0:HEAD_DIM, :], axis=1)
        s_sc[1, :, p * page:(p + 1) * page] = jnp.sum(prod[:, HEAD_DIM:V_DIM, :], axis=1)

    own = lax.broadcasted_iota(jnp.int32, (N_HEADS, n_tok), 1) == b

    def softmax_parts(which):
        s_new = jnp.max(jnp.where(own, sn_ref[which], -jnp.inf), axis=1, keepdims=True)
        s = s_sc[which]
        m = jnp.maximum(jnp.max(s, axis=1, keepdims=True), s_new)
        p = jnp.exp(s - m)
        p_new = jnp.exp(s_new - m)
        inv_l = 1.0 / (jnp.sum(p, axis=1, keepdims=True) + p_new)
        return p * inv_l, p_new * inv_l

    lam = _lambda(lam_ref, lam_init)
    w1, wn1 = softmax_parts(0)
    w2, wn2 = softmax_parts(1)
    w = w1 - lam * w2
    acc = (wn1 - lam * wn2) * vn_ref[...]
    for p in range(n_pages):
        for t in range(page):
            c = p * page + t
            acc = acc + w[:, c:c + 1] * v_refs[p][t]
    ms = jnp.mean(acc * acc, axis=1, keepdims=True)
    o_ref[...] = acc * lax.rsqrt(ms + SUBLN_EPS) * g_ref[...] * (1.0 - lam_init)


def _attn_sample(qt, s_new, v_new, cache_k, cache_v, page_table, lam_p, subln_g, lam_init):
    d, db = qt.shape
    n_pages = page_table.shape[1]
    n_phys, page = cache_k.shape[0], cache_k.shape[1]
    ck = jnp.transpose(cache_k, (0, 2, 3, 4, 1)).reshape(n_phys, d, page)
    pt = page_table.reshape(-1)

    def page_spec(p, shape):
        zeros = (0,) * len(shape)
        return pl.BlockSpec((None,) + shape, lambda b, pt_ref: (pt_ref[b * n_pages + p],) + zeros)

    const = lambda shape: pl.BlockSpec(shape, lambda b, pt_ref: (0,) * len(shape))
    head_spec = pl.BlockSpec((None, N_HEADS, V_DIM), lambda b, pt_ref: (b, 0, 0))
    grid_spec = pltpu.PrefetchScalarGridSpec(
        num_scalar_prefetch=1,
        grid=(db,),
        in_specs=[const(lam_p.shape), const((d, db)), const((2, N_HEADS, db)), head_spec, const((1, V_DIM))]
                 + [page_spec(p, (d, page)) for p in range(n_pages)]
                 + [page_spec(p, (page, N_HEADS, V_DIM)) for p in range(n_pages)],
        out_specs=head_spec,
        scratch_shapes=[pltpu.VMEM((2, N_HEADS, n_pages * page), F32)],
    )
    out = pl.pallas_call(
        functools.partial(_attn_sample_kernel, n_pages=n_pages, lam_init=lam_init),
        grid_spec=grid_spec,
        out_shape=jax.ShapeDtypeStruct((db, N_HEADS, V_DIM), F32),
        compiler_params=_params("arbitrary"),
        name="attn_sample",
    )(pt, lam_p, qt, s_new, v_new.reshape(db, N_HEADS, V_DIM), subln_g.reshape(1, V_DIM),
      *([ck] * n_pages), *([cache_v] * n_pages))
    return out.reshape(db, d)


POOL_HALO = 16


def _pool_tile(u_ref, halo_ref, ext_sc, seq_tile):
    tm = u_ref.shape[0]
    hb = halo_ref.shape[0]

    @pl.when(seq_tile == 0)
    def _():
        ext_sc[0:hb, :] = jnp.zeros((hb, ext_sc.shape[1]), F32)

    @pl.when(seq_tile > 0)
    def _():
        ext_sc[0:hb, :] = halo_ref[...]

    ext_sc[hb:hb + tm, :] = u_ref[...]
    pos = seq_tile * tm + lax.broadcasted_iota(jnp.int32, (tm, POOL_GROUP_DIM), 0)
    parts = []
    for g, win in enumerate(POOL_WINDOWS):
        sl = slice(g * POOL_GROUP_DIM, (g + 1) * POOL_GROUP_DIM)
        cur = ext_sc[hb:hb + tm, sl]
        tot = cur
        for j in range(1, win):
            tot = tot + ext_sc[hb - j:hb - j + tm, sl]
        cnt = jnp.minimum(win, pos + 1).astype(F32)
        parts.append(tot / cnt - cur)
    return jnp.concatenate(parts, axis=1)


def _pool_sample_kernel(st_ref, u_ref, o_ref, np_ref, *, start_pos):
    nb = st_ref.shape[0]
    u = u_ref[...]
    for g, win in enumerate(POOL_WINDOWS):
        sl = slice(g * POOL_GROUP_DIM, (g + 1) * POOL_GROUP_DIM)
        cur = u[:, sl]
        tot = cur
        for j in range(1, win):
            tot = tot + st_ref[nb - j, :, sl]
        o_ref[:, sl] = tot / float(min(win, start_pos + 1)) - cur
    for j in range(nb - 1):
        np_ref[j] = st_ref[j + 1]
    np_ref[nb - 1] = u


def _pool_sample(state, u, start_pos):
    db, nb, pw = state.shape
    full = lambda shape: pl.BlockSpec(shape, lambda i: (0,) * len(shape))
    pooled, new_pool = pl.pallas_call(
        functools.partial(_pool_sample_kernel, start_pos=start_pos),
        grid=(1,),
        in_specs=[full((nb, db, pw)), full((db, pw))],
        out_specs=[full((db, pw)), full((nb, db, pw))],
        out_shape=[jax.ShapeDtypeStruct((db, pw), F32), jax.ShapeDtypeStruct((nb, db, pw), F32)],
        compiler_params=_params("arbitrary"),
        name="pool_sample",
    )(jnp.transpose(state, (1, 0, 2)), u)
    return pooled, jnp.transpose(new_pool, (1, 0, 2))


def _route(lt):
    g = [lt[j:j + 1] for j in range(N_GROUPS)]
    gmax = functools.reduce(jnp.maximum, g)
    gidx = jnp.full(gmax.shape, N_GROUPS - 1, jnp.int32)
    for j in range(N_GROUPS - 2, -1, -1):
        gidx = jnp.where(g[j] == gmax, j, gidx)
    p_g = 1.0 / functools.reduce(jnp.add, [jnp.exp(gj - gmax) for gj in g])
    e = []
    for k in range(EXPERTS_PER_GROUP):
        sel = lt[N_GROUPS + k:N_GROUPS + k + 1]
        for j in range(1, N_GROUPS):
            r = N_GROUPS + j * EXPERTS_PER_GROUP + k
            sel = jnp.where(gidx == j, lt[r:r + 1], sel)
        e.append(sel)

    def first_argmax(vals):
        vmax = functools.reduce(jnp.maximum, vals)
        idx = jnp.full(vmax.shape, len(vals) - 1, jnp.int32)
        for k in range(len(vals) - 2, -1, -1):
            idx = jnp.where(vals[k] == vmax, k, idx)
        return vmax, idx

    v1, i1 = first_argmax(e)
    rest = [jnp.where(i1 == k, -jnp.inf, e[k]) for k in range(EXPERTS_PER_GROUP)]
    v2, i2 = first_argmax(rest)
    t = jnp.exp(v2 - v1)
    w1 = p_g / (1.0 + t)
    w2 = p_g * t / (1.0 + t)
    within = [jnp.where(i1 == k, w1, 0.0) + jnp.where(i2 == k, w2, 0.0) for k in range(EXPERTS_PER_GROUP)]
    rows = [jnp.where(gidx == j, within[k], 0.0) for j in range(N_GROUPS) for k in range(EXPERTS_PER_GROUP)]
    return jnp.concatenate(rows, axis=0)


def _merge_kernel(*refs, alpha, tiles_per_seq):
    if tiles_per_seq:
        u_ref, halo_ref, *refs, ext_sc = refs
        pooled = _pool_tile(u_ref, halo_ref, ext_sc, pl.program_id(0) % tiles_per_seq)
    else:
        pooled_ref, *refs = refs
        pooled = pooled_ref[...]
    (gates_ref, attn_ref, x_ref, g1_ref, sc2_ref, sh2_ref, wpool_ref, pscale_ref, wout_ref, lng_ref, lnb_ref,
     wr_ref, br_ref, x1_ref, h2_ref, comb_ref) = refs
    d = x_ref.shape[1]
    tm = x_ref.shape[0]
    mm_dtype = wout_ref.dtype
    dot = functools.partial(jnp.dot, preferred_element_type=F32,
                            precision=lax.Precision.HIGHEST if mm_dtype == F32 else None)
    pooled = pooled.astype(mm_dtype)
    parts = [dot(pooled[:, g * POOL_GROUP_DIM:(g + 1) * POOL_GROUP_DIM], wpool_ref[g])
             for g in range(len(POOL_WINDOWS))]
    pool_out = jnp.concatenate(parts, axis=1) * pscale_ref[...]
    merged = (gates_ref[:, 0:d].astype(F32) * pool_out
              + gates_ref[:, d:2 * d].astype(F32) * attn_ref[...].astype(F32))
    y = dot(merged.astype(mm_dtype), wout_ref[...])
    x1 = _layer_norm(alpha * x_ref[...] + (1.0 + g1_ref[...]) * y, lng_ref[...], lnb_ref[...])
    x1_ref[...] = x1
    h2 = x1 * (1.0 + sc2_ref[...]) + sh2_ref[...]
    h2_ref[...] = h2.astype(BF16)
    lt = lax.dot_general(wr_ref[...], h2, NT_DIMS, preferred_element_type=F32,
                         precision=lax.Precision.HIGHEST) + br_ref[...]
    comb_t = _route(lt)
    comb_t = jnp.concatenate([comb_t, jnp.zeros((LANES - N_EXPERTS, tm), F32)], axis=0)
    comb_ref[...] = comb_t.T


def _merge(pool_in, gates, attn, x, mod_specs, mod_arrs, w_pool, pool_scale, w_out,
           ln_g, ln_b, w_r, b_r, alpha, tm, seq_len=None):
    t, d = x.shape
    pw = pool_in.shape[1]
    row = lambda width: pl.BlockSpec((tm, width), lambda i: (i, 0))
    const = lambda shape: pl.BlockSpec(shape, lambda i: (0,) * len(shape))
    if seq_len:
        halo_spec = pl.BlockSpec((POOL_HALO, pw), lambda i: (jnp.maximum(i * (tm // POOL_HALO) - 1, 0), 0))
        pool_specs, pool_args = [row(pw), halo_spec], [pool_in, pool_in]
        scratch = [pltpu.VMEM((POOL_HALO + tm, pw), F32)]
    else:
        pool_specs, pool_args, scratch = [row(pw)], [pool_in], []
    return pl.pallas_call(
        functools.partial(_merge_kernel, alpha=alpha, tiles_per_seq=seq_len // tm if seq_len else None),
        grid=(t // tm,),
        in_specs=[*pool_specs, row(2 * d), row(d), row(d), *mod_specs,
                  const(w_pool.shape), const((1, d)), const((d, d)), const((1, d)), const((1, d)),
                  const(w_r.shape), const(b_r.shape)],
        out_specs=[row(d), row(d), row(LANES)],
        out_shape=[jax.ShapeDtypeStruct((t, d), F32), jax.ShapeDtypeStruct((t, d), BF16),
                   jax.ShapeDtypeStruct((t, LANES), F32)],
        scratch_shapes=scratch,
        compiler_params=_params("arbitrary"),
        name="merge",
    )(*pool_args, gates, attn, x, *mod_arrs, w_pool, pool_scale.reshape(1, d), w_out,
      ln_g.reshape(1, d), ln_b.reshape(1, d), w_r, b_r)


def _moe_kernel(h2_ref, comb_ref, wg_ref, wu_ref, wd_ref, x1_ref, g2_ref, lng_ref, lnb_ref,
                o_ref, acc_sc, *, alpha):
    j = pl.program_id(1)
    eb = wg_ref.shape[0]

    @pl.when(j == 0)
    def _():
        acc_sc[...] = jnp.zeros(acc_sc.shape, F32)

    h2 = h2_ref[...]
    comb = comb_ref[...]
    lane = lax.broadcasted_iota(jnp.int32, comb.shape, 1)
    for e in range(eb):
        hg = jnp.dot(h2, wg_ref[e], preferred_element_type=F32)
        hu = jnp.dot(h2, wu_ref[e], preferred_element_type=F32)
        c = jnp.sum(jnp.where(lane == j * eb + e, comb, 0.0), axis=1, keepdims=True)
        act = (hg * _sigmoid(hg)) * hu * c
        acc_sc[...] += jnp.dot(act.astype(BF16), wd_ref[e], preferred_element_type=F32)

    @pl.when(j == pl.num_programs(1) - 1)
    def _():
        y = alpha * x1_ref[...] + (1.0 + g2_ref[...]) * acc_sc[...]
        o_ref[...] = _layer_norm(y, lng_ref[...], lnb_ref[...])


def _moe(h2, comb, wg_bf, wu_bf, wd_bf, x1, g2_spec, g2_arr, ln_g, ln_b, alpha, tm, eb):
    t, d = x1.shape
    ne, _, ff = wg_bf.shape
    row = lambda width: pl.BlockSpec((tm, width), lambda i, j: (i, 0))
    const = lambda shape: pl.BlockSpec(shape, lambda i, j: (0,) * len(shape))
    return pl.pallas_call(
        functools.partial(_moe_kernel, alpha=alpha),
        grid=(t // tm, ne // eb),
        in_specs=[row(d), row(LANES),
                  pl.BlockSpec((eb, d, ff), lambda i, j: (j, 0, 0)),
                  pl.BlockSpec((eb, d, ff), lambda i, j: (j, 0, 0)),
                  pl.BlockSpec((eb, ff, d), lambda i, j: (j, 0, 0)),
                  row(d), g2_spec, const((1, d)), const((1, d))],
        out_specs=row(d),
        out_shape=jax.ShapeDtypeStruct((t, d), F32),
        scratch_shapes=[pltpu.VMEM((tm, d), F32)],
        compiler_params=_params("arbitrary", "arbitrary"),
        name="moe_ffn",
    )(h2, comb, wg_bf, wu_bf, wd_bf, x1, g2_arr, ln_g.reshape(1, d), ln_b.reshape(1, d))


def kernel(x_prompt, x_sample, cache_k, cache_v, state_pool, page_table, c_prompt, c_sample, w_ada, b_ada, w_in, lambda_q1, lambda_k1, lambda_q2, lambda_k2, subln_g, w_pool, pool_scale, w_out, ln1_g, ln1_b, w_group, b_group, w_router, b_router, w_e_gate, w_e_up, w_e_down, ln2_g, ln2_b):
    depth = w_in.shape[0]
    bsz, seq, d = x_prompt.shape
    db, dseq, _ = x_sample.shape
    assert dseq == 1, "sample group decodes one token per sequence"
    n_pages, page = page_table.shape[1], cache_k.shape[2]
    past_len = n_pages * page
    alpha = (2.0 * depth) ** 0.25
    tp, ts = bsz * seq, db * dseq

    TM_PROJ, TQ, TM_MERGE, TM_MOE, E_BLK = 256, 512, 512, 1024, 4

    xp = x_prompt.reshape(tp, d)
    xs = x_sample.reshape(ts, d)
    n_c = bsz + db
    n_c_pad = -(-n_c // 8) * 8
    c_all = jnp.concatenate([c_prompt, c_sample, jnp.zeros((n_c_pad - n_c, d), F32)], axis=0)
    tabs_p = _rope_tables(jnp.arange(seq))
    tabs_s = _rope_tables(jnp.full((ts,), past_len))

    outs = {k: [] for k in ("kp", "vp", "pp", "ks", "vs", "ps")}
    for l in range(depth):
        lam_init = 0.8 - 0.6 * math.exp(-0.3 * l)
        mod = _ada(c_all, w_ada[l], b_ada[l])
        mod_p = mod[:bsz].reshape(bsz * N_MOD, 1, d)
        mod_s = mod[bsz:n_c]
        w_in_bf = w_in[l].astype(BF16)
        w_pool_bf = w_pool[l].astype(BF16)
        w_out_bf = w_out[l].astype(BF16)
        wg_bf, wu_bf, wd_bf = (w[l].astype(BF16) for w in (w_e_gate, w_e_up, w_e_down))
        lam_p = jnp.stack([lambda_q1[l], lambda_k1[l], lambda_q2[l], lambda_k2[l]])
        n_r = N_GROUPS + N_EXPERTS
        w_r = jnp.concatenate([w_group[l], w_router[l]], axis=1).T
        w_r = jnp.concatenate([w_r, jnp.zeros((ROUTER_ROWS - n_r, d), F32)], axis=0)
        b_r = jnp.concatenate([b_group[l], b_router[l], jnp.zeros((ROUTER_ROWS - n_r,), F32)]).reshape(ROUTER_ROWS, 1)

        def pmod(k, tm, two_d=False):
            per_b = seq // tm
            if two_d:
                return pl.BlockSpec((None, 1, d), lambda i, j: ((i // per_b) * N_MOD + k, 0, 0))
            return pl.BlockSpec((None, 1, d), lambda i: ((i // per_b) * N_MOD + k, 0, 0))

        nblk = seq // TM_PROJ
        tab_t_spec = pl.BlockSpec((HEAD_DIM // 2, TM_PROJ), lambda i: (0, i % nblk))
        qt_bf, kt_f, k_bf, v_f, vt_bf, u, gates = _inproj(
            xp, pmod(1, TM_PROJ), pmod(0, TM_PROJ), mod_p, mod_p, w_in_bf,
            tabs_p, tab_t_spec, TM_PROJ, bsz, True)
        attn = _attn_prompt(qt_bf, k_bf.reshape(bsz, seq, d), vt_bf,
                            lam_p, subln_g[l], lam_init, TQ).reshape(tp, d)
        x1, h2, comb = _merge(u, gates, attn, xp,
                              [pmod(2, TM_MERGE), pmod(4, TM_MERGE), pmod(3, TM_MERGE)], [mod_p] * 3,
                              w_pool_bf, pool_scale[l], w_out_bf, ln1_g[l], ln1_b[l], w_r, b_r, alpha, TM_MERGE,
                              seq_len=seq)
        xp = _moe(h2, comb, wg_bf, wu_bf, wd_bf, x1, pmod(5, TM_MOE, True), mod_p,
                  ln2_g[l], ln2_b[l], alpha, TM_MOE, E_BLK)
        outs["kp"].append(jnp.transpose(kt_f.reshape(bsz, N_HEADS, 2, HEAD_DIM, seq), (0, 4, 1, 2, 3)))
        outs["vp"].append(v_f.reshape(bsz, seq, N_HEADS, V_DIM))
        outs["pp"].append(u.reshape(bsz, seq, -1)[:, seq - POOL_BUF:])

        def smod(k, two_d=False):
            if two_d:
                return pl.BlockSpec((ts, d), lambda i, j: (0, k))
            return pl.BlockSpec((ts, d), lambda i: (0, k))

        tab_t_spec_s = pl.BlockSpec((HEAD_DIM // 2, ts), lambda i: (0, 0))
        qt_s, kt_f, s_new, v_f, u, gates = _inproj(
            xs, smod(1), smod(0), mod_s, mod_s, w_in[l],
            tabs_s, tab_t_spec_s, ts, 1, False)
        attn = _attn_sample(qt_s[0], s_new, v_f, cache_k[l], cache_v[l], page_table, lam_p, subln_g[l],
                            lam_init)
        pooled, new_pool = _pool_sample(state_pool[l], u, past_len)
        x1, h2, comb = _merge(pooled, gates, attn, xs, [smod(2), smod(4), smod(3)], [mod_s] * 3,
                              w_pool[l], pool_scale[l], w_out[l], ln1_g[l], ln1_b[l], w_r, b_r, alpha, ts)
        xs = _moe(h2, comb, wg_bf, wu_bf, wd_bf, x1, smod(5, True), mod_s,
                  ln2_g[l], ln2_b[l], alpha, ts, E_BLK)
        outs["ks"].append(jnp.transpose(kt_f.reshape(N_HEADS, 2, HEAD_DIM, db, dseq), (3, 4, 0, 1, 2)))
        outs["vs"].append(v_f.reshape(db, dseq, N_HEADS, V_DIM))
        outs["ps"].append(new_pool)

    return (xp.reshape(bsz, seq, d), xs.reshape(db, dseq, d),
            jnp.stack(outs["kp"]), jnp.stack(outs["vp"]), jnp.stack(outs["pp"]),
            jnp.stack(outs["ks"]), jnp.stack(outs["vs"]), jnp.stack(outs["ps"]))
```

```python
import functools
import math

import jax
import jax.numpy as jnp
from jax import lax
from jax.experimental import pallas as pl
from jax.experimental.pallas import tpu as pltpu
from jax.experimental.pallas import tpu_sc as plsc

F32 = jnp.float32
BF16 = jnp.bfloat16

LANES = 128
VMEM_LIMIT = 56 << 20

N_MOD = 6
N_HEADS = 8
HEAD_DIM = 64
V_DIM = 2 * HEAD_DIM
ROPE_THETA = 10000.0
POOL_WINDOWS = (2, 4, 8, 16)
POOL_GROUP_DIM = 128
POOL_BUF = max(POOL_WINDOWS) - 1
N_GROUPS = 4
EXPERTS_PER_GROUP = 4
N_EXPERTS = N_GROUPS * EXPERTS_PER_GROUP
LN_EPS = 1e-5
SUBLN_EPS = 1e-5
ROUTER_ROWS = 32
GATHER_WINDOW = 128

NT_DIMS = (((1,), (1,)), ((), ()))


def _params(*sem):
    return pltpu.CompilerParams(dimension_semantics=sem, vmem_limit_bytes=VMEM_LIMIT)


def _sigmoid(x):
    return 1.0 / (1.0 + jnp.exp(-x))


def _layer_norm(x, g, b):
    mu = jnp.mean(x, axis=-1, keepdims=True)
    xc = x - mu
    var = jnp.mean(xc * xc, axis=-1, keepdims=True)
    return xc * lax.rsqrt(var + LN_EPS) * g + b


def _lambda(lam_ref, lam_init):
    lp = lam_ref[...]
    a = jnp.sum(lp[0:1] * lp[1:2], axis=1, keepdims=True)
    b = jnp.sum(lp[2:3] * lp[3:4], axis=1, keepdims=True)
    return jnp.exp(a) - jnp.exp(b) + lam_init


def _ada_kernel(c_ref, w_ref, b_ref, o_ref):
    c = c_ref[...]
    o_ref[...] = jnp.dot(c * _sigmoid(c), w_ref[...], preferred_element_type=F32,
                         precision=lax.Precision.HIGHEST) + b_ref[...]


def _ada(c_all, w_ada, b_ada):
    n, d = c_all.shape
    cols = w_ada.shape[1]
    return pl.pallas_call(
        _ada_kernel,
        grid=(cols // d,),
        in_specs=[pl.BlockSpec((n, d), lambda j: (0, 0)),
                  pl.BlockSpec((d, d), lambda j: (0, j)),
                  pl.BlockSpec((1, d), lambda j: (0, j))],
        out_specs=pl.BlockSpec((n, d), lambda j: (0, j)),
        out_shape=jax.ShapeDtypeStruct((n, cols), F32),
        compiler_params=_params("arbitrary"),
        name="ada_mod",
    )(c_all, w_ada, b_ada.reshape(1, cols))


def _inproj_kernel(x_ref, sc_ref, sh_ref, w_ref, cost_ref, sint_ref, *out_refs, d, pw, prompt):
    mm_dtype = w_ref.dtype
    prec = lax.Precision.HIGHEST if mm_dtype == F32 else None
    dot = functools.partial(jnp.dot, preferred_element_type=F32, precision=prec)
    h =(x_ref[...] * (1.0 + sc_ref[...]) + sh_ref[...]).astype(mm_dtype)
    half = HEAD_DIM // 2
    scale = HEAD_DIM ** -0.5 * (math.log2(math.e) if prompt else 1.0)
    cost, sint = cost_ref[...], sint_ref[...]

    def rope_rows(xt):
        out = []
        for g in range(d // HEAD_DIM):
            lo = slice(g * HEAD_DIM, g * HEAD_DIM + half)
            hi = slice(g * HEAD_DIM + half, (g + 1) * HEAD_DIM)
            x1, x2 = xt[lo], xt[hi]
            out.append((lo, x1 * cost - x2 * sint))
            out.append((hi, x2 * cost + x1 * sint))
        return out

    qt = rope_rows(dot(h, w_ref[:, 0:d]).T)
    kt = rope_rows(dot(h, w_ref[:, d:2 * d]).T)
    v = dot(h, w_ref[:, 2 * d:3 * d])
    if prompt:
        qt_ref, ktf_ref, kb_ref, vf_ref, vtb_ref, u_ref, g_ref = out_refs
        for sl, r in qt:
            qt_ref[sl, :] = (r * scale).astype(BF16)
        for sl, r in kt:
            ktf_ref[sl, :] = r
        kb_ref[...] = ktf_ref[...].T.astype(BF16)
        vtb_ref[...] = v.T.astype(BF16)
        g_dtype = BF16
    else:
        g_dtype = F32
        qt_ref, ktf_ref, sn_ref, vf_ref, u_ref, g_ref = out_refs
        for idx, ((sl, rq), (_, rk)) in enumerate(zip(qt, kt)):
            qs = rq * scale
            qt_ref[sl, :] = qs
            ktf_ref[sl, :] = rk
            part = jnp.sum(qs * rk, axis=0, keepdims=True)
            g, is_hi = idx // 2, idx % 2
            head, which = g // 2, g % 2
            if is_hi:
                sn_ref[which, head:head + 1, :] += part
            else:
                sn_ref[which, head:head + 1, :] = part
    vf_ref[...] = v
    u_ref[...] = dot(h, w_ref[:, 3 * d:3 * d + pw])
    g_ref[...] = _sigmoid(dot(h, w_ref[:, 3 * d + pw:])).astype(g_dtype)


def _inproj(x, sc_spec, sh_spec, sc_arr, sh_arr, w_in, tabs_t, tab_t_spec, tm, n_seq, prompt):
    t, d = x.shape
    cols = w_in.shape[1]
    pw = cols - 5 * d
    per_seq = t // n_seq
    nblk = per_seq // tm
    row = lambda width: pl.BlockSpec((tm, width), lambda i: (i, 0))
    ft_spec = pl.BlockSpec((None, d, tm), lambda i: (i // nblk, 0, i % nblk))
    ft_shape = lambda dt: jax.ShapeDtypeStruct((n_seq, d, per_seq), dt)
    tok_shape = lambda width, dt: jax.ShapeDtypeStruct((t, width), dt)
    if prompt:
        out_specs = [ft_spec, ft_spec, row(d), row(d), ft_spec, row(pw), row(2 * d)]
        out_shape = [ft_shape(BF16), ft_shape(F32), tok_shape(d, BF16), tok_shape(d, F32), ft_shape(BF16),
                     tok_shape(pw, F32), tok_shape(2 * d, BF16)]
    else:
        assert n_seq == 1 and nblk == 1
        out_specs = [ft_spec, ft_spec, pl.BlockSpec((2, N_HEADS, tm), lambda i: (0, 0, 0)), row(d),
                     row(pw), row(2 * d)]
        out_shape = [ft_shape(F32), ft_shape(F32), jax.ShapeDtypeStruct((2, N_HEADS, t), F32),
                     tok_shape(d, F32), tok_shape(pw, F32), tok_shape(2 * d, F32)]
    return pl.pallas_call(
        functools.partial(_inproj_kernel, d=d, pw=pw, prompt=prompt),
        grid=(t // tm,),
        in_specs=[row(d), sc_spec, sh_spec,
                  pl.BlockSpec((d, cols), lambda i: (0, 0), pipeline_mode=pl.Buffered(1)),
                  tab_t_spec, tab_t_spec],
        out_specs=out_specs,
        out_shape=out_shape,
        compiler_params=_params("arbitrary"),
        name="in_proj",
    )(x, sc_arr, sh_arr, w_in, *tabs_t)


def _rope_tables(pos):
    half = HEAD_DIM // 2
    inv = ROPE_THETA ** (-jnp.arange(half, dtype=F32) / half)
    ang = inv[:, None] * pos.astype(F32)[None, :]
    return jnp.cos(ang), jnp.sin(ang)


def _attn_kernel(lam_ref, qt_ref, k_ref, vt_ref, g_ref, o_ref, q2_sc, sta_sc, stb_sc, m_sc, l_sc, acc_sc,
                 *, tq, lam_init):
    qi = pl.program_id(2)
    zero = jnp.zeros((HEAD_DIM, tq), BF16)
    q2_sc[0:HEAD_DIM, 0:tq] = qt_ref[0:HEAD_DIM, :]
    q2_sc[HEAD_DIM:V_DIM, 0:tq] = zero
    q2_sc[0:HEAD_DIM, tq:2 * tq] = zero
    q2_sc[HEAD_DIM:V_DIM, tq:2 * tq] = qt_ref[HEAD_DIM:V_DIM, :]
    m_sc[...] = jnp.full(m_sc.shape, -jnp.inf, F32)
    l_sc[...] = jnp.zeros(l_sc.shape, F32)
    acc_sc[...] = jnp.zeros(acc_sc.shape, F32)

    def scores(ki, st_ref):
        start = pl.multiple_of(ki * tq, tq)
        st_ref[...] = jnp.dot(k_ref[pl.ds(start, tq), :], q2_sc[...], preferred_element_type=F32)

    def consume(ki, st_ref, masked):
        st = st_ref[...]
        if masked:
            key = lax.broadcasted_iota(jnp.int32, st.shape, 0)
            qry = lax.broadcasted_iota(jnp.int32, st.shape, 1)
            qry = jnp.where(qry >= tq, qry - tq, qry)
            st = jnp.where(key <= qry, st, -jnp.inf)
        m_old = m_sc[...]
        m_new = jnp.maximum(m_old, jnp.max(st, axis=0, keepdims=True))
        alpha = jnp.exp2(m_old - m_new)
        p = jnp.exp2(st - m_new).astype(BF16)
        start = pl.multiple_of(ki * tq, tq)
        vt = vt_ref[:, pl.ds(start, tq)]
        vt1 = jnp.concatenate([vt, jnp.ones((16, tq), BF16)], axis=0)
        r = jnp.dot(vt1, p, preferred_element_type=F32)
        l_sc[...] = alpha * l_sc[...] + r[V_DIM:V_DIM + 1]
        acc_sc[...] = alpha * acc_sc[...] + r[0:V_DIM]
        m_sc[...] = m_new

    scores(0, sta_sc)

    def pair(pi, carry):
        k0 = 2 * pi
        scores(k0 + 1, stb_sc)
        consume(k0, sta_sc, False)
        scores(k0 + 2, sta_sc)
        consume(k0 + 1, stb_sc, False)
        return carry

    lax.fori_loop(0, qi // 2, pair, 0)

    @pl.when(qi % 2 == 0)
    def _():
        consume(qi, sta_sc, True)

    @pl.when(qi % 2 == 1)
    def _():
        scores(qi, stb_sc)
        consume(qi - 1, sta_sc, False)
        consume(qi, stb_sc, True)

    lam = _lambda(lam_ref, lam_init)
    inv_l = 1.0 / l_sc[...]
    ot = acc_sc[:, 0:tq] * inv_l[:, 0:tq] - lam * (acc_sc[:, tq:2 * tq] * inv_l[:, tq:2 * tq])
    ms = jnp.mean(ot * ot, axis=0, keepdims=True)
    ot = ot * lax.rsqrt(ms + SUBLN_EPS) * g_ref[...] * (1.0 - lam_init)
    o_ref[...] = ot.T.astype(o_ref.dtype)


def _attn_prompt(qt_bf, k_bf, vt_bf, lam_p, subln_g, lam_init, tq):
    b, d, s = qt_bf.shape
    nh = d // V_DIM
    return pl.pallas_call(
        functools.partial(_attn_kernel, tq=tq, lam_init=lam_init),
        grid=(b, nh, s // tq),
        in_specs=[pl.BlockSpec(lam_p.shape, lambda bi, h, qi: (0, 0)),
                  pl.BlockSpec((None, V_DIM, tq), lambda bi, h, qi: (bi, h, qi)),
                  pl.BlockSpec((None, s, V_DIM), lambda bi, h, qi: (bi, 0, h)),
                  pl.BlockSpec((None, V_DIM, s), lambda bi, h, qi: (bi, h, 0)),
                  pl.BlockSpec((V_DIM, 1), lambda bi, h, qi: (0, 0))],
        out_specs=pl.BlockSpec((None, tq, V_DIM), lambda bi, h, qi: (bi, qi, h)),
        out_shape=jax.ShapeDtypeStruct((b, s, d), BF16),
        scratch_shapes=[pltpu.VMEM((V_DIM, 2 * tq), BF16),
                        pltpu.VMEM((tq, 2 * tq), F32),
                        pltpu.VMEM((tq, 2 * tq), F32),
                        pltpu.VMEM((1, 2 * tq), F32),
                        pltpu.VMEM((1, 2 * tq), F32),
                        pltpu.VMEM((V_DIM, 2 * tq), F32)],
        compiler_params=_params("arbitrary", "arbitrary", "arbitrary"),
        name="attn_prompt",
    )(lam_p, qt_bf, k_bf, vt_bf, subln_g.reshape(V_DIM, 1))


def _attn_sample_kernel(pt_ref, lam_ref, qt_ref, sn_ref, vn_ref, g_ref, *refs, n_pages, lam_init):
    del pt_ref
    kt_refs, v_refs = refs[:n_pages], refs[n_pages:2 * n_pages]
    o_ref, s_sc = refs[2 * n_pages:]
    b = pl.program_id(0)
    d, n_tok = qt_ref.shape
    page = kt_refs[0].shape[1]

    onehot = jnp.where(lax.broadcasted_iota(jnp.int32, (n_tok, page), 0) == b, 1.0, 0.0)
    qb = jnp.dot(qt_ref[...], onehot, preferred_element_type=F32,
                 precision=lax.Precision.HIGHEST)
    for p in range(n_pages):
        prod = (kt_refs[p][...] * qb).reshape(N_HEADS, V_DIM, page)
        s_sc[0, :, p * page:(p + 1) * page] = jnp.sum(prod[:, 0:HEAD_DIM, :], axis=1)
        s_sc[1, :, p * page:(p + 1) * page] = jnp.sum(prod[:, HEAD_DIM:V_DIM, :], axis=1)

    own = lax.broadcasted_iota(jnp.int32, (N_HEADS, n_tok), 1) == b

    def softmax_parts(which):
        s_new = jnp.max(jnp.where(own, sn_ref[which], -jnp.inf), axis=1, keepdims=True)
        s = s_sc[which]
        m = jnp.maximum(jnp.max(s, axis=1, keepdims=True), s_new)
        p = jnp.exp(s - m)
        p_new = jnp.exp(s_new - m)
        inv_l = 1.0 / (jnp.sum(p, axis=1, keepdims=True) + p_new)
        return p * inv_l, p_new * inv_l

    lam = _lambda(lam_ref, lam_init)
    w1, wn1 = softmax_parts(0)
    w2, wn2 = softmax_parts(1)
    w = w1 - lam * w2
    acc = (wn1 - lam * wn2) * vn_ref[...]
    for p in range(n_pages):
        for t in range(page):
            c = p * page + t
            acc = acc + w[:, c:c + 1] * v_refs[p][t]
    ms = jnp.mean(acc * acc, axis=1, keepdims=True)
    o_ref[...] = acc * lax.rsqrt(ms + SUBLN_EPS) * g_ref[...] * (1.0 - lam_init)


def _attn_sample(qt, s_new, v_new, cache_k, cache_v, page_table, lam_p, subln_g, lam_init):
    d, db = qt.shape
    n_pages = page_table.shape[1]
    n_phys, page = cache_k.shape[0], cache_k.shape[1]
    ck = jnp.transpose(cache_k, (0, 2, 3, 4, 1)).reshape(n_phys, d, page)
    pt = page_table.reshape(-1)

    def page_spec(p, shape):
        zeros = (0,) * len(shape)
        return pl.BlockSpec((None,) + shape, lambda b, pt_ref: (pt_ref[b * n_pages + p],) + zeros)

    const = lambda shape: pl.BlockSpec(shape, lambda b, pt_ref: (0,) * len(shape))
    head_spec = pl.BlockSpec((None, N_HEADS, V_DIM), lambda b, pt_ref: (b, 0, 0))
    grid_spec = pltpu.PrefetchScalarGridSpec(
        num_scalar_prefetch=1,
        grid=(db,),
        in_specs=[const(lam_p.shape), const((d, db)), const((2, N_HEADS, db)), head_spec, const((1, V_DIM))]
                 + [page_spec(p, (d, page)) for p in range(n_pages)]
                 + [page_spec(p, (page, N_HEADS, V_DIM)) for p in range(n_pages)],
        out_specs=head_spec,
        scratch_shapes=[pltpu.VMEM((2, N_HEADS, n_pages * page), F32)],
    )
    out = pl.pallas_call(
        functools.partial(_attn_sample_kernel, n_pages=n_pages, lam_init=lam_init),
        grid_spec=grid_spec,
        out_shape=jax.ShapeDtypeStruct((db, N_HEADS, V_DIM), F32),
        compiler_params=_params("arbitrary"),
        name="attn_sample",
    )(pt, lam_p, qt, s_new, v_new.reshape(db, N_HEADS, V_DIM), subln_g.reshape(1, V_DIM),
      *([ck] * n_pages), *([cache_v] * n_pages))
    return out.reshape(db, d)


POOL_HALO = 16


def _pool_tile(u_ref, halo_ref, ext_sc, seq_tile):
    tm = u_ref.shape[0]
    hb = halo_ref.shape[0]

    @pl.when(seq_tile == 0)
    def _():
        ext_sc[0:hb, :] = jnp.zeros((hb, ext_sc.shape[1]), F32)

    @pl.when(seq_tile > 0)
    def _():
        ext_sc[0:hb, :] = halo_ref[...]

    ext_sc[hb:hb + tm, :] = u_ref[...]
    pos = seq_tile * tm + lax.broadcasted_iota(jnp.int32, (tm, POOL_GROUP_DIM), 0)
    parts = []
    for g, win in enumerate(POOL_WINDOWS):
        sl = slice(g * POOL_GROUP_DIM, (g + 1) * POOL_GROUP_DIM)
        cur = ext_sc[hb:hb + tm, sl]
        tot = cur
        for j in range(1, win):
            tot = tot + ext_sc[hb - j:hb - j + tm, sl]
        cnt = jnp.minimum(win, pos + 1).astype(F32)
        parts.append(tot / cnt - cur)
    return jnp.concatenate(parts, axis=1)


def _pool_sample_kernel(st_ref, u_ref, o_ref, np_ref, *, start_pos):
    nb = st_ref.shape[0]
    u = u_ref[...]
    for g, win in enumerate(POOL_WINDOWS):
        sl = slice(g * POOL_GROUP_DIM, (g + 1) * POOL_GROUP_DIM)
        cur = u[:, sl]
        tot = cur
        for j in range(1, win):
            tot = tot + st_ref[nb - j, :, sl]
        o_ref[:, sl] = tot / float(min(win, start_pos + 1)) - cur
    for j in range(nb - 1):
        np_ref[j] = st_ref[j + 1]
    np_ref[nb - 1] = u


def _pool_sample(state, u, start_pos):
    db, nb, pw = state.shape
    full = lambda shape: pl.BlockSpec(shape, lambda i: (0,) * len(shape))
    pooled, new_pool = pl.pallas_call(
        functools.partial(_pool_sample_kernel, start_pos=start_pos),
        grid=(1,),
        in_specs=[full((nb, db, pw)), full((db, pw))],
        out_specs=[full((db, pw)), full((nb, db, pw))],
        out_shape=[jax.ShapeDtypeStruct((db, pw), F32), jax.ShapeDtypeStruct((nb, db, pw), F32)],
        compiler_params=_params("arbitrary"),
        name="pool_sample",
    )(jnp.transpose(state, (1, 0, 2)), u)
    return pooled, jnp.transpose(new_pool, (1, 0, 2))


def _route(lt):
    g = [lt[j:j + 1] for j in range(N_GROUPS)]
    gmax = functools.reduce(jnp.maximum, g)
    gidx = jnp.full(gmax.shape, N_GROUPS - 1, jnp.int32)
    for j in range(N_GROUPS - 2, -1, -1):
        gidx = jnp.where(g[j] == gmax, j, gidx)
    p_g = 1.0 / functools.reduce(jnp.add, [jnp.exp(gj - gmax) for gj in g])
    e = []
    for k in range(EXPERTS_PER_GROUP):
        sel = lt[N_GROUPS + k:N_GROUPS + k + 1]
        for j in range(1, N_GROUPS):
            r = N_GROUPS + j * EXPERTS_PER_GROUP + k
            sel = jnp.where(gidx == j, lt[r:r + 1], sel)
        e.append(sel)

    def first_argmax(vals):
        vmax = functools.reduce(jnp.maximum, vals)
        idx = jnp.full(vmax.shape, len(vals) - 1, jnp.int32)
        for k in range(len(vals) - 2, -1, -1):
            idx = jnp.where(vals[k] == vmax, k, idx)
        return vmax, idx

    v1, i1 = first_argmax(e)
    rest = [jnp.where(i1 == k, -jnp.inf, e[k]) for k in range(EXPERTS_PER_GROUP)]
    v2, i2 = first_argmax(rest)
    t = jnp.exp(v2 - v1)
    w1 = p_g / (1.0 + t)
    w2 = p_g * t / (1.0 + t)
    within = [jnp.where(i1 == k, w1, 0.0) + jnp.where(i2 == k, w2, 0.0) for k in range(EXPERTS_PER_GROUP)]
    rows = [jnp.where(gidx == j, within[k], 0.0) for j in range(N_GROUPS) for k in range(EXPERTS_PER_GROUP)]
    return jnp.concatenate(rows, axis=0), gidx


def _pack_bf16_pair(a, b):
    a_bits = lax.bitcast_convert_type(a.astype(BF16).astype(F32), jnp.uint32)
    b_bits = lax.bitcast_convert_type(b.astype(BF16).astype(F32), jnp.uint32)
    return (a_bits >> 16) | (b_bits & jnp.uint32(0xFFFF0000))


def _unpack_bf16_pair(w):
    lo = lax.bitcast_convert_type(w << 16, F32)
    hi = lax.bitcast_convert_type(w & jnp.uint32(0xFFFF0000), F32)
    return lo, hi


def _merge_kernel(*refs, alpha, tiles_per_seq):
    if tiles_per_seq:
        u_ref, halo_ref, *refs, ext_sc = refs
        pooled = _pool_tile(u_ref, halo_ref, ext_sc, pl.program_id(0) % tiles_per_seq)
    else:
        pooled_ref, *refs = refs
        pooled = pooled_ref[...]
    (gates_ref, attn_ref, x_ref, g1_ref, sc2_ref, sh2_ref, wpool_ref, pscale_ref, wout_ref, lng_ref, lnb_ref,
     wr_ref, br_ref, x1_ref, *route_refs) = refs
    d = x_ref.shape[1]
    tm = x_ref.shape[0]
    mm_dtype = wout_ref.dtype
    dot = functools.partial(jnp.dot, preferred_element_type=F32,
                            precision=lax.Precision.HIGHEST if mm_dtype == F32 else None)
    pooled = pooled.astype(mm_dtype)
    parts = [dot(pooled[:, g * POOL_GROUP_DIM:(g + 1) * POOL_GROUP_DIM], wpool_ref[g])
             for g in range(len(POOL_WINDOWS))]
    pool_out = jnp.concatenate(parts, axis=1) * pscale_ref[...]
    merged = (gates_ref[:, 0:d].astype(F32) * pool_out
              + gates_ref[:, d:2 * d].astype(F32) * attn_ref[...].astype(F32))
    y = dot(merged.astype(mm_dtype), wout_ref[...])
    x1 = _layer_norm(alpha * x_ref[...] + (1.0 + g1_ref[...]) * y, lng_ref[...], lnb_ref[...])
    x1_ref[...] = x1
    h2 = x1 * (1.0 + sc2_ref[...]) + sh2_ref[...]
    lt = lax.dot_general(wr_ref[...], h2, NT_DIMS, preferred_element_type=F32,
                         precision=lax.Precision.HIGHEST) + br_ref[...]
    comb_t, gidx = _route(lt)
    comb = jnp.concatenate([comb_t, jnp.zeros((LANES - N_EXPERTS, tm), F32)], axis=0).T
    if tiles_per_seq:
        row_ref, gid_ref = route_refs
        row_ref[:, 0:d // 2] = _pack_bf16_pair(h2[:, 0:d // 2], h2[:, d // 2:d])
        row_ref[:, d // 2:d // 2 + LANES] = lax.bitcast_convert_type(comb, jnp.uint32)
        gid_ref[...] = gidx
    else:
        h2_ref, comb_ref = route_refs
        h2_ref[...] = h2.astype(BF16)
        comb_ref[...] = comb


def _merge(pool_in, gates, attn, x, mod_specs, mod_arrs, w_pool, pool_scale, w_out,
           ln_g, ln_b, w_r, b_r, alpha, tm, seq_len=None):
    t, d = x.shape
    pw = pool_in.shape[1]
    row = lambda width: pl.BlockSpec((tm, width), lambda i: (i, 0))
    const = lambda shape: pl.BlockSpec(shape, lambda i: (0,) * len(shape))
    if seq_len:
        halo_spec = pl.BlockSpec((POOL_HALO, pw), lambda i: (jnp.maximum(i * (tm // POOL_HALO) - 1, 0), 0))
        pool_specs, pool_args = [row(pw), halo_spec], [pool_in, pool_in]
        scratch = [pltpu.VMEM((POOL_HALO + tm, pw), F32)]
    else:
        pool_specs, pool_args, scratch = [row(pw)], [pool_in], []
    if seq_len:
        route_specs = [row(d // 2 + LANES), pl.BlockSpec((None, 1, tm), lambda i: (i, 0, 0))]
        route_shapes = [jax.ShapeDtypeStruct((t, d // 2 + LANES), jnp.uint32),
                        jax.ShapeDtypeStruct((t // tm, 1, tm), jnp.int32)]
    else:
        route_specs = [row(d), row(LANES)]
        route_shapes = [jax.ShapeDtypeStruct((t, d), BF16), jax.ShapeDtypeStruct((t, LANES), F32)]
    return pl.pallas_call(
        functools.partial(_merge_kernel, alpha=alpha, tiles_per_seq=seq_len // tm if seq_len else None),
        grid=(t // tm,),
        in_specs=[*pool_specs, row(2 * d), row(d), row(d), *mod_specs,
                  const(w_pool.shape), const((1, d)), const((d, d)), const((1, d)), const((1, d)),
                  const(w_r.shape), const(b_r.shape)],
        out_specs=[row(d), *route_specs],
        out_shape=[jax.ShapeDtypeStruct((t, d), F32), *route_shapes],
        scratch_shapes=scratch,
        compiler_params=_params("arbitrary"),
        name="merge",
    )(*pool_args, gates, attn, x, *mod_arrs, w_pool, pool_scale.reshape(1, d), w_out,
      ln_g.reshape(1, d), ln_b.reshape(1, d), w_r, b_r)


def _moe_kernel(h2_ref, comb_ref, wg_ref, wu_ref, wd_ref, x1_ref, g2_ref, lng_ref, lnb_ref,
                o_ref, acc_sc, *, alpha):
    j = pl.program_id(1)
    eb = wg_ref.shape[0]

    @pl.when(j == 0)
    def _():
        acc_sc[...] = jnp.zeros(acc_sc.shape, F32)

    h2 = h2_ref[...]
    comb = comb_ref[...]
    lane = lax.broadcasted_iota(jnp.int32, comb.shape, 1)
    for e in range(eb):
        hg = jnp.dot(h2, wg_ref[e], preferred_element_type=F32)
        hu = jnp.dot(h2, wu_ref[e], preferred_element_type=F32)
        c = jnp.sum(jnp.where(lane == j * eb + e, comb, 0.0), axis=1, keepdims=True)
        act = (hg * _sigmoid(hg)) * hu * c
        acc_sc[...] += jnp.dot(act.astype(BF16), wd_ref[e], preferred_element_type=F32)

    @pl.when(j == pl.num_programs(1) - 1)
    def _():
        y = alpha * x1_ref[...] + (1.0 + g2_ref[...]) * acc_sc[...]
        o_ref[...] = _layer_norm(y, lng_ref[...], lnb_ref[...])


def _moe(h2, comb, wg_bf, wu_bf, wd_bf, x1, g2_spec, g2_arr, ln_g, ln_b, alpha, tm, eb):
    t, d = x1.shape
    ne, _, ff = wg_bf.shape
    row = lambda width: pl.BlockSpec((tm, width), lambda i, j: (i, 0))
    const = lambda shape: pl.BlockSpec(shape, lambda i, j: (0,) * len(shape))
    return pl.pallas_call(
        functools.partial(_moe_kernel, alpha=alpha),
        grid=(t // tm, ne // eb),
        in_specs=[row(d), row(LANES),
                  pl.BlockSpec((eb, d, ff), lambda i, j: (j, 0, 0)),
                  pl.BlockSpec((eb, d, ff), lambda i, j: (j, 0, 0)),
                  pl.BlockSpec((eb, ff, d), lambda i, j: (j, 0, 0)),
                  row(d), g2_spec, const((1, d)), const((1, d))],
        out_specs=row(d),
        out_shape=jax.ShapeDtypeStruct((t, d), F32),
        scratch_shapes=[pltpu.VMEM((tm, d), F32)],
        compiler_params=_params("arbitrary", "arbitrary"),
        name="moe_ffn",
    )(h2, comb, wg_bf, wu_bf, wd_bf, x1, g2_arr, ln_g.reshape(1, d), ln_b.reshape(1, d))


def _gather_rows(x, idx):
    n, width = idx.shape[0], x.shape[1]
    info = pltpu.get_tpu_info().sparse_core
    units = info.num_cores * info.num_subcores
    per_unit = n // (units * GATHER_WINDOW)
    assert per_unit * units * GATHER_WINDOW == n

    @pl.kernel(out_type=jax.ShapeDtypeStruct((n, width), x.dtype),
               mesh=plsc.VectorSubcoreMesh(core_axis_name="core", subcore_axis_name="subcore"),
               scratch_types=[pltpu.VMEM((1, GATHER_WINDOW), jnp.int32), pltpu.VMEM((GATHER_WINDOW, width), x.dtype)])
    def gather_kernel(x_hbm, i_hbm, o_hbm, i_vmem, buf):
        unit = lax.axis_index("core") * info.num_subcores + lax.axis_index("subcore")

        @pl.loop(0, per_unit)
        def _(j):
            base = (unit * per_unit + j) * GATHER_WINDOW
            pltpu.sync_copy(i_hbm.at[:, pl.ds(base, GATHER_WINDOW)], i_vmem)
            pltpu.sync_copy(x_hbm.at[i_vmem.at[0]], buf)
            pltpu.sync_copy(buf, o_hbm.at[pl.ds(base, GATHER_WINDOW), :])

    return gather_kernel(x, idx.reshape(1, n))


def _dispatch_plan(gid, tile, n_slots):
    t = gid.shape[0]
    onehot = (gid[:, None] == jnp.arange(N_GROUPS, dtype=jnp.int32)[None, :]).astype(jnp.int32)
    rank = jnp.cumsum(onehot, axis=0) - onehot
    counts = jnp.sum(onehot, axis=0)
    padded = -(-counts // tile) * tile
    ends = jnp.cumsum(padded)
    starts = ends - padded
    pos = jnp.sum(onehot * (starts[None, :] + rank), axis=1)
    src = jnp.zeros((n_slots,), jnp.int32).at[pos].set(jnp.arange(t, dtype=jnp.int32))
    tile_start = jnp.arange(n_slots // tile, dtype=jnp.int32) * tile
    tile_group = jnp.sum((tile_start[:, None] >= ends[None, :]).astype(jnp.int32), axis=1)
    tile_group = jnp.where(tile_start < ends[-1], tile_group, -1)
    return pos, src, tile_group


def _moe_grouped_kernel(tg_ref, row_ref, wg_ref, wu_ref, wd_ref, y_ref):
    g = tg_ref[pl.program_id(0)]
    half = y_ref.shape[1]

    @pl.when(g < 0)
    def _():
        y_ref[...] = jnp.zeros(y_ref.shape, y_ref.dtype)

    @pl.when(g >= 0)
    def _():
        lo, hi = _unpack_bf16_pair(row_ref[:, 0:half])
        h2 = jnp.concatenate([lo, hi], axis=1).astype(BF16)
        comb = lax.bitcast_convert_type(row_ref[:, half:half + LANES], F32)
        lane = lax.broadcasted_iota(jnp.int32, comb.shape, 1)
        acc = jnp.zeros((h2.shape[0], 2 * half), F32)
        for e in range(EXPERTS_PER_GROUP):
            hg = jnp.dot(h2, wg_ref[e], preferred_element_type=F32)
            hu = jnp.dot(h2, wu_ref[e], preferred_element_type=F32)
            c = jnp.sum(jnp.where(lane == g * EXPERTS_PER_GROUP + e, comb, 0.0), axis=1, keepdims=True)
            act = (hg * _sigmoid(hg)) * hu * c
            acc = acc + jnp.dot(act.astype(BF16), wd_ref[e], preferred_element_type=F32)
        y_ref[...] = _pack_bf16_pair(acc[:, 0:half], acc[:, half:2 * half])


def _moe_grouped(rows_sorted, tile_group, wg_bf, wu_bf, wd_bf, tile):
    n_slots, width = rows_sorted.shape
    ne, d, ff = wg_bf.shape
    group_w = lambda shape: pl.BlockSpec((EXPERTS_PER_GROUP,) + shape,
                                         lambda i, tg: (jnp.maximum(tg[i], 0), 0, 0))
    grid_spec = pltpu.PrefetchScalarGridSpec(
        num_scalar_prefetch=1,
        grid=(n_slots // tile,),
        in_specs=[pl.BlockSpec((tile, width), lambda i, tg: (i, 0)),
                  group_w((d, ff)), group_w((d, ff)), group_w((ff, d))],
        out_specs=pl.BlockSpec((tile, d // 2), lambda i, tg: (i, 0)),
    )
    return pl.pallas_call(
        _moe_grouped_kernel,
        grid_spec=grid_spec,
        out_shape=jax.ShapeDtypeStruct((n_slots, d // 2), jnp.uint32),
        compiler_params=_params("arbitrary"),
        name="moe_grouped",
    )(tile_group, rows_sorted, wg_bf, wu_bf, wd_bf)


def _moe_finish_kernel(x1_ref, y_ref, g2_ref, lng_ref, lnb_ref, o_ref, *, alpha):
    lo, hi = _unpack_bf16_pair(y_ref[...])
    y = jnp.concatenate([lo, hi], axis=1)
    o_ref[...] = _layer_norm(alpha * x1_ref[...] + (1.0 + g2_ref[...]) * y, lng_ref[...], lnb_ref[...])


def _moe_finish(x1, y_packed, g2_spec, g2_arr, ln_g, ln_b, alpha, tm):
    t, d = x1.shape
    row = lambda width: pl.BlockSpec((tm, width), lambda i: (i, 0))
    const = lambda shape: pl.BlockSpec(shape, lambda i: (0,) * len(shape))
    return pl.pallas_call(
        functools.partial(_moe_finish_kernel, alpha=alpha),
        grid=(t // tm,),
        in_specs=[row(d), row(d // 2), g2_spec, const((1, d)), const((1, d))],
        out_specs=row(d),
        out_shape=jax.ShapeDtypeStruct((t, d), F32),
        compiler_params=_params("arbitrary"),
        name="moe_finish",
    )(x1, y_packed, g2_arr, ln_g.reshape(1, d), ln_b.reshape(1, d))


def kernel(x_prompt, x_sample, cache_k, cache_v, state_pool, page_table, c_prompt, c_sample, w_ada, b_ada, w_in, lambda_q1, lambda_k1, lambda_q2, lambda_k2, subln_g, w_pool, pool_scale, w_out, ln1_g, ln1_b, w_group, b_group, w_router, b_router, w_e_gate, w_e_up, w_e_down, ln2_g, ln2_b):
    depth = w_in.shape[0]
    bsz, seq, d = x_prompt.shape
    db, dseq, _ = x_sample.shape
    assert dseq == 1, "sample group decodes one token per sequence"
    n_pages, page = page_table.shape[1], cache_k.shape[2]
    past_len = n_pages * page
    alpha = (2.0 * depth) ** 0.25
    tp, ts = bsz * seq, db * dseq

    TM_PROJ, TQ, TM_MERGE, TM_MOE, TM_GROUP, E_BLK = 256, 512, 512, 1024, 512, 4
    sc_info = pltpu.get_tpu_info().sparse_core
    sc_units = sc_info.num_cores * sc_info.num_subcores
    assert tp % (GATHER_WINDOW * sc_units) == 0

    xp = x_prompt.reshape(tp, d)
    xs = x_sample.reshape(ts, d)
    n_c = bsz + db
    n_c_pad = -(-n_c // 8) * 8
    c_all = jnp.concatenate([c_prompt, c_sample, jnp.zeros((n_c_pad - n_c, d), F32)], axis=0)
    tabs_p = _rope_tables(jnp.arange(seq))
    tabs_s = _rope_tables(jnp.full((ts,), past_len))

    outs = {k: [] for k in ("kp", "vp", "pp", "ks", "vs", "ps")}
    for l in range(depth):
        lam_init = 0.8 - 0.6 * math.exp(-0.3 * l)
        mod = _ada(c_all, w_ada[l], b_ada[l])
        mod_p = mod[:bsz].reshape(bsz * N_MOD, 1, d)
        mod_s = mod[bsz:n_c]
        w_in_bf = w_in[l].astype(BF16)
        w_pool_bf = w_pool[l].astype(BF16)
        w_out_bf = w_out[l].astype(BF16)
        wg_bf, wu_bf, wd_bf = (w[l].astype(BF16) for w in (w_e_gate, w_e_up, w_e_down))
        lam_p = jnp.stack([lambda_q1[l], lambda_k1[l], lambda_q2[l], lambda_k2[l]])
        n_r = N_GROUPS + N_EXPERTS
        w_r = jnp.concatenate([w_group[l], w_router[l]], axis=1).T
        w_r = jnp.concatenate([w_r, jnp.zeros((ROUTER_ROWS - n_r, d), F32)], axis=0)
        b_r = jnp.concatenate([b_group[l], b_router[l], jnp.zeros((ROUTER_ROWS - n_r,), F32)]).reshape(ROUTER_ROWS, 1)

        def pmod(k, tm, two_d=False):
            per_b = seq // tm
            if two_d:
                return pl.BlockSpec((None, 1, d), lambda i, j: ((i // per_b) * N_MOD + k, 0, 0))
            return pl.BlockSpec((None, 1, d), lambda i: ((i // per_b) * N_MOD + k, 0, 0))

        nblk = seq // TM_PROJ
        tab_t_spec = pl.BlockSpec((HEAD_DIM // 2, TM_PROJ), lambda i: (0, i % nblk))
        qt_bf, kt_f, k_bf, v_f, vt_bf, u, gates = _inproj(
            xp, pmod(1, TM_PROJ), pmod(0, TM_PROJ), mod_p, mod_p, w_in_bf,
            tabs_p, tab_t_spec, TM_PROJ, bsz, True)
        attn = _attn_prompt(qt_bf, k_bf.reshape(bsz, seq, d), vt_bf,
                            lam_p, subln_g[l], lam_init, TQ).reshape(tp, d)
        x1, ffn_rows, gid = _merge(u, gates, attn, xp,
                                   [pmod(2, TM_MERGE), pmod(4, TM_MERGE), pmod(3, TM_MERGE)], [mod_p] * 3,
                                   w_pool_bf, pool_scale[l], w_out_bf, ln1_g[l], ln1_b[l], w_r, b_r, alpha,
                                   TM_MERGE, seq_len=seq)
        slot_quantum = GATHER_WINDOW * sc_units
        n_slots = -(-(tp + N_GROUPS * (TM_GROUP - 1)) // slot_quantum) * slot_quantum
        pos, src, tile_group = _dispatch_plan(gid.reshape(tp), TM_GROUP, n_slots)
        y_sorted = _moe_grouped(_gather_rows(ffn_rows, src), tile_group, wg_bf, wu_bf, wd_bf, TM_GROUP)
        xp = _moe_finish(x1, _gather_rows(y_sorted, pos), pmod(5, TM_MOE), mod_p,
                         ln2_g[l], ln2_b[l], alpha, TM_MOE)
        outs["kp"].append(jnp.transpose(kt_f.reshape(bsz, N_HEADS, 2, HEAD_DIM, seq), (0, 4, 1, 2, 3)))
        outs["vp"].append(v_f.reshape(bsz, seq, N_HEADS, V_DIM))
        outs["pp"].append(u.reshape(bsz, seq, -1)[:, seq - POOL_BUF:])

        def smod(k, two_d=False):
            if two_d:
                return pl.BlockSpec((ts, d), lambda i, j: (0, k))
            return pl.BlockSpec((ts, d), lambda i: (0, k))

        tab_t_spec_s = pl.BlockSpec((HEAD_DIM // 2, ts), lambda i: (0, 0))
        qt_s, kt_f, s_new, v_f, u, gates = _inproj(
            xs, smod(1), smod(0), mod_s, mod_s, w_in[l],
            tabs_s, tab_t_spec_s, ts, 1, False)
        attn = _attn_sample(qt_s[0], s_new, v_f, cache_k[l], cache_v[l], page_table, lam_p, subln_g[l],
                            lam_init)
        pooled, new_pool = _pool_sample(state_pool[l], u, past_len)
        x1, h2, comb = _merge(pooled, gates, attn, xs, [smod(2), smod(4), smod(3)], [mod_s] * 3,
                              w_pool[l], pool_scale[l], w_out[l], ln1_g[l], ln1_b[l], w_r, b_r, alpha, ts)
        xs = _moe(h2, comb, wg_bf, wu_bf, wd_bf, x1, smod(5, True), mod_s,
                  ln2_g[l], ln2_b[l], alpha, ts, E_BLK)
        outs["ks"].append(jnp.transpose(kt_f.reshape(N_HEADS, 2, HEAD_DIM, db, dseq), (3, 4, 0, 1, 2)))
        outs["vs"].append(v_f.reshape(db, dseq, N_HEADS, V_DIM))
        outs["ps"].append(new_pool)

    return (xp.reshape(bsz, seq, d), xs.reshape(db, dseq, d),
            jnp.stack(outs["kp"]), jnp.stack(outs["vp"]), jnp.stack(outs["pp"]),
            jnp.stack(outs["ks"]), jnp.stack(outs["vs"]), jnp.stack(outs["ps"]))
```

```python
import functools
import math

import jax
import jax.numpy as jnp
from jax import lax
from jax.experimental import pallas as pl
from jax.experimental.pallas import tpu as pltpu
from jax.experimental.pallas import tpu_sc as plsc

F32 = jnp.float32
BF16 = jnp.bfloat16

LANES = 128
VMEM_LIMIT = 56 << 20

N_MOD = 6
N_HEADS = 8
HEAD_DIM = 64
V_DIM = 2 * HEAD_DIM
ROPE_THETA = 10000.0
POOL_WINDOWS = (2, 4, 8, 16)
POOL_GROUP_DIM = 128
POOL_BUF = max(POOL_WINDOWS) - 1
N_GROUPS = 4
EXPERTS_PER_GROUP = 4
N_EXPERTS = N_GROUPS * EXPERTS_PER_GROUP
LN_EPS = 1e-5
SUBLN_EPS = 1e-5
ROUTER_ROWS = 32
GATHER_WINDOW = 128

NT_DIMS = (((1,), (1,)), ((), ()))


def _params(*sem):
    return pltpu.CompilerParams(dimension_semantics=sem, vmem_limit_bytes=VMEM_LIMIT)


def _sigmoid(x):
    return 1.0 / (1.0 + jnp.exp(-x))


def _layer_norm(x, g, b):
    mu = jnp.mean(x, axis=-1, keepdims=True)
    xc = x - mu
    var = jnp.mean(xc * xc, axis=-1, keepdims=True)
    return xc * lax.rsqrt(var + LN_EPS) * g + b


def _lambda(lam_ref, lam_init):
    lp = lam_ref[...]
    a = jnp.sum(lp[0:1] * lp[1:2], axis=1, keepdims=True)
    b = jnp.sum(lp[2:3] * lp[3:4], axis=1, keepdims=True)
    return jnp.exp(a) - jnp.exp(b) + lam_init


def _ada_kernel(c_ref, w_ref, b_ref, o_ref):
    c = c_ref[...]
    o_ref[...] = jnp.dot(c * _sigmoid(c), w_ref[...], preferred_element_type=F32,
                         precision=lax.Precision.HIGHEST) + b_ref[...]


def _ada(c_all, w_ada, b_ada):
    n, d = c_all.shape
    cols = w_ada.shape[1]
    return pl.pallas_call(
        _ada_kernel,
        grid=(cols // d,),
        in_specs=[pl.BlockSpec((n, d), lambda j: (0, 0)),
                  pl.BlockSpec((d, d), lambda j: (0, j)),
                  pl.BlockSpec((1, d), lambda j: (0, j))],
        out_specs=pl.BlockSpec((n, d), lambda j: (0, j)),
        out_shape=jax.ShapeDtypeStruct((n, cols), F32),
        compiler_params=_params("arbitrary"),
        name="ada_mod",
    )(c_all, w_ada, b_ada.reshape(1, cols))


def _inproj_kernel(x_ref, sc_ref, sh_ref, w_ref, cost_ref, sint_ref, *out_refs, d, pw, prompt):
    mm_dtype = w_ref.dtype
    prec = lax.Precision.HIGHEST if mm_dtype == F32 else None
    dot = functools.partial(jnp.dot, preferred_element_type=F32, precision=prec)
    h =(x_ref[...] * (1.0 + sc_ref[...]) + sh_ref[...]).astype(mm_dtype)
    half = HEAD_DIM // 2
    scale = HEAD_DIM ** -0.5 * (math.log2(math.e) if prompt else 1.0)
    cost, sint = cost_ref[...], sint_ref[...]

    def rope_rows(xt):
        out = []
        for g in range(d // HEAD_DIM):
            lo = slice(g * HEAD_DIM, g * HEAD_DIM + half)
            hi = slice(g * HEAD_DIM + half, (g + 1) * HEAD_DIM)
            x1, x2 = xt[lo], xt[hi]
            out.append((lo, x1 * cost - x2 * sint))
            out.append((hi, x2 * cost + x1 * sint))
        return out

    qt = rope_rows(dot(h, w_ref[:, 0:d]).T)
    kt = rope_rows(dot(h, w_ref[:, d:2 * d]).T)
    v = dot(h, w_ref[:, 2 * d:3 * d])
    if prompt:
        qt_ref, ktf_ref, kb_ref, vf_ref, vtb_ref, u_ref, g_ref = out_refs
        for sl, r in qt:
            qt_ref[sl, :] = (r * scale).astype(BF16)
        for sl, r in kt:
            ktf_ref[sl, :] = r
        kb_ref[...] = ktf_ref[...].T.astype(BF16)
        vtb_ref[...] = v.T.astype(BF16)
        g_dtype = BF16
    else:
        g_dtype = F32
        qt_ref, ktf_ref, sn_ref, vf_ref, u_ref, g_ref = out_refs
        for idx, ((sl, rq), (_, rk)) in enumerate(zip(qt, kt)):
            qs = rq * scale
            qt_ref[sl, :] = qs
            ktf_ref[sl, :] = rk
            part = jnp.sum(qs * rk, axis=0, keepdims=True)
            g, is_hi = idx // 2, idx % 2
            head, which = g // 2, g % 2
            if is_hi:
                sn_ref[which, head:head + 1, :] += part
            else:
                sn_ref[which, head:head + 1, :] = part
    vf_ref[...] = v
    u_ref[...] = dot(h, w_ref[:, 3 * d:3 * d + pw])
    g_ref[...] = _sigmoid(dot(h, w_ref[:, 3 * d + pw:])).astype(g_dtype)


def _inproj(x, sc_spec, sh_spec, sc_arr, sh_arr, w_in, tabs_t, tab_t_spec, tm, n_seq, prompt):
    t, d = x.shape
    cols = w_in.shape[1]
    pw = cols - 5 * d
    per_seq = t // n_seq
    nblk = per_seq // tm
    row = lambda width: pl.BlockSpec((tm, width), lambda i: (i, 0))
    ft_spec = pl.BlockSpec((None, d, tm), lambda i: (i // nblk, 0, i % nblk))
    ft_shape = lambda dt: jax.ShapeDtypeStruct((n_seq, d, per_seq), dt)
    tok_shape = lambda width, dt: jax.ShapeDtypeStruct((t, width), dt)
    if prompt:
        out_specs = [ft_spec, ft_spec, row(d), row(d), ft_spec, row(pw), row(2 * d)]
        out_shape = [ft_shape(BF16), ft_shape(F32), tok_shape(d, BF16), tok_shape(d, F32), ft_shape(BF16),
                     tok_shape(pw, F32), tok_shape(2 * d, BF16)]
    else:
        assert n_seq == 1 and nblk == 1
        out_specs = [ft_spec, ft_spec, pl.BlockSpec((2, N_HEADS, tm), lambda i: (0, 0, 0)), row(d),
                     row(pw), row(2 * d)]
        out_shape = [ft_shape(F32), ft_shape(F32), jax.ShapeDtypeStruct((2, N_HEADS, t), F32),
                     tok_shape(d, F32), tok_shape(pw, F32), tok_shape(2 * d, F32)]
    return pl.pallas_call(
        functools.partial(_inproj_kernel, d=d, pw=pw, prompt=prompt),
        grid=(t // tm,),
        in_specs=[row(d), sc_spec, sh_spec,
                  pl.BlockSpec((d, cols), lambda i: (0, 0), pipeline_mode=pl.Buffered(1)),
                  tab_t_spec, tab_t_spec],
        out_specs=out_specs,
        out_shape=out_shape,
        compiler_params=_params("arbitrary"),
        name="in_proj",
    )(x, sc_arr, sh_arr, w_in, *tabs_t)


def _rope_tables(pos):
    half = HEAD_DIM // 2
    inv = ROPE_THETA ** (-jnp.arange(half, dtype=F32) / half)
    ang = inv[:, None] * pos.astype(F32)[None, :]
    return jnp.cos(ang), jnp.sin(ang)


def _attn_kernel(lam_ref, qt_ref, k_ref, vt_ref, g_ref, o_ref, q2_sc, sta_sc, stb_sc, m_sc, l_sc, acc_sc,
                 *, tq, lam_init):
    qi = pl.program_id(2)
    zero = jnp.zeros((HEAD_DIM, tq), BF16)
    q2_sc[0:HEAD_DIM, 0:tq] = qt_ref[0:HEAD_DIM, :]
    q2_sc[HEAD_DIM:V_DIM, 0:tq] = zero
    q2_sc[0:HEAD_DIM, tq:2 * tq] = zero
    q2_sc[HEAD_DIM:V_DIM, tq:2 * tq] = qt_ref[HEAD_DIM:V_DIM, :]
    m_sc[...] = jnp.full(m_sc.shape, -jnp.inf, F32)
    l_sc[...] = jnp.zeros(l_sc.shape, F32)
    acc_sc[...] = jnp.zeros(acc_sc.shape, F32)

    def scores(ki, st_ref):
        start = pl.multiple_of(ki * tq, tq)
        st_ref[...] = jnp.dot(k_ref[pl.ds(start, tq), :], q2_sc[...], preferred_element_type=F32)

    def consume(ki, st_ref, masked):
        st = st_ref[...]
        if masked:
            key = lax.broadcasted_iota(jnp.int32, st.shape, 0)
            qry = lax.broadcasted_iota(jnp.int32, st.shape, 1)
            qry = jnp.where(qry >= tq, qry - tq, qry)
            st = jnp.where(key <= qry, st, -jnp.inf)
        m_old = m_sc[...]
        m_new = jnp.maximum(m_old, jnp.max(st, axis=0, keepdims=True))
        alpha = jnp.exp2(m_old - m_new)
        p = jnp.exp2(st - m_new).astype(BF16)
        start = pl.multiple_of(ki * tq, tq)
        vt = vt_ref[:, pl.ds(start, tq)]
        vt1 = jnp.concatenate([vt, jnp.ones((16, tq), BF16)], axis=0)
        r = jnp.dot(vt1, p, preferred_element_type=F32)
        l_sc[...] = alpha * l_sc[...] + r[V_DIM:V_DIM + 1]
        acc_sc[...] = alpha * acc_sc[...] + r[0:V_DIM]
        m_sc[...] = m_new

    scores(0, sta_sc)

    def pair(pi, carry):
        k0 = 2 * pi
        scores(k0 + 1, stb_sc)
        consume(k0, sta_sc, False)
        scores(k0 + 2, sta_sc)
        consume(k0 + 1, stb_sc, False)
        return carry

    lax.fori_loop(0, qi // 2, pair, 0)

    @pl.when(qi % 2 == 0)
    def _():
        consume(qi, sta_sc, True)

    @pl.when(qi % 2 == 1)
    def _():
        scores(qi, stb_sc)
        consume(qi - 1, sta_sc, False)
        consume(qi, stb_sc, True)

    lam = _lambda(lam_ref, lam_init)
    inv_l = 1.0 / l_sc[...]
    ot = acc_sc[:, 0:tq] * inv_l[:, 0:tq] - lam * (acc_sc[:, tq:2 * tq] * inv_l[:, tq:2 * tq])
    ms = jnp.mean(ot * ot, axis=0, keepdims=True)
    ot = ot * lax.rsqrt(ms + SUBLN_EPS) * g_ref[...] * (1.0 - lam_init)
    o_ref[...] = ot.T.astype(o_ref.dtype)


def _attn_prompt(qt_bf, k_bf, vt_bf, lam_p, subln_g, lam_init, tq):
    b, d, s = qt_bf.shape
    nh = d // V_DIM
    return pl.pallas_call(
        functools.partial(_attn_kernel, tq=tq, lam_init=lam_init),
        grid=(b, nh, s // tq),
        in_specs=[pl.BlockSpec(lam_p.shape, lambda bi, h, qi: (0, 0)),
                  pl.BlockSpec((None, V_DIM, tq), lambda bi, h, qi: (bi, h, qi)),
                  pl.BlockSpec((None, s, V_DIM), lambda bi, h, qi: (bi, 0, h)),
                  pl.BlockSpec((None, V_DIM, s), lambda bi, h, qi: (bi, h, 0)),
                  pl.BlockSpec((V_DIM, 1), lambda bi, h, qi: (0, 0))],
        out_specs=pl.BlockSpec((None, tq, V_DIM), lambda bi, h, qi: (bi, qi, h)),
        out_shape=jax.ShapeDtypeStruct((b, s, d), BF16),
        scratch_shapes=[pltpu.VMEM((V_DIM, 2 * tq), BF16),
                        pltpu.VMEM((tq, 2 * tq), F32),
                        pltpu.VMEM((tq, 2 * tq), F32),
                        pltpu.VMEM((1, 2 * tq), F32),
                        pltpu.VMEM((1, 2 * tq), F32),
                        pltpu.VMEM((V_DIM, 2 * tq), F32)],
        compiler_params=_params("arbitrary", "arbitrary", "arbitrary"),
        name="attn_prompt",
    )(lam_p, qt_bf, k_bf, vt_bf, subln_g.reshape(V_DIM, 1))


def _attn_sample_kernel(pt_ref, lam_ref, qt_ref, sn_ref, vn_ref, g_ref, *refs, n_pages, lam_init):
    del pt_ref
    kt_refs, v_refs = refs[:n_pages], refs[n_pages:2 * n_pages]
    o_ref, s_sc = refs[2 * n_pages:]
    b = pl.program_id(0)
    d, n_tok = qt_ref.shape
    page = kt_refs[0].shape[1]

    onehot = jnp.where(lax.broadcasted_iota(jnp.int32, (n_tok, page), 0) == b, 1.0, 0.0)
    qb = jnp.dot(qt_ref[...], onehot, preferred_element_type=F32,
                 precision=lax.Precision.HIGHEST)
    for p in range(n_pages):
        prod = (kt_refs[p][...] * qb).reshape(N_HEADS, V_DIM, page)
        s_sc[0, :, p * page:(p + 1) * page] = jnp.sum(prod[:, 0:HEAD_DIM, :], axis=1)
        s_sc[1, :, p * page:(p + 1) * page] = jnp.sum(prod[:, HEAD_DIM:V_DIM, :], axis=1)

    own = lax.broadcasted_iota(jnp.int32, (N_HEADS, n_tok), 1) == b

    def softmax_parts(which):
        s_new = jnp.max(jnp.where(own, sn_ref[which], -jnp.inf), axis=1, keepdims=True)
        s = s_sc[which]
        m = jnp.maximum(jnp.max(s, axis=1, keepdims=True), s_new)
        p = jnp.exp(s - m)
        p_new = jnp.exp(s_new - m)
        inv_l = 1.0 / (jnp.sum(p, axis=1, keepdims=True) + p_new)
        return p * inv_l, p_new * inv_l

    lam = _lambda(lam_ref, lam_init)
    w1, wn1 = softmax_parts(0)
    w2, wn2 = softmax_parts(1)
    w = w1 - lam * w2
    acc = (wn1 - lam * wn2) * vn_ref[...]
    for p in range(n_pages):
        for t in range(page):
            c = p * page + t
            acc = acc + w[:, c:c + 1] * v_refs[p][t]
    ms = jnp.mean(acc * acc, axis=1, keepdims=True)
    o_ref[...] = acc * lax.rsqrt(ms + SUBLN_EPS) * g_ref[...] * (1.0 - lam_init)


def _attn_sample(qt, s_new, v_new, cache_k, cache_v, page_table, lam_p, subln_g, lam_init):
    d, db = qt.shape
    n_pages = page_table.shape[1]
    n_phys, page = cache_k.shape[0], cache_k.shape[1]
    ck = jnp.transpose(cache_k, (0, 2, 3, 4, 1)).reshape(n_phys, d, page)
    pt = page_table.reshape(-1)

    def page_spec(p, shape):
        zeros = (0,) * len(shape)
        return pl.BlockSpec((None,) + shape, lambda b, pt_ref: (pt_ref[b * n_pages + p],) + zeros)

    const = lambda shape: pl.BlockSpec(shape, lambda b, pt_ref: (0,) * len(shape))
    head_spec = pl.BlockSpec((None, N_HEADS, V_DIM), lambda b, pt_ref: (b, 0, 0))
    grid_spec = pltpu.PrefetchScalarGridSpec(
        num_scalar_prefetch=1,
        grid=(db,),
        in_specs=[const(lam_p.shape), const((d, db)), const((2, N_HEADS, db)), head_spec, const((1, V_DIM))]
                 + [page_spec(p, (d, page)) for p in range(n_pages)]
                 + [page_spec(p, (page, N_HEADS, V_DIM)) for p in range(n_pages)],
        out_specs=head_spec,
        scratch_shapes=[pltpu.VMEM((2, N_HEADS, n_pages * page), F32)],
    )
    out = pl.pallas_call(
        functools.partial(_attn_sample_kernel, n_pages=n_pages, lam_init=lam_init),
        grid_spec=grid_spec,
        out_shape=jax.ShapeDtypeStruct((db, N_HEADS, V_DIM), F32),
        compiler_params=_params("arbitrary"),
        name="attn_sample",
    )(pt, lam_p, qt, s_new, v_new.reshape(db, N_HEADS, V_DIM), subln_g.reshape(1, V_DIM),
      *([ck] * n_pages), *([cache_v] * n_pages))
    return out.reshape(db, d)


POOL_HALO = 16


def _pool_tile(u_ref, halo_ref, ext_sc, seq_tile):
    tm = u_ref.shape[0]
    hb = halo_ref.shape[0]

    @pl.when(seq_tile == 0)
    def _():
        ext_sc[0:hb, :] = jnp.zeros((hb, ext_sc.shape[1]), F32)

    @pl.when(seq_tile > 0)
    def _():
        ext_sc[0:hb, :] = halo_ref[...]

    ext_sc[hb:hb + tm, :] = u_ref[...]
    pos = seq_tile * tm + lax.broadcasted_iota(jnp.int32, (tm, POOL_GROUP_DIM), 0)
    parts = []
    for g, win in enumerate(POOL_WINDOWS):
        sl = slice(g * POOL_GROUP_DIM, (g + 1) * POOL_GROUP_DIM)
        cur = ext_sc[hb:hb + tm, sl]
        tot = cur
        for j in range(1, win):
            tot = tot + ext_sc[hb - j:hb - j + tm, sl]
        cnt = jnp.minimum(win, pos + 1).astype(F32)
        parts.append(tot / cnt - cur)
    return jnp.concatenate(parts, axis=1)


def _pool_sample_kernel(st_ref, u_ref, o_ref, np_ref, *, start_pos):
    nb = st_ref.shape[0]
    u = u_ref[...]
    for g, win in enumerate(POOL_WINDOWS):
        sl = slice(g * POOL_GROUP_DIM, (g + 1) * POOL_GROUP_DIM)
        cur = u[:, sl]
        tot = cur
        for j in range(1, win):
            tot = tot + st_ref[nb - j, :, sl]
        o_ref[:, sl] = tot / float(min(win, start_pos + 1)) - cur
    for j in range(nb - 1):
        np_ref[j] = st_ref[j + 1]
    np_ref[nb - 1] = u


def _pool_sample(state, u, start_pos):
    db, nb, pw = state.shape
    full = lambda shape: pl.BlockSpec(shape, lambda i: (0,) * len(shape))
    pooled, new_pool = pl.pallas_call(
        functools.partial(_pool_sample_kernel, start_pos=start_pos),
        grid=(1,),
        in_specs=[full((nb, db, pw)), full((db, pw))],
        out_specs=[full((db, pw)), full((nb, db, pw))],
        out_shape=[jax.ShapeDtypeStruct((db, pw), F32), jax.ShapeDtypeStruct((nb, db, pw), F32)],
        compiler_params=_params("arbitrary"),
        name="pool_sample",
    )(jnp.transpose(state, (1, 0, 2)), u)
    return pooled, jnp.transpose(new_pool, (1, 0, 2))


def _route(lt):
    g = [lt[j:j + 1] for j in range(N_GROUPS)]
    gmax = functools.reduce(jnp.maximum, g)
    gidx = jnp.full(gmax.shape, N_GROUPS - 1, jnp.int32)
    for j in range(N_GROUPS - 2, -1, -1):
        gidx = jnp.where(g[j] == gmax, j, gidx)
    p_g = 1.0 / functools.reduce(jnp.add, [jnp.exp(gj - gmax) for gj in g])
    e = []
    for k in range(EXPERTS_PER_GROUP):
        sel = lt[N_GROUPS + k:N_GROUPS + k + 1]
        for j in range(1, N_GROUPS):
            r = N_GROUPS + j * EXPERTS_PER_GROUP + k
            sel = jnp.where(gidx == j, lt[r:r + 1], sel)
        e.append(sel)

    def first_argmax(vals):
        vmax = functools.reduce(jnp.maximum, vals)
        idx = jnp.full(vmax.shape, len(vals) - 1, jnp.int32)
        for k in range(len(vals) - 2, -1, -1):
            idx = jnp.where(vals[k] == vmax, k, idx)
        return vmax, idx

    v1, i1 = first_argmax(e)
    rest = [jnp.where(i1 == k, -jnp.inf, e[k]) for k in range(EXPERTS_PER_GROUP)]
    v2, i2 = first_argmax(rest)
    t = jnp.exp(v2 - v1)
    w1 = p_g / (1.0 + t)
    w2 = p_g * t / (1.0 + t)
    within = [jnp.where(i1 == k, w1, 0.0) + jnp.where(i2 == k, w2, 0.0) for k in range(EXPERTS_PER_GROUP)]
    rows = [jnp.where(gidx == j, within[k], 0.0) for j in range(N_GROUPS) for k in range(EXPERTS_PER_GROUP)]
    return jnp.concatenate(rows, axis=0), gidx


def _pack_bf16_pair(a, b):
    a_bits = lax.bitcast_convert_type(a.astype(BF16).astype(F32), jnp.uint32)
    b_bits = lax.bitcast_convert_type(b.astype(BF16).astype(F32), jnp.uint32)
    return (a_bits >> 16) | (b_bits & jnp.uint32(0xFFFF0000))


def _unpack_bf16_pair(w):
    lo = lax.bitcast_convert_type(w << 16, F32)
    hi = lax.bitcast_convert_type(w & jnp.uint32(0xFFFF0000), F32)
    return lo, hi


def _merge_kernel(*refs, alpha, tiles_per_seq):
    if tiles_per_seq:
        u_ref, halo_ref, *refs, ext_sc = refs
        pooled = _pool_tile(u_ref, halo_ref, ext_sc, pl.program_id(0) % tiles_per_seq)
    else:
        pooled_ref, *refs = refs
        pooled = pooled_ref[...]
    (gates_ref, attn_ref, x_ref, g1_ref, sc2_ref, sh2_ref, wpool_ref, pscale_ref, wout_ref, lng_ref, lnb_ref,
     wr_ref, br_ref, x1_ref, *route_refs) = refs
    d = x_ref.shape[1]
    tm = x_ref.shape[0]
    mm_dtype = wout_ref.dtype
    dot = functools.partial(jnp.dot, preferred_element_type=F32,
                            precision=lax.Precision.HIGHEST if mm_dtype == F32 else None)
    pooled = pooled.astype(mm_dtype)
    parts = [dot(pooled[:, g * POOL_GROUP_DIM:(g + 1) * POOL_GROUP_DIM], wpool_ref[g])
             for g in range(len(POOL_WINDOWS))]
    pool_out = jnp.concatenate(parts, axis=1) * pscale_ref[...]
    merged = (gates_ref[:, 0:d].astype(F32) * pool_out
              + gates_ref[:, d:2 * d].astype(F32) * attn_ref[...].astype(F32))
    y = dot(merged.astype(mm_dtype), wout_ref[...])
    x1 = _layer_norm(alpha * x_ref[...] + (1.0 + g1_ref[...]) * y, lng_ref[...], lnb_ref[...])
    x1_ref[...] = x1
    h2 = x1 * (1.0 + sc2_ref[...]) + sh2_ref[...]
    lt = lax.dot_general(wr_ref[...], h2, NT_DIMS, preferred_element_type=F32,
                         precision=lax.Precision.HIGHEST) + br_ref[...]
    comb_t, gidx = _route(lt)
    comb = jnp.concatenate([comb_t, jnp.zeros((LANES - N_EXPERTS, tm), F32)], axis=0).T
    if tiles_per_seq:
        row_ref, gid_ref = route_refs
        row_ref[:, 0:d // 2] = _pack_bf16_pair(h2[:, 0:d // 2], h2[:, d // 2:d])
        row_ref[:, d // 2:d // 2 + LANES] = lax.bitcast_convert_type(comb, jnp.uint32)
        gid_ref[...] = gidx
    else:
        h2_ref, comb_ref = route_refs
        h2_ref[...] = h2.astype(BF16)
        comb_ref[...] = comb


def _merge(pool_in, gates, attn, x, mod_specs, mod_arrs, w_pool, pool_scale, w_out,
           ln_g, ln_b, w_r, b_r, alpha, tm, seq_len=None):
    t, d = x.shape
    pw = pool_in.shape[1]
    row = lambda width: pl.BlockSpec((tm, width), lambda i: (i, 0))
    const = lambda shape: pl.BlockSpec(shape, lambda i: (0,) * len(shape))
    if seq_len:
        halo_spec = pl.BlockSpec((POOL_HALO, pw), lambda i: (jnp.maximum(i * (tm // POOL_HALO) - 1, 0), 0))
        pool_specs, pool_args = [row(pw), halo_spec], [pool_in, pool_in]
        scratch = [pltpu.VMEM((POOL_HALO + tm, pw), F32)]
    else:
        pool_specs, pool_args, scratch = [row(pw)], [pool_in], []
    if seq_len:
        route_specs = [row(d // 2 + LANES), pl.BlockSpec((None, 1, tm), lambda i: (i, 0, 0))]
        route_shapes = [jax.ShapeDtypeStruct((t, d // 2 + LANES), jnp.uint32),
                        jax.ShapeDtypeStruct((t // tm, 1, tm), jnp.int32)]
    else:
        route_specs = [row(d), row(LANES)]
        route_shapes = [jax.ShapeDtypeStruct((t, d), BF16), jax.ShapeDtypeStruct((t, LANES), F32)]
    return pl.pallas_call(
        functools.partial(_merge_kernel, alpha=alpha, tiles_per_seq=seq_len // tm if seq_len else None),
        grid=(t // tm,),
        in_specs=[*pool_specs, row(2 * d), row(d), row(d), *mod_specs,
                  const(w_pool.shape), const((1, d)), const((d, d)), const((1, d)), const((1, d)),
                  const(w_r.shape), const(b_r.shape)],
        out_specs=[row(d), *route_specs],
        out_shape=[jax.ShapeDtypeStruct((t, d), F32), *route_shapes],
        scratch_shapes=scratch,
        compiler_params=_params("arbitrary"),
        name="merge",
    )(*pool_args, gates, attn, x, *mod_arrs, w_pool, pool_scale.reshape(1, d), w_out,
      ln_g.reshape(1, d), ln_b.reshape(1, d), w_r, b_r)


def _moe_kernel(h2_ref, comb_ref, wg_ref, wu_ref, wd_ref, x1_ref, g2_ref, lng_ref, lnb_ref,
                o_ref, acc_sc, *, alpha):
    j = pl.program_id(1)
    eb = wg_ref.shape[0]

    @pl.when(j == 0)
    def _():
        acc_sc[...] = jnp.zeros(acc_sc.shape, F32)

    h2 = h2_ref[...]
    comb = comb_ref[...]
    lane = lax.broadcasted_iota(jnp.int32, comb.shape, 1)
    for e in range(eb):
        hg = jnp.dot(h2, wg_ref[e], preferred_element_type=F32)
        hu = jnp.dot(h2, wu_ref[e], preferred_element_type=F32)
        c = jnp.sum(jnp.where(lane == j * eb + e, comb, 0.0), axis=1, keepdims=True)
        act = (hg * _sigmoid(hg)) * hu * c
        acc_sc[...] += jnp.dot(act.astype(BF16), wd_ref[e], preferred_element_type=F32)

    @pl.when(j == pl.num_programs(1) - 1)
    def _():
        y = alpha * x1_ref[...] + (1.0 + g2_ref[...]) * acc_sc[...]
        o_ref[...] = _layer_norm(y, lng_ref[...], lnb_ref[...])


def _moe(h2, comb, wg_bf, wu_bf, wd_bf, x1, g2_spec, g2_arr, ln_g, ln_b, alpha, tm, eb):
    t, d = x1.shape
    ne, _, ff = wg_bf.shape
    row = lambda width: pl.BlockSpec((tm, width), lambda i, j: (i, 0))
    const = lambda shape: pl.BlockSpec(shape, lambda i, j: (0,) * len(shape))
    return pl.pallas_call(
        functools.partial(_moe_kernel, alpha=alpha),
        grid=(t // tm, ne // eb),
        in_specs=[row(d), row(LANES),
                  pl.BlockSpec((eb, d, ff), lambda i, j: (j, 0, 0)),
                  pl.BlockSpec((eb, d, ff), lambda i, j: (j, 0, 0)),
                  pl.BlockSpec((eb, ff, d), lambda i, j: (j, 0, 0)),
                  row(d), g2_spec, const((1, d)), const((1, d))],
        out_specs=row(d),
        out_shape=jax.ShapeDtypeStruct((t, d), F32),
        scratch_shapes=[pltpu.VMEM((tm, d), F32)],
        compiler_params=_params("arbitrary", "arbitrary"),
        name="moe_ffn",
    )(h2, comb, wg_bf, wu_bf, wd_bf, x1, g2_arr, ln_g.reshape(1, d), ln_b.reshape(1, d))


def _move_rows(x, idx, n_out=None):
    n, width = idx.shape[0], x.shape[1]
    scatter = n_out is not None
    info = pltpu.get_tpu_info().sparse_core
    units = info.num_cores * info.num_subcores
    per_unit = n // (units * GATHER_WINDOW)
    assert per_unit * units * GATHER_WINDOW == n

    @pl.kernel(out_type=jax.ShapeDtypeStruct((n_out if scatter else n, width), x.dtype),
               mesh=plsc.VectorSubcoreMesh(core_axis_name="core", subcore_axis_name="subcore"),
               scratch_types=[pltpu.VMEM((1, GATHER_WINDOW), jnp.int32), pltpu.VMEM((GATHER_WINDOW, width), x.dtype)])
    def move_kernel(x_hbm, i_hbm, o_hbm, i_vmem, buf):
        unit = lax.axis_index("core") * info.num_subcores + lax.axis_index("subcore")

        @pl.loop(0, per_unit)
        def _(j):
            base = (unit * per_unit + j) * GATHER_WINDOW
            pltpu.sync_copy(i_hbm.at[:, pl.ds(base, GATHER_WINDOW)], i_vmem)
            if scatter:
                pltpu.sync_copy(x_hbm.at[pl.ds(base, GATHER_WINDOW), :], buf)
                pltpu.sync_copy(buf, o_hbm.at[i_vmem.at[0]])
            else:
                pltpu.sync_copy(x_hbm.at[i_vmem.at[0]], buf)
                pltpu.sync_copy(buf, o_hbm.at[pl.ds(base, GATHER_WINDOW), :])

    return move_kernel(x, idx.reshape(1, n))


def _dispatch_plan(gid, tile, n_slots):
    t = gid.shape[0]
    onehot = (gid[:, None] == jnp.arange(N_GROUPS, dtype=jnp.int32)[None, :]).astype(jnp.int32)
    rank = jnp.cumsum(onehot, axis=0) - onehot
    counts = jnp.sum(onehot, axis=0)
    padded = -(-counts // tile) * tile
    ends = jnp.cumsum(padded)
    starts = ends - padded
    pos = jnp.sum(onehot * (starts[None, :] + rank), axis=1)
    tile_start = jnp.arange(n_slots // tile, dtype=jnp.int32) * tile
    tile_group = jnp.minimum(jnp.sum((tile_start[:, None] >= ends[None, :]).astype(jnp.int32), axis=1),
                             N_GROUPS - 1)
    tile_rows = jnp.clip((starts + counts)[tile_group] - tile_start, 0, tile)
    return pos, tile_group, tile_rows


def _moe_grouped_kernel(tg_ref, tr_ref, row_ref, wg_ref, wu_ref, wd_ref, y_ref):
    g = tg_ref[pl.program_id(0)]
    n_rows = tr_ref[pl.program_id(0)]
    half = y_ref.shape[1]

    @pl.when(n_rows == 0)
    def _():
        y_ref[...] = jnp.zeros(y_ref.shape, y_ref.dtype)

    @pl.when(n_rows > 0)
    def _():
        live = lax.broadcasted_iota(jnp.int32, (row_ref.shape[0], 1), 0) < n_rows
        lo, hi = _unpack_bf16_pair(row_ref[:, 0:half])
        h2 = jnp.where(live, jnp.concatenate([lo, hi], axis=1), 0.0).astype(BF16)
        comb = jnp.where(live, lax.bitcast_convert_type(row_ref[:, half:half + LANES], F32), 0.0)
        lane = lax.broadcasted_iota(jnp.int32, comb.shape, 1)
        acc = jnp.zeros((h2.shape[0], 2 * half), F32)
        for e in range(EXPERTS_PER_GROUP):
            hg = jnp.dot(h2, wg_ref[e], preferred_element_type=F32)
            hu = jnp.dot(h2, wu_ref[e], preferred_element_type=F32)
            c = jnp.sum(jnp.where(lane == g * EXPERTS_PER_GROUP + e, comb, 0.0), axis=1, keepdims=True)
            act = (hg * _sigmoid(hg)) * hu * c
            acc = acc + jnp.dot(act.astype(BF16), wd_ref[e], preferred_element_type=F32)
        y_ref[...] = _pack_bf16_pair(acc[:, 0:half], acc[:, half:2 * half])


def _moe_grouped(rows_sorted, tile_group, tile_rows, wg_bf, wu_bf, wd_bf, tile):
    n_slots, width = rows_sorted.shape
    ne, d, ff = wg_bf.shape
    group_w = lambda shape: pl.BlockSpec((EXPERTS_PER_GROUP,) + shape, lambda i, tg, tr: (tg[i], 0, 0))
    grid_spec = pltpu.PrefetchScalarGridSpec(
        num_scalar_prefetch=2,
        grid=(n_slots // tile,),
        in_specs=[pl.BlockSpec((tile, width), lambda i, tg, tr: (i, 0)),
                  group_w((d, ff)), group_w((d, ff)), group_w((ff, d))],
        out_specs=pl.BlockSpec((tile, d // 2), lambda i, tg, tr: (i, 0)),
    )
    return pl.pallas_call(
        _moe_grouped_kernel,
        grid_spec=grid_spec,
        out_shape=jax.ShapeDtypeStruct((n_slots, d // 2), jnp.uint32),
        compiler_params=_params("arbitrary"),
        name="moe_grouped",
    )(tile_group, tile_rows, rows_sorted, wg_bf, wu_bf, wd_bf)


def _moe_finish_kernel(x1_ref, y_ref, g2_ref, lng_ref, lnb_ref, o_ref, *, alpha):
    lo, hi = _unpack_bf16_pair(y_ref[...])
    y = jnp.concatenate([lo, hi], axis=1)
    o_ref[...] = _layer_norm(alpha * x1_ref[...] + (1.0 + g2_ref[...]) * y, lng_ref[...], lnb_ref[...])


def _moe_finish(x1, y_packed, g2_spec, g2_arr, ln_g, ln_b, alpha, tm):
    t, d = x1.shape
    row = lambda width: pl.BlockSpec((tm, width), lambda i: (i, 0))
    const = lambda shape: pl.BlockSpec(shape, lambda i: (0,) * len(shape))
    return pl.pallas_call(
        functools.partial(_moe_finish_kernel, alpha=alpha),
        grid=(t // tm,),
        in_specs=[row(d), row(d // 2), g2_spec, const((1, d)), const((1, d))],
        out_specs=row(d),
        out_shape=jax.ShapeDtypeStruct((t, d), F32),
        compiler_params=_params("arbitrary"),
        name="moe_finish",
    )(x1, y_packed, g2_arr, ln_g.reshape(1, d), ln_b.reshape(1, d))


def kernel(x_prompt, x_sample, cache_k, cache_v, state_pool, page_table, c_prompt, c_sample, w_ada, b_ada, w_in, lambda_q1, lambda_k1, lambda_q2, lambda_k2, subln_g, w_pool, pool_scale, w_out, ln1_g, ln1_b, w_group, b_group, w_router, b_router, w_e_gate, w_e_up, w_e_down, ln2_g, ln2_b):
    depth = w_in.shape[0]
    bsz, seq, d = x_prompt.shape
    db, dseq, _ = x_sample.shape
    assert dseq == 1, "sample group decodes one token per sequence"
    n_pages, page = page_table.shape[1], cache_k.shape[2]
    past_len = n_pages * page
    alpha = (2.0 * depth) ** 0.25
    tp, ts = bsz * seq, db * dseq

    TM_PROJ, TQ, TM_MERGE, TM_MOE, TM_GROUP, E_BLK = 256, 512, 512, 1024, 512, 4
    sc_info = pltpu.get_tpu_info().sparse_core
    sc_units = sc_info.num_cores * sc_info.num_subcores
    assert tp % (GATHER_WINDOW * sc_units) == 0

    xp = x_prompt.reshape(tp, d)
    xs = x_sample.reshape(ts, d)
    n_c = bsz + db
    n_c_pad = -(-n_c // 8) * 8
    c_all = jnp.concatenate([c_prompt, c_sample, jnp.zeros((n_c_pad - n_c, d), F32)], axis=0)
    tabs_p = _rope_tables(jnp.arange(seq))
    tabs_s = _rope_tables(jnp.full((ts,), past_len))

    outs = {k: [] for k in ("kp", "vp", "pp", "ks", "vs", "ps")}
    for l in range(depth):
        lam_init = 0.8 - 0.6 * math.exp(-0.3 * l)
        mod = _ada(c_all, w_ada[l], b_ada[l])
        mod_p = mod[:bsz].reshape(bsz * N_MOD, 1, d)
        mod_s = mod[bsz:n_c]
        w_in_bf = w_in[l].astype(BF16)
        w_pool_bf = w_pool[l].astype(BF16)
        w_out_bf = w_out[l].astype(BF16)
        wg_bf, wu_bf, wd_bf = (w[l].astype(BF16) for w in (w_e_gate, w_e_up, w_e_down))
        lam_p = jnp.stack([lambda_q1[l], lambda_k1[l], lambda_q2[l], lambda_k2[l]])
        n_r = N_GROUPS + N_EXPERTS
        w_r = jnp.concatenate([w_group[l], w_router[l]], axis=1).T
        w_r = jnp.concatenate([w_r, jnp.zeros((ROUTER_ROWS - n_r, d), F32)], axis=0)
        b_r = jnp.concatenate([b_group[l], b_router[l], jnp.zeros((ROUTER_ROWS - n_r,), F32)]).reshape(ROUTER_ROWS, 1)

        def pmod(k, tm, two_d=False):
            per_b = seq // tm
            if two_d:
                return pl.BlockSpec((None, 1, d), lambda i, j: ((i // per_b) * N_MOD + k, 0, 0))
            return pl.BlockSpec((None, 1, d), lambda i: ((i // per_b) * N_MOD + k, 0, 0))

        nblk = seq // TM_PROJ
        tab_t_spec = pl.BlockSpec((HEAD_DIM // 2, TM_PROJ), lambda i: (0, i % nblk))
        qt_bf, kt_f, k_bf, v_f, vt_bf, u, gates = _inproj(
            xp, pmod(1, TM_PROJ), pmod(0, TM_PROJ), mod_p, mod_p, w_in_bf,
            tabs_p, tab_t_spec, TM_PROJ, bsz, True)
        attn = _attn_prompt(qt_bf, k_bf.reshape(bsz, seq, d), vt_bf,
                            lam_p, subln_g[l], lam_init, TQ).reshape(tp, d)
        x1, ffn_rows, gid = _merge(u, gates, attn, xp,
                                   [pmod(2, TM_MERGE), pmod(4, TM_MERGE), pmod(3, TM_MERGE)], [mod_p] * 3,
                                   w_pool_bf, pool_scale[l], w_out_bf, ln1_g[l], ln1_b[l], w_r, b_r, alpha,
                                   TM_MERGE, seq_len=seq)
        slot_quantum = GATHER_WINDOW * sc_units
        n_slots = -(-(tp + N_GROUPS * (TM_GROUP - 1)) // slot_quantum) * slot_quantum
        pos, tile_group, tile_rows = _dispatch_plan(gid.reshape(tp), TM_GROUP, n_slots)
        rows_sorted = _move_rows(ffn_rows, pos, n_slots)
        outs["kp"].append(jnp.transpose(kt_f.reshape(bsz, N_HEADS, 2, HEAD_DIM, seq), (0, 4, 1, 2, 3)))
        outs["vp"].append(v_f.reshape(bsz, seq, N_HEADS, V_DIM))
        outs["pp"].append(u.reshape(bsz, seq, -1)[:, seq - POOL_BUF:])

        def smod(k, two_d=False):
            if two_d:
                return pl.BlockSpec((ts, d), lambda i, j: (0, k))
            return pl.BlockSpec((ts, d), lambda i: (0, k))

        tab_t_spec_s = pl.BlockSpec((HEAD_DIM // 2, ts), lambda i: (0, 0))
        qt_s, kt_f, s_new, v_f, u, gates = _inproj(
            xs, smod(1), smod(0), mod_s, mod_s, w_in[l],
            tabs_s, tab_t_spec_s, ts, 1, False)
        hb = ts // 2
        attn_half = lambda sl: _attn_sample(qt_s[0][:, sl], s_new[:, :, sl], v_f[sl], cache_k[l], cache_v[l],
                                            page_table[sl], lam_p, subln_g[l], lam_init)
        rows_sorted, attn_a = lax.optimization_barrier((rows_sorted, attn_half(slice(0, hb))))
        y_sorted = _moe_grouped(rows_sorted, tile_group, tile_rows, wg_bf, wu_bf, wd_bf, TM_GROUP)
        y_tok, attn_b = lax.optimization_barrier((_move_rows(y_sorted, pos), attn_half(slice(hb, ts))))
        xp = _moe_finish(x1, y_tok, pmod(5, TM_MOE), mod_p, ln2_g[l], ln2_b[l], alpha, TM_MOE)
        attn = jnp.concatenate([attn_a, attn_b], axis=0)
        pooled, new_pool = _pool_sample(state_pool[l], u, past_len)
        x1, h2, comb = _merge(pooled, gates, attn, xs, [smod(2), smod(4), smod(3)], [mod_s] * 3,
                              w_pool[l], pool_scale[l], w_out[l], ln1_g[l], ln1_b[l], w_r, b_r, alpha, ts)
        xs = _moe(h2, comb, wg_bf, wu_bf, wd_bf, x1, smod(5, True), mod_s,
                  ln2_g[l], ln2_b[l], alpha, ts, E_BLK)
        outs["ks"].append(jnp.transpose(kt_f.reshape(N_HEADS, 2, HEAD_DIM, db, dseq), (3, 4, 0, 1, 2)))
        outs["vs"].append(v_f.reshape(db, dseq, N_HEADS, V_DIM))
        outs["ps"].append(new_pool)

    return (xp.reshape(bsz, seq, d), xs.reshape(db, dseq, d),
            jnp.stack(outs["kp"]), jnp.stack(outs["vp"]), jnp.stack(outs["pp"]),
            jnp.stack(outs["ks"]), jnp.stack(outs["vs"]), jnp.stack(outs["ps"]))
```

```python
import functools
import math

import jax
import jax.numpy as jnp
from jax import lax
from jax.experimental import pallas as pl
from jax.experimental.pallas import tpu as pltpu
from jax.experimental.pallas import tpu_sc as plsc

F32 = jnp.float32
BF16 = jnp.bfloat16

LANES = 128
VMEM_LIMIT = 56 << 20

N_MOD = 6
N_HEADS = 8
HEAD_DIM = 64
V_DIM = 2 * HEAD_DIM
ROPE_THETA = 10000.0
POOL_WINDOWS = (2, 4, 8, 16)
POOL_GROUP_DIM = 128
POOL_BUF = max(POOL_WINDOWS) - 1
N_GROUPS = 4
EXPERTS_PER_GROUP = 4
N_EXPERTS = N_GROUPS * EXPERTS_PER_GROUP
LN_EPS = 1e-5
SUBLN_EPS = 1e-5
ROUTER_ROWS = 32
GATHER_WINDOW = 128

NT_DIMS = (((1,), (1,)), ((), ()))


def _params(*sem):
    return pltpu.CompilerParams(dimension_semantics=sem, vmem_limit_bytes=VMEM_LIMIT)


def _sigmoid(x):
    return 1.0 / (1.0 + jnp.exp(-x))


def _layer_norm(x, g, b):
    mu = jnp.mean(x, axis=-1, keepdims=True)
    xc = x - mu
    var = jnp.mean(xc * xc, axis=-1, keepdims=True)
    return xc * lax.rsqrt(var + LN_EPS) * g + b


def _lambda(lam_ref, lam_init):
    lp = lam_ref[...]
    a = jnp.sum(lp[0:1] * lp[1:2], axis=1, keepdims=True)
    b = jnp.sum(lp[2:3] * lp[3:4], axis=1, keepdims=True)
    return jnp.exp(a) - jnp.exp(b) + lam_init


def _ada_kernel(c_ref, w_ref, b_ref, o_ref):
    c = c_ref[...]
    o_ref[...] = jnp.dot(c * _sigmoid(c), w_ref[...], preferred_element_type=F32,
                         precision=lax.Precision.HIGHEST) + b_ref[...]


def _ada(c_all, w_ada, b_ada):
    n, d = c_all.shape
    cols = w_ada.shape[1]
    return pl.pallas_call(
        _ada_kernel,
        grid=(cols // d,),
        in_specs=[pl.BlockSpec((n, d), lambda j: (0, 0)),
                  pl.BlockSpec((d, d), lambda j: (0, j)),
                  pl.BlockSpec((1, d), lambda j: (0, j))],
        out_specs=pl.BlockSpec((n, d), lambda j: (0, j)),
        out_shape=jax.ShapeDtypeStruct((n, cols), F32),
        compiler_params=_params("arbitrary"),
        name="ada_mod",
    )(c_all, w_ada, b_ada.reshape(1, cols))


def _inproj_kernel(x_ref, sc_ref, sh_ref, w_ref, cost_ref, sint_ref, *out_refs, d, pw, prompt):
    mm_dtype = w_ref.dtype
    prec = lax.Precision.HIGHEST if mm_dtype == F32 else None
    dot = functools.partial(jnp.dot, preferred_element_type=F32, precision=prec)
    h =(x_ref[...] * (1.0 + sc_ref[...]) + sh_ref[...]).astype(mm_dtype)
    half = HEAD_DIM // 2
    scale = HEAD_DIM ** -0.5 * (math.log2(math.e) if prompt else 1.0)
    cost, sint = cost_ref[...], sint_ref[...]

    def rope_rows(xt):
        out = []
        for g in range(d // HEAD_DIM):
            lo = slice(g * HEAD_DIM, g * HEAD_DIM + half)
            hi = slice(g * HEAD_DIM + half, (g + 1) * HEAD_DIM)
            x1, x2 = xt[lo], xt[hi]
            out.append((lo, x1 * cost - x2 * sint))
            out.append((hi, x2 * cost + x1 * sint))
        return out

    qt = rope_rows(dot(h, w_ref[:, 0:d]).T)
    kt = rope_rows(dot(h, w_ref[:, d:2 * d]).T)
    v = dot(h, w_ref[:, 2 * d:3 * d])
    if prompt:
        qt_ref, ktf_ref, kb_ref, vf_ref, vtb_ref, u_ref, g_ref = out_refs
        for sl, r in qt:
            qt_ref[sl, :] = (r * scale).astype(BF16)
        for sl, r in kt:
            ktf_ref[sl, :] = r
        kb_ref[...] = ktf_ref[...].T.astype(BF16)
        vtb_ref[...] = v.T.astype(BF16)
        g_dtype = BF16
    else:
        g_dtype = F32
        qt_ref, ktf_ref, sn_ref, vf_ref, u_ref, g_ref = out_refs
        for idx, ((sl, rq), (_, rk)) in enumerate(zip(qt, kt)):
            qs = rq * scale
            qt_ref[sl, :] = qs
            ktf_ref[sl, :] = rk
            part = jnp.sum(qs * rk, axis=0, keepdims=True)
            g, is_hi = idx // 2, idx % 2
            head, which = g // 2, g % 2
            if is_hi:
                sn_ref[which, head:head + 1, :] += part
            else:
                sn_ref[which, head:head + 1, :] = part
    vf_ref[...] = v
    u_ref[...] = dot(h, w_ref[:, 3 * d:3 * d + pw])
    g_ref[...] = _sigmoid(dot(h, w_ref[:, 3 * d + pw:])).astype(g_dtype)


def _inproj(x, sc_spec, sh_spec, sc_arr, sh_arr, w_in, tabs_t, tab_t_spec, tm, n_seq, prompt):
    t, d = x.shape
    cols = w_in.shape[1]
    pw = cols - 5 * d
    per_seq = t // n_seq
    nblk = per_seq // tm
    row = lambda width: pl.BlockSpec((tm, width), lambda i: (i, 0))
    ft_spec = pl.BlockSpec((None, d, tm), lambda i: (i // nblk, 0, i % nblk))
    ft_shape = lambda dt: jax.ShapeDtypeStruct((n_seq, d, per_seq), dt)
    tok_shape = lambda width, dt: jax.ShapeDtypeStruct((t, width), dt)
    if prompt:
        out_specs = [ft_spec, ft_spec, row(d), row(d), ft_spec, row(pw), row(2 * d)]
        out_shape = [ft_shape(BF16), ft_shape(F32), tok_shape(d, BF16), tok_shape(d, F32), ft_shape(BF16),
                     tok_shape(pw, F32), tok_shape(2 * d, BF16)]
    else:
        assert n_seq == 1 and nblk == 1
        out_specs = [ft_spec, ft_spec, pl.BlockSpec((2, N_HEADS, tm), lambda i: (0, 0, 0)), row(d),
                     row(pw), row(2 * d)]
        out_shape = [ft_shape(F32), ft_shape(F32), jax.ShapeDtypeStruct((2, N_HEADS, t), F32),
                     tok_shape(d, F32), tok_shape(pw, F32), tok_shape(2 * d, F32)]
    return pl.pallas_call(
        functools.partial(_inproj_kernel, d=d, pw=pw, prompt=prompt),
        grid=(t // tm,),
        in_specs=[row(d), sc_spec, sh_spec,
                  pl.BlockSpec((d, cols), lambda i: (0, 0), pipeline_mode=pl.Buffered(1)),
                  tab_t_spec, tab_t_spec],
        out_specs=out_specs,
        out_shape=out_shape,
        compiler_params=_params("arbitrary"),
        name="in_proj",
    )(x, sc_arr, sh_arr, w_in, *tabs_t)


def _rope_tables(pos):
    half = HEAD_DIM // 2
    inv = ROPE_THETA ** (-jnp.arange(half, dtype=F32) / half)
    ang = inv[:, None] * pos.astype(F32)[None, :]
    return jnp.cos(ang), jnp.sin(ang)


def _attn_kernel(lam_ref, qt_ref, k_ref, vt_ref, g_ref, o_ref, q2_sc, sta_sc, stb_sc, m_sc, l_sc, acc_sc,
                 *, tq, lam_init):
    qi = pl.program_id(2)
    zero = jnp.zeros((HEAD_DIM, tq), BF16)
    q2_sc[0:HEAD_DIM, 0:tq] = qt_ref[0:HEAD_DIM, :]
    q2_sc[HEAD_DIM:V_DIM, 0:tq] = zero
    q2_sc[0:HEAD_DIM, tq:2 * tq] = zero
    q2_sc[HEAD_DIM:V_DIM, tq:2 * tq] = qt_ref[HEAD_DIM:V_DIM, :]
    m_sc[...] = jnp.full(m_sc.shape, -jnp.inf, F32)
    l_sc[...] = jnp.zeros(l_sc.shape, F32)
    acc_sc[...] = jnp.zeros(acc_sc.shape, F32)

    def scores(ki, st_ref):
        start = pl.multiple_of(ki * tq, tq)
        st_ref[...] = jnp.dot(k_ref[pl.ds(start, tq), :], q2_sc[...], preferred_element_type=F32)

    def consume(ki, st_ref, masked):
        st = st_ref[...]
        if masked:
            key = lax.broadcasted_iota(jnp.int32, st.shape, 0)
            qry = lax.broadcasted_iota(jnp.int32, st.shape, 1)
            qry = jnp.where(qry >= tq, qry - tq, qry)
            st = jnp.where(key <= qry, st, -jnp.inf)
        m_old = m_sc[...]
        m_new = jnp.maximum(m_old, jnp.max(st, axis=0, keepdims=True))
        alpha = jnp.exp2(m_old - m_new)
        p = jnp.exp2(st - m_new).astype(BF16)
        start = pl.multiple_of(ki * tq, tq)
        vt = vt_ref[:, pl.ds(start, tq)]
        vt1 = jnp.concatenate([vt, jnp.ones((16, tq), BF16)], axis=0)
        r = jnp.dot(vt1, p, preferred_element_type=F32)
        l_sc[...] = alpha * l_sc[...] + r[V_DIM:V_DIM + 1]
        acc_sc[...] = alpha * acc_sc[...] + r[0:V_DIM]
        m_sc[...] = m_new

    scores(0, sta_sc)

    def pair(pi, carry):
        k0 = 2 * pi
        scores(k0 + 1, stb_sc)
        consume(k0, sta_sc, False)
        scores(k0 + 2, sta_sc)
        consume(k0 + 1, stb_sc, False)
        return carry

    lax.fori_loop(0, qi // 2, pair, 0)

    @pl.when(qi % 2 == 0)
    def _():
        consume(qi, sta_sc, True)

    @pl.when(qi % 2 == 1)
    def _():
        scores(qi, stb_sc)
        consume(qi - 1, sta_sc, False)
        consume(qi, stb_sc, True)

    lam = _lambda(lam_ref, lam_init)
    inv_l = 1.0 / l_sc[...]
    ot = acc_sc[:, 0:tq] * inv_l[:, 0:tq] - lam * (acc_sc[:, tq:2 * tq] * inv_l[:, tq:2 * tq])
    ms = jnp.mean(ot * ot, axis=0, keepdims=True)
    ot = ot * lax.rsqrt(ms + SUBLN_EPS) * g_ref[...] * (1.0 - lam_init)
    o_ref[...] = ot.T.astype(o_ref.dtype)


def _attn_prompt(qt_bf, k_bf, vt_bf, lam_p, subln_g, lam_init, tq):
    b, d, s = qt_bf.shape
    nh = d // V_DIM
    return pl.pallas_call(
        functools.partial(_attn_kernel, tq=tq, lam_init=lam_init),
        grid=(b, nh, s // tq),
        in_specs=[pl.BlockSpec(lam_p.shape, lambda bi, h, qi: (0, 0)),
                  pl.BlockSpec((None, V_DIM, tq), lambda bi, h, qi: (bi, h, qi)),
                  pl.BlockSpec((None, s, V_DIM), lambda bi, h, qi: (bi, 0, h)),
                  pl.BlockSpec((None, V_DIM, s), lambda bi, h, qi: (bi, h, 0)),
                  pl.BlockSpec((V_DIM, 1), lambda bi, h, qi: (0, 0))],
        out_specs=pl.BlockSpec((None, tq, V_DIM), lambda bi, h, qi: (bi, qi, h)),
        out_shape=jax.ShapeDtypeStruct((b, s, d), BF16),
        scratch_shapes=[pltpu.VMEM((V_DIM, 2 * tq), BF16),
                        pltpu.VMEM((tq, 2 * tq), F32),
                        pltpu.VMEM((tq, 2 * tq), F32),
                        pltpu.VMEM((1, 2 * tq), F32),
                        pltpu.VMEM((1, 2 * tq), F32),
                        pltpu.VMEM((V_DIM, 2 * tq), F32)],
        compiler_params=_params("arbitrary", "arbitrary", "arbitrary"),
        name="attn_prompt",
    )(lam_p, qt_bf, k_bf, vt_bf, subln_g.reshape(V_DIM, 1))


def _attn_sample_kernel(pt_ref, lam_ref, qt_ref, sn_ref, vn_ref, g_ref, *refs, n_pages, lam_init):
    del pt_ref
    kt_refs, v_refs = refs[:n_pages], refs[n_pages:2 * n_pages]
    o_ref, s_sc = refs[2 * n_pages:]
    b = pl.program_id(0)
    d, n_tok = qt_ref.shape
    page = kt_refs[0].shape[1]

    onehot = jnp.where(lax.broadcasted_iota(jnp.int32, (n_tok, page), 0) == b, 1.0, 0.0)
    qb = jnp.dot(qt_ref[...], onehot, preferred_element_type=F32,
                 precision=lax.Precision.HIGHEST)
    for p in range(n_pages):
        prod = (kt_refs[p][...] * qb).reshape(N_HEADS, V_DIM, page)
        s_sc[0, :, p * page:(p + 1) * page] = jnp.sum(prod[:, 0:HEAD_DIM, :], axis=1)
        s_sc[1, :, p * page:(p + 1) * page] = jnp.sum(prod[:, HEAD_DIM:V_DIM, :], axis=1)

    own = lax.broadcasted_iota(jnp.int32, (N_HEADS, n_tok), 1) == b

    def softmax_parts(which):
        s_new = jnp.max(jnp.where(own, sn_ref[which], -jnp.inf), axis=1, keepdims=True)
        s = s_sc[which]
        m = jnp.maximum(jnp.max(s, axis=1, keepdims=True), s_new)
        p = jnp.exp(s - m)
        p_new = jnp.exp(s_new - m)
        inv_l = 1.0 / (jnp.sum(p, axis=1, keepdims=True) + p_new)
        return p * inv_l, p_new * inv_l

    lam = _lambda(lam_ref, lam_init)
    w1, wn1 = softmax_parts(0)
    w2, wn2 = softmax_parts(1)
    w = w1 - lam * w2
    acc = (wn1 - lam * wn2) * vn_ref[...]
    for p in range(n_pages):
        for t in range(page):
            c = p * page + t
            acc = acc + w[:, c:c + 1] * v_refs[p][t]
    ms = jnp.mean(acc * acc, axis=1, keepdims=True)
    o_ref[...] = acc * lax.rsqrt(ms + SUBLN_EPS) * g_ref[...] * (1.0 - lam_init)


def _attn_sample(qt, s_new, v_new, cache_k, cache_v, page_table, lam_p, subln_g, lam_init):
    d, db = qt.shape
    n_pages = page_table.shape[1]
    n_phys, page = cache_k.shape[0], cache_k.shape[1]
    ck = jnp.transpose(cache_k, (0, 2, 3, 4, 1)).reshape(n_phys, d, page)
    pt = page_table.reshape(-1)

    def page_spec(p, shape):
        zeros = (0,) * len(shape)
        return pl.BlockSpec((None,) + shape, lambda b, pt_ref: (pt_ref[b * n_pages + p],) + zeros)

    const = lambda shape: pl.BlockSpec(shape, lambda b, pt_ref: (0,) * len(shape))
    head_spec = pl.BlockSpec((None, N_HEADS, V_DIM), lambda b, pt_ref: (b, 0, 0))
    grid_spec = pltpu.PrefetchScalarGridSpec(
        num_scalar_prefetch=1,
        grid=(db,),
        in_specs=[const(lam_p.shape), const((d, db)), const((2, N_HEADS, db)), head_spec, const((1, V_DIM))]
                 + [page_spec(p, (d, page)) for p in range(n_pages)]
                 + [page_spec(p, (page, N_HEADS, V_DIM)) for p in range(n_pages)],
        out_specs=head_spec,
        scratch_shapes=[pltpu.VMEM((2, N_HEADS, n_pages * page), F32)],
    )
    out = pl.pallas_call(
        functools.partial(_attn_sample_kernel, n_pages=n_pages, lam_init=lam_init),
        grid_spec=grid_spec,
        out_shape=jax.ShapeDtypeStruct((db, N_HEADS, V_DIM), F32),
        compiler_params=_params("arbitrary"),
        name="attn_sample",
    )(pt, lam_p, qt, s_new, v_new.reshape(db, N_HEADS, V_DIM), subln_g.reshape(1, V_DIM),
      *([ck] * n_pages), *([cache_v] * n_pages))
    return out.reshape(db, d)


POOL_HALO = 16


def _pool_tile(u_ref, halo_ref, ext_sc, seq_tile):
    tm = u_ref.shape[0]
    hb = halo_ref.shape[0]

    @pl.when(seq_tile == 0)
    def _():
        ext_sc[0:hb, :] = jnp.zeros((hb, ext_sc.shape[1]), F32)

    @pl.when(seq_tile > 0)
    def _():
        ext_sc[0:hb, :] = halo_ref[...]

    ext_sc[hb:hb + tm, :] = u_ref[...]
    pos = seq_tile * tm + lax.broadcasted_iota(jnp.int32, (tm, POOL_GROUP_DIM), 0)
    parts = []
    for g, win in enumerate(POOL_WINDOWS):
        sl = slice(g * POOL_GROUP_DIM, (g + 1) * POOL_GROUP_DIM)
        cur = ext_sc[hb:hb + tm, sl]
        tot = cur
        for j in range(1, win):
            tot = tot + ext_sc[hb - j:hb - j + tm, sl]
        cnt = jnp.minimum(win, pos + 1).astype(F32)
        parts.append(tot / cnt - cur)
    return jnp.concatenate(parts, axis=1)


def _pool_sample_kernel(st_ref, u_ref, o_ref, np_ref, *, start_pos):
    nb = st_ref.shape[0]
    u = u_ref[...]
    for g, win in enumerate(POOL_WINDOWS):
        sl = slice(g * POOL_GROUP_DIM, (g + 1) * POOL_GROUP_DIM)
        cur = u[:, sl]
        tot = cur
        for j in range(1, win):
            tot = tot + st_ref[nb - j, :, sl]
        o_ref[:, sl] = tot / float(min(win, start_pos + 1)) - cur
    for j in range(nb - 1):
        np_ref[j] = st_ref[j + 1]
    np_ref[nb - 1] = u


def _pool_sample(state, u, start_pos):
    db, nb, pw = state.shape
    full = lambda shape: pl.BlockSpec(shape, lambda i: (0,) * len(shape))
    pooled, new_pool = pl.pallas_call(
        functools.partial(_pool_sample_kernel, start_pos=start_pos),
        grid=(1,),
        in_specs=[full((nb, db, pw)), full((db, pw))],
        out_specs=[full((db, pw)), full((nb, db, pw))],
        out_shape=[jax.ShapeDtypeStruct((db, pw), F32), jax.ShapeDtypeStruct((nb, db, pw), F32)],
        compiler_params=_params("arbitrary"),
        name="pool_sample",
    )(jnp.transpose(state, (1, 0, 2)), u)
    return pooled, jnp.transpose(new_pool, (1, 0, 2))


def _route(lt):
    g = [lt[j:j + 1] for j in range(N_GROUPS)]
    gmax = functools.reduce(jnp.maximum, g)
    gidx = jnp.full(gmax.shape, N_GROUPS - 1, jnp.int32)
    for j in range(N_GROUPS - 2, -1, -1):
        gidx = jnp.where(g[j] == gmax, j, gidx)
    p_g = 1.0 / functools.reduce(jnp.add, [jnp.exp(gj - gmax) for gj in g])
    e = []
    for k in range(EXPERTS_PER_GROUP):
        sel = lt[N_GROUPS + k:N_GROUPS + k + 1]
        for j in range(1, N_GROUPS):
            r = N_GROUPS + j * EXPERTS_PER_GROUP + k
            sel = jnp.where(gidx == j, lt[r:r + 1], sel)
        e.append(sel)

    def first_argmax(vals):
        vmax = functools.reduce(jnp.maximum, vals)
        idx = jnp.full(vmax.shape, len(vals) - 1, jnp.int32)
        for k in range(len(vals) - 2, -1, -1):
            idx = jnp.where(vals[k] == vmax, k, idx)
        return vmax, idx

    v1, i1 = first_argmax(e)
    rest = [jnp.where(i1 == k, -jnp.inf, e[k]) for k in range(EXPERTS_PER_GROUP)]
    v2, i2 = first_argmax(rest)
    t = jnp.exp(v2 - v1)
    w1 = p_g / (1.0 + t)
    w2 = p_g * t / (1.0 + t)
    within = [jnp.where(i1 == k, w1, 0.0) + jnp.where(i2 == k, w2, 0.0) for k in range(EXPERTS_PER_GROUP)]
    rows = [jnp.where(gidx == j, within[k], 0.0) for j in range(N_GROUPS) for k in range(EXPERTS_PER_GROUP)]
    return jnp.concatenate(rows, axis=0), gidx


def _pack_bf16_pair(a, b):
    a_bits = lax.bitcast_convert_type(a.astype(BF16).astype(F32), jnp.uint32)
    b_bits = lax.bitcast_convert_type(b.astype(BF16).astype(F32), jnp.uint32)
    return (a_bits >> 16) | (b_bits & jnp.uint32(0xFFFF0000))


def _unpack_bf16_pair(w):
    lo = lax.bitcast_convert_type(w << 16, F32)
    hi = lax.bitcast_convert_type(w & jnp.uint32(0xFFFF0000), F32)
    return lo, hi


def _merge_kernel(*refs, alpha, tiles_per_seq):
    if tiles_per_seq:
        u_ref, halo_ref, *refs, ext_sc = refs
        pooled = _pool_tile(u_ref, halo_ref, ext_sc, pl.program_id(0) % tiles_per_seq)
    else:
        pooled_ref, *refs = refs
        pooled = pooled_ref[...]
    (gates_ref, attn_ref, x_ref, g1_ref, sc2_ref, sh2_ref, wpool_ref, pscale_ref, wout_ref, lng_ref, lnb_ref,
     wr_ref, br_ref, x1_ref, *route_refs) = refs
    d = x_ref.shape[1]
    tm = x_ref.shape[0]
    mm_dtype = wout_ref.dtype
    dot = functools.partial(jnp.dot, preferred_element_type=F32,
                            precision=lax.Precision.HIGHEST if mm_dtype == F32 else None)
    pooled = pooled.astype(mm_dtype)
    parts = [dot(pooled[:, g * POOL_GROUP_DIM:(g + 1) * POOL_GROUP_DIM], wpool_ref[g])
             for g in range(len(POOL_WINDOWS))]
    pool_out = jnp.concatenate(parts, axis=1) * pscale_ref[...]
    g_dtype = gates_ref.dtype
    merged = (gates_ref[:, 0:d] * pool_out.astype(g_dtype)
              + gates_ref[:, d:2 * d] * attn_ref[...].astype(g_dtype))
    y = dot(merged.astype(mm_dtype), wout_ref[...])
    x1 = _layer_norm(alpha * x_ref[...] + (1.0 + g1_ref[...]) * y, lng_ref[...], lnb_ref[...])
    x1_ref[...] = x1
    h2 = x1 * (1.0 + sc2_ref[...]) + sh2_ref[...]
    lt = lax.dot_general(wr_ref[...], h2, NT_DIMS, preferred_element_type=F32,
                         precision=lax.Precision.HIGHEST) + br_ref[...]
    comb_t, gidx = _route(lt)
    comb = jnp.concatenate([comb_t, jnp.zeros((LANES - N_EXPERTS, tm), F32)], axis=0).T
    if tiles_per_seq:
        row_ref, gid_ref = route_refs
        row_ref[:, 0:d // 2] = _pack_bf16_pair(h2[:, 0:d // 2], h2[:, d // 2:d])
        row_ref[:, d // 2:d // 2 + LANES] = lax.bitcast_convert_type(comb, jnp.uint32)
        gid_ref[...] = gidx
    else:
        h2_ref, comb_ref = route_refs
        h2_ref[...] = h2.astype(BF16)
        comb_ref[...] = comb


def _merge(pool_in, gates, attn, x, mod_specs, mod_arrs, w_pool, pool_scale, w_out,
           ln_g, ln_b, w_r, b_r, alpha, tm, seq_len=None):
    t, d = x.shape
    pw = pool_in.shape[1]
    row = lambda width: pl.BlockSpec((tm, width), lambda i: (i, 0))
    const = lambda shape: pl.BlockSpec(shape, lambda i: (0,) * len(shape))
    if seq_len:
        halo_spec = pl.BlockSpec((POOL_HALO, pw), lambda i: (jnp.maximum(i * (tm // POOL_HALO) - 1, 0), 0))
        pool_specs, pool_args = [row(pw), halo_spec], [pool_in, pool_in]
        scratch = [pltpu.VMEM((POOL_HALO + tm, pw), F32)]
    else:
        pool_specs, pool_args, scratch = [row(pw)], [pool_in], []
    if seq_len:
        route_specs = [row(d // 2 + LANES), pl.BlockSpec((None, 1, tm), lambda i: (i, 0, 0))]
        route_shapes = [jax.ShapeDtypeStruct((t, d // 2 + LANES), jnp.uint32),
                        jax.ShapeDtypeStruct((t // tm, 1, tm), jnp.int32)]
    else:
        route_specs = [row(d), row(LANES)]
        route_shapes = [jax.ShapeDtypeStruct((t, d), BF16), jax.ShapeDtypeStruct((t, LANES), F32)]
    return pl.pallas_call(
        functools.partial(_merge_kernel, alpha=alpha, tiles_per_seq=seq_len // tm if seq_len else None),
        grid=(t // tm,),
        in_specs=[*pool_specs, row(2 * d), row(d), row(d), *mod_specs,
                  const(w_pool.shape), const((1, d)), const((d, d)), const((1, d)), const((1, d)),
                  const(w_r.shape), const(b_r.shape)],
        out_specs=[row(d), *route_specs],
        out_shape=[jax.ShapeDtypeStruct((t, d), F32), *route_shapes],
        scratch_shapes=scratch,
        compiler_params=_params("arbitrary"),
        name="merge",
    )(*pool_args, gates, attn, x, *mod_arrs, w_pool, pool_scale.reshape(1, d), w_out,
      ln_g.reshape(1, d), ln_b.reshape(1, d), w_r, b_r)


def _moe_kernel(h2_ref, comb_ref, wg_ref, wu_ref, wd_ref, x1_ref, g2_ref, lng_ref, lnb_ref,
                o_ref, acc_sc, *, alpha):
    j = pl.program_id(1)
    eb = wg_ref.shape[0]

    @pl.when(j == 0)
    def _():
        acc_sc[...] = jnp.zeros(acc_sc.shape, F32)

    h2 = h2_ref[...]
    comb = comb_ref[...]
    lane = lax.broadcasted_iota(jnp.int32, comb.shape, 1)
    for e in range(eb):
        hg = jnp.dot(h2, wg_ref[e], preferred_element_type=F32)
        hu = jnp.dot(h2, wu_ref[e], preferred_element_type=F32)
        c = jnp.sum(jnp.where(lane == j * eb + e, comb, 0.0), axis=1, keepdims=True)
        act = (hg * _sigmoid(hg)) * hu * c
        acc_sc[...] += jnp.dot(act.astype(BF16), wd_ref[e], preferred_element_type=F32)

    @pl.when(j == pl.num_programs(1) - 1)
    def _():
        y = alpha * x1_ref[...] + (1.0 + g2_ref[...]) * acc_sc[...]
        o_ref[...] = _layer_norm(y, lng_ref[...], lnb_ref[...])


def _moe(h2, comb, wg_bf, wu_bf, wd_bf, x1, g2_spec, g2_arr, ln_g, ln_b, alpha, tm, eb):
    t, d = x1.shape
    ne, _, ff = wg_bf.shape
    row = lambda width: pl.BlockSpec((tm, width), lambda i, j: (i, 0))
    const = lambda shape: pl.BlockSpec(shape, lambda i, j: (0,) * len(shape))
    return pl.pallas_call(
        functools.partial(_moe_kernel, alpha=alpha),
        grid=(t // tm, ne // eb),
        in_specs=[row(d), row(LANES),
                  pl.BlockSpec((eb, d, ff), lambda i, j: (j, 0, 0)),
                  pl.BlockSpec((eb, d, ff), lambda i, j: (j, 0, 0)),
                  pl.BlockSpec((eb, ff, d), lambda i, j: (j, 0, 0)),
                  row(d), g2_spec, const((1, d)), const((1, d))],
        out_specs=row(d),
        out_shape=jax.ShapeDtypeStruct((t, d), F32),
        scratch_shapes=[pltpu.VMEM((tm, d), F32)],
        compiler_params=_params("arbitrary", "arbitrary"),
        name="moe_ffn",
    )(h2, comb, wg_bf, wu_bf, wd_bf, x1, g2_arr, ln_g.reshape(1, d), ln_b.reshape(1, d))


def _move_rows(x, idx, n_out=None):
    n, width = idx.shape[0], x.shape[1]
    scatter = n_out is not None
    info = pltpu.get_tpu_info().sparse_core
    units = info.num_cores * info.num_subcores
    per_unit = n // (units * GATHER_WINDOW)
    assert per_unit * units * GATHER_WINDOW == n

    @pl.kernel(out_type=jax.ShapeDtypeStruct((n_out if scatter else n, width), x.dtype),
               mesh=plsc.VectorSubcoreMesh(core_axis_name="core", subcore_axis_name="subcore"),
               scratch_types=[pltpu.VMEM((1, GATHER_WINDOW), jnp.int32), pltpu.VMEM((GATHER_WINDOW, width), x.dtype)])
    def move_kernel(x_hbm, i_hbm, o_hbm, i_vmem, buf):
        unit = lax.axis_index("core") * info.num_subcores + lax.axis_index("subcore")

        @pl.loop(0, per_unit)
        def _(j):
            base = (unit * per_unit + j) * GATHER_WINDOW
            pltpu.sync_copy(i_hbm.at[:, pl.ds(base, GATHER_WINDOW)], i_vmem)
            if scatter:
                pltpu.sync_copy(x_hbm.at[pl.ds(base, GATHER_WINDOW), :], buf)
                pltpu.sync_copy(buf, o_hbm.at[i_vmem.at[0]])
            else:
                pltpu.sync_copy(x_hbm.at[i_vmem.at[0]], buf)
                pltpu.sync_copy(buf, o_hbm.at[pl.ds(base, GATHER_WINDOW), :])

    return move_kernel(x, idx.reshape(1, n))


def _dispatch_plan(gid, tile, n_slots):
    t = gid.shape[0]
    onehot = (gid[:, None] == jnp.arange(N_GROUPS, dtype=jnp.int32)[None, :]).astype(jnp.int32)
    rank = jnp.cumsum(onehot, axis=0) - onehot
    counts = jnp.sum(onehot, axis=0)
    padded = -(-counts // tile) * tile
    ends = jnp.cumsum(padded)
    starts = ends - padded
    pos = jnp.sum(onehot * (starts[None, :] + rank), axis=1)
    tile_start = jnp.arange(n_slots // tile, dtype=jnp.int32) * tile
    tile_group = jnp.minimum(jnp.sum((tile_start[:, None] >= ends[None, :]).astype(jnp.int32), axis=1),
                             N_GROUPS - 1)
    tile_rows = jnp.clip((starts + counts)[tile_group] - tile_start, 0, tile)
    return pos, tile_group, tile_rows


def _moe_grouped_kernel(tg_ref, tr_ref, row_ref, wg_ref, wu_ref, wd_ref, y_ref):
    g = tg_ref[pl.program_id(0)]
    n_rows = tr_ref[pl.program_id(0)]
    half = y_ref.shape[1]

    @pl.when(n_rows == 0)
    def _():
        y_ref[...] = jnp.zeros(y_ref.shape, y_ref.dtype)

    @pl.when(n_rows > 0)
    def _():
        live = lax.broadcasted_iota(jnp.int32, (row_ref.shape[0], 1), 0) < n_rows
        lo, hi = _unpack_bf16_pair(row_ref[:, 0:half])
        h2 = jnp.where(live, jnp.concatenate([lo, hi], axis=1), 0.0).astype(BF16)
        comb = jnp.where(live, lax.bitcast_convert_type(row_ref[:, half:half + LANES], F32), 0.0)
        lane = lax.broadcasted_iota(jnp.int32, comb.shape, 1)
        acc = jnp.zeros((h2.shape[0], 2 * half), F32)
        for e in range(EXPERTS_PER_GROUP):
            hg = jnp.dot(h2, wg_ref[e], preferred_element_type=F32)
            hu = jnp.dot(h2, wu_ref[e], preferred_element_type=F32)
            c = jnp.sum(jnp.where(lane == g * EXPERTS_PER_GROUP + e, comb, 0.0), axis=1, keepdims=True)
            act = (hg * _sigmoid(hg)) * hu * c
            acc = acc + jnp.dot(act.astype(BF16), wd_ref[e], preferred_element_type=F32)
        y_ref[...] = _pack_bf16_pair(acc[:, 0:half], acc[:, half:2 * half])


def _moe_grouped(rows_sorted, tile_group, tile_rows, wg_bf, wu_bf, wd_bf, tile):
    n_slots, width = rows_sorted.shape
    ne, d, ff = wg_bf.shape
    group_w = lambda shape: pl.BlockSpec((EXPERTS_PER_GROUP,) + shape, lambda i, tg, tr: (tg[i], 0, 0))
    grid_spec = pltpu.PrefetchScalarGridSpec(
        num_scalar_prefetch=2,
        grid=(n_slots // tile,),
        in_specs=[pl.BlockSpec((tile, width), lambda i, tg, tr: (i, 0)),
                  group_w((d, ff)), group_w((d, ff)), group_w((ff, d))],
        out_specs=pl.BlockSpec((tile, d // 2), lambda i, tg, tr: (i, 0)),
    )
    return pl.pallas_call(
        _moe_grouped_kernel,
        grid_spec=grid_spec,
        out_shape=jax.ShapeDtypeStruct((n_slots, d // 2), jnp.uint32),
        compiler_params=_params("arbitrary"),
        name="moe_grouped",
    )(tile_group, tile_rows, rows_sorted, wg_bf, wu_bf, wd_bf)


def _moe_finish_kernel(x1_ref, y_ref, g2_ref, lng_ref, lnb_ref, o_ref, *, alpha):
    lo, hi = _unpack_bf16_pair(y_ref[...])
    y = jnp.concatenate([lo, hi], axis=1)
    o_ref[...] = _layer_norm(alpha * x1_ref[...] + (1.0 + g2_ref[...]) * y, lng_ref[...], lnb_ref[...])


def _moe_finish(x1, y_packed, g2_spec, g2_arr, ln_g, ln_b, alpha, tm):
    t, d = x1.shape
    row = lambda width: pl.BlockSpec((tm, width), lambda i: (i, 0))
    const = lambda shape: pl.BlockSpec(shape, lambda i: (0,) * len(shape))
    return pl.pallas_call(
        functools.partial(_moe_finish_kernel, alpha=alpha),
        grid=(t // tm,),
        in_specs=[row(d), row(d // 2), g2_spec, const((1, d)), const((1, d))],
        out_specs=row(d),
        out_shape=jax.ShapeDtypeStruct((t, d), F32),
        compiler_params=_params("arbitrary"),
        name="moe_finish",
    )(x1, y_packed, g2_arr, ln_g.reshape(1, d), ln_b.reshape(1, d))


def kernel(x_prompt, x_sample, cache_k, cache_v, state_pool, page_table, c_prompt, c_sample, w_ada, b_ada, w_in, lambda_q1, lambda_k1, lambda_q2, lambda_k2, subln_g, w_pool, pool_scale, w_out, ln1_g, ln1_b, w_group, b_group, w_router, b_router, w_e_gate, w_e_up, w_e_down, ln2_g, ln2_b):
    depth = w_in.shape[0]
    bsz, seq, d = x_prompt.shape
    db, dseq, _ = x_sample.shape
    assert dseq == 1, "sample group decodes one token per sequence"
    n_pages, page = page_table.shape[1], cache_k.shape[2]
    past_len = n_pages * page
    alpha = (2.0 * depth) ** 0.25
    tp, ts = bsz * seq, db * dseq

    TM_PROJ, TQ, TM_MERGE, TM_MOE, TM_GROUP, E_BLK = 256, 1024, 512, 1024, 512, 4
    sc_info = pltpu.get_tpu_info().sparse_core
    sc_units = sc_info.num_cores * sc_info.num_subcores
    assert tp % (GATHER_WINDOW * sc_units) == 0

    xp = x_prompt.reshape(tp, d)
    xs = x_sample.reshape(ts, d)
    n_c = bsz + db
    n_c_pad = -(-n_c // 8) * 8
    c_all = jnp.concatenate([c_prompt, c_sample, jnp.zeros((n_c_pad - n_c, d), F32)], axis=0)
    tabs_p = _rope_tables(jnp.arange(seq))
    tabs_s = _rope_tables(jnp.full((ts,), past_len))

    outs = {k: [] for k in ("kp", "vp", "pp", "ks", "vs", "ps")}
    for l in range(depth):
        lam_init = 0.8 - 0.6 * math.exp(-0.3 * l)
        mod = _ada(c_all, w_ada[l], b_ada[l])
        mod_p = mod[:bsz].reshape(bsz * N_MOD, 1, d)
        mod_s = mod[bsz:n_c]
        w_in_bf = w_in[l].astype(BF16)
        w_pool_bf = w_pool[l].astype(BF16)
        w_out_bf = w_out[l].astype(BF16)
        wg_bf, wu_bf, wd_bf = (w[l].astype(BF16) for w in (w_e_gate, w_e_up, w_e_down))
        lam_p = jnp.stack([lambda_q1[l], lambda_k1[l], lambda_q2[l], lambda_k2[l]])
        n_r = N_GROUPS + N_EXPERTS
        w_r = jnp.concatenate([w_group[l], w_router[l]], axis=1).T
        w_r = jnp.concatenate([w_r, jnp.zeros((ROUTER_ROWS - n_r, d), F32)], axis=0)
        b_r = jnp.concatenate([b_group[l], b_router[l], jnp.zeros((ROUTER_ROWS - n_r,), F32)]).reshape(ROUTER_ROWS, 1)

        def pmod(k, tm, two_d=False):
            per_b = seq // tm
            if two_d:
                return pl.BlockSpec((None, 1, d), lambda i, j: ((i // per_b) * N_MOD + k, 0, 0))
            return pl.BlockSpec((None, 1, d), lambda i: ((i // per_b) * N_MOD + k, 0, 0))

        nblk = seq // TM_PROJ
        tab_t_spec = pl.BlockSpec((HEAD_DIM // 2, TM_PROJ), lambda i: (0, i % nblk))
        qt_bf, kt_f, k_bf, v_f, vt_bf, u, gates = _inproj(
            xp, pmod(1, TM_PROJ), pmod(0, TM_PROJ), mod_p, mod_p, w_in_bf,
            tabs_p, tab_t_spec, TM_PROJ, bsz, True)
        attn = _attn_prompt(qt_bf, k_bf.reshape(bsz, seq, d), vt_bf,
                            lam_p, subln_g[l], lam_init, TQ).reshape(tp, d)
        x1, ffn_rows, gid = _merge(u, gates, attn, xp,
                                   [pmod(2, TM_MERGE), pmod(4, TM_MERGE), pmod(3, TM_MERGE)], [mod_p] * 3,
                                   w_pool_bf, pool_scale[l], w_out_bf, ln1_g[l], ln1_b[l], w_r, b_r, alpha,
                                   TM_MERGE, seq_len=seq)
        slot_quantum = GATHER_WINDOW * sc_units
        n_slots = -(-(tp + N_GROUPS * (TM_GROUP - 1)) // slot_quantum) * slot_quantum
        pos, tile_group, tile_rows = _dispatch_plan(gid.reshape(tp), TM_GROUP, n_slots)
        rows_sorted = _move_rows(ffn_rows, pos, n_slots)
        outs["kp"].append(jnp.transpose(kt_f.reshape(bsz, N_HEADS, 2, HEAD_DIM, seq), (0, 4, 1, 2, 3)))
        outs["vp"].append(v_f.reshape(bsz, seq, N_HEADS, V_DIM))
        outs["pp"].append(u.reshape(bsz, seq, -1)[:, seq - POOL_BUF:])

        def smod(k, two_d=False):
            if two_d:
                return pl.BlockSpec((ts, d), lambda i, j: (0, k))
            return pl.BlockSpec((ts, d), lambda i: (0, k))

        tab_t_spec_s = pl.BlockSpec((HEAD_DIM // 2, ts), lambda i: (0, 0))
        qt_s, kt_f, s_new, v_f, u, gates = _inproj(
            xs, smod(1), smod(0), mod_s, mod_s, w_in[l],
            tabs_s, tab_t_spec_s, ts, 1, False)
        hb = ts // 2
        attn_half = lambda sl: _attn_sample(qt_s[0][:, sl], s_new[:, :, sl], v_f[sl], cache_k[l], cache_v[l],
                                            page_table[sl], lam_p, subln_g[l], lam_init)
        rows_sorted, attn_a = lax.optimization_barrier((rows_sorted, attn_half(slice(0, hb))))
        y_sorted = _moe_grouped(rows_sorted, tile_group, tile_rows, wg_bf, wu_bf, wd_bf, TM_GROUP)
        y_tok, attn_b = lax.optimization_barrier((_move_rows(y_sorted, pos), attn_half(slice(hb, ts))))
        xp = _moe_finish(x1, y_tok, pmod(5, TM_MOE), mod_p, ln2_g[l], ln2_b[l], alpha, TM_MOE)
        attn = jnp.concatenate([attn_a, attn_b], axis=0)
        pooled, new_pool = _pool_sample(state_pool[l], u, past_len)
        x1, h2, comb = _merge(pooled, gates, attn, xs, [smod(2), smod(4), smod(3)], [mod_s] * 3,
                              w_pool[l], pool_scale[l], w_out[l], ln1_g[l], ln1_b[l], w_r, b_r, alpha, ts)
        xs = _moe(h2, comb, wg_bf, wu_bf, wd_bf, x1, smod(5, True), mod_s,
                  ln2_g[l], ln2_b[l], alpha, ts, E_BLK)
        outs["ks"].append(jnp.transpose(kt_f.reshape(N_HEADS, 2, HEAD_DIM, db, dseq), (3, 4, 0, 1, 2)))
        outs["vs"].append(v_f.reshape(db, dseq, N_HEADS, V_DIM))
        outs["ps"].append(new_pool)

    return (xp.reshape(bsz, seq, d), xs.reshape(db, dseq, d),
            jnp.stack(outs["kp"]), jnp.stack(outs["vp"]), jnp.stack(outs["pp"]),
            jnp.stack(outs["ks"]), jnp.stack(outs["vs"]), jnp.stack(outs["ps"]))
```

```python
import functools
import math

import jax
import jax.numpy as jnp
from jax import lax
from jax.experimental import pallas as pl
from jax.experimental.pallas import tpu as pltpu
from jax.experimental.pallas import tpu_sc as plsc

F32 = jnp.float32
BF16 = jnp.bfloat16

LANES = 128
VMEM_LIMIT = 56 << 20

N_MOD = 6
N_HEADS = 8
HEAD_DIM = 64
V_DIM = 2 * HEAD_DIM
ROPE_THETA = 10000.0
POOL_WINDOWS = (2, 4, 8, 16)
POOL_GROUP_DIM = 128
POOL_BUF = max(POOL_WINDOWS) - 1
N_GROUPS = 4
EXPERTS_PER_GROUP = 4
N_EXPERTS = N_GROUPS * EXPERTS_PER_GROUP
TOP_K_IN_GROUP = 2
EXPERT_PAIRS = tuple((a, b) for a in range(EXPERTS_PER_GROUP) for b in range(a + 1, EXPERTS_PER_GROUP))
PAIRS_PER_GROUP = len(EXPERT_PAIRS)
N_CLASSES = N_GROUPS * PAIRS_PER_GROUP
LN_EPS = 1e-5
SUBLN_EPS = 1e-5
ROUTER_ROWS = 32
GATHER_WINDOW = 128

NT_DIMS = (((1,), (1,)), ((), ()))


def _params(*sem):
    return pltpu.CompilerParams(dimension_semantics=sem, vmem_limit_bytes=VMEM_LIMIT)


def _sigmoid(x):
    return 1.0 / (1.0 + jnp.exp(-x))


def _layer_norm(x, g, b):
    mu = jnp.mean(x, axis=-1, keepdims=True)
    xc = x - mu
    var = jnp.mean(xc * xc, axis=-1, keepdims=True)
    return xc * lax.rsqrt(var + LN_EPS) * g + b


def _lambda(lam_ref, lam_init):
    lp = lam_ref[...]
    a = jnp.sum(lp[0:1] * lp[1:2], axis=1, keepdims=True)
    b = jnp.sum(lp[2:3] * lp[3:4], axis=1, keepdims=True)
    return jnp.exp(a) - jnp.exp(b) + lam_init


def _ada_kernel(c_ref, w_ref, b_ref, o_ref):
    c = c_ref[...]
    o_ref[...] = jnp.dot(c * _sigmoid(c), w_ref[...], preferred_element_type=F32,
                         precision=lax.Precision.HIGHEST) + b_ref[...]


def _ada(c_all, w_ada, b_ada):
    n, d = c_all.shape
    cols = w_ada.shape[1]
    return pl.pallas_call(
        _ada_kernel,
        grid=(cols // d,),
        in_specs=[pl.BlockSpec((n, d), lambda j: (0, 0)),
                  pl.BlockSpec((d, d), lambda j: (0, j)),
                  pl.BlockSpec((1, d), lambda j: (0, j))],
        out_specs=pl.BlockSpec((n, d), lambda j: (0, j)),
        out_shape=jax.ShapeDtypeStruct((n, cols), F32),
        compiler_params=_params("arbitrary"),
        name="ada_mod",
    )(c_all, w_ada, b_ada.reshape(1, cols))


def _inproj_kernel(x_ref, sc_ref, sh_ref, w_ref, cost_ref, sint_ref, *out_refs, d, pw, prompt):
    mm_dtype = w_ref.dtype
    prec = lax.Precision.HIGHEST if mm_dtype == F32 else None
    dot = functools.partial(jnp.dot, preferred_element_type=F32, precision=prec)
    h =(x_ref[...] * (1.0 + sc_ref[...]) + sh_ref[...]).astype(mm_dtype)
    half = HEAD_DIM // 2
    scale = HEAD_DIM ** -0.5 * (math.log2(math.e) if prompt else 1.0)
    cost, sint = cost_ref[...], sint_ref[...]

    def rope_rows(xt):
        out = []
        for g in range(d // HEAD_DIM):
            lo = slice(g * HEAD_DIM, g * HEAD_DIM + half)
            hi = slice(g * HEAD_DIM + half, (g + 1) * HEAD_DIM)
            x1, x2 = xt[lo], xt[hi]
            out.append((lo, x1 * cost - x2 * sint))
            out.append((hi, x2 * cost + x1 * sint))
        return out

    qt = rope_rows(dot(h, w_ref[:, 0:d]).T)
    kt = rope_rows(dot(h, w_ref[:, d:2 * d]).T)
    v = dot(h, w_ref[:, 2 * d:3 * d])
    if prompt:
        qt_ref, ktf_ref, kb_ref, vf_ref, vtb_ref, u_ref, g_ref = out_refs
        for sl, r in qt:
            qt_ref[sl, :] = (r * scale).astype(BF16)
        for sl, r in kt:
            ktf_ref[sl, :] = r
        kb_ref[...] = ktf_ref[...].T.astype(BF16)
        vtb_ref[...] = v.T.astype(BF16)
        g_dtype = BF16
    else:
        g_dtype = F32
        qt_ref, ktf_ref, sn_ref, vf_ref, u_ref, g_ref = out_refs
        for idx, ((sl, rq), (_, rk)) in enumerate(zip(qt, kt)):
            qs = rq * scale
            qt_ref[sl, :] = qs
            ktf_ref[sl, :] = rk
            part = jnp.sum(qs * rk, axis=0, keepdims=True)
            g, is_hi = idx // 2, idx % 2
            head, which = g // 2, g % 2
            if is_hi:
                sn_ref[which, head:head + 1, :] += part
            else:
                sn_ref[which, head:head + 1, :] = part
    vf_ref[...] = v
    u_ref[...] = dot(h, w_ref[:, 3 * d:3 * d + pw])
    g_ref[...] = _sigmoid(dot(h, w_ref[:, 3 * d + pw:])).astype(g_dtype)


def _inproj(x, sc_spec, sh_spec, sc_arr, sh_arr, w_in, tabs_t, tab_t_spec, tm, n_seq, prompt):
    t, d = x.shape
    cols = w_in.shape[1]
    pw = cols - 5 * d
    per_seq = t // n_seq
    nblk = per_seq // tm
    row = lambda width: pl.BlockSpec((tm, width), lambda i: (i, 0))
    ft_spec = pl.BlockSpec((None, d, tm), lambda i: (i // nblk, 0, i % nblk))
    ft_shape = lambda dt: jax.ShapeDtypeStruct((n_seq, d, per_seq), dt)
    tok_shape = lambda width, dt: jax.ShapeDtypeStruct((t, width), dt)
    if prompt:
        out_specs = [ft_spec, ft_spec, row(d), row(d), ft_spec, row(pw), row(2 * d)]
        out_shape = [ft_shape(BF16), ft_shape(F32), tok_shape(d, BF16), tok_shape(d, F32), ft_shape(BF16),
                     tok_shape(pw, F32), tok_shape(2 * d, BF16)]
    else:
        assert n_seq == 1 and nblk == 1
        out_specs = [ft_spec, ft_spec, pl.BlockSpec((2, N_HEADS, tm), lambda i: (0, 0, 0)), row(d),
                     row(pw), row(2 * d)]
        out_shape = [ft_shape(F32), ft_shape(F32), jax.ShapeDtypeStruct((2, N_HEADS, t), F32),
                     tok_shape(d, F32), tok_shape(pw, F32), tok_shape(2 * d, F32)]
    return pl.pallas_call(
        functools.partial(_inproj_kernel, d=d, pw=pw, prompt=prompt),
        grid=(t // tm,),
        in_specs=[row(d), sc_spec, sh_spec,
                  pl.BlockSpec((d, cols), lambda i: (0, 0), pipeline_mode=pl.Buffered(1)),
                  tab_t_spec, tab_t_spec],
        out_specs=out_specs,
        out_shape=out_shape,
        compiler_params=_params("arbitrary"),
        name="in_proj",
    )(x, sc_arr, sh_arr, w_in, *tabs_t)


def _rope_tables(pos):
    half = HEAD_DIM // 2
    inv = ROPE_THETA ** (-jnp.arange(half, dtype=F32) / half)
    ang = inv[:, None] * pos.astype(F32)[None, :]
    return jnp.cos(ang), jnp.sin(ang)


def _attn_kernel(lam_ref, qt_ref, k_ref, vt_ref, g_ref, o_ref, q2_sc, sta_sc, stb_sc, m_sc, l_sc, acc_sc,
                 *, tq, lam_init):
    qi = pl.program_id(2)
    zero = jnp.zeros((HEAD_DIM, tq), BF16)
    q2_sc[0:HEAD_DIM, 0:tq] = qt_ref[0:HEAD_DIM, :]
    q2_sc[HEAD_DIM:V_DIM, 0:tq] = zero
    q2_sc[0:HEAD_DIM, tq:2 * tq] = zero
    q2_sc[HEAD_DIM:V_DIM, tq:2 * tq] = qt_ref[HEAD_DIM:V_DIM, :]
    m_sc[...] = jnp.full(m_sc.shape, -jnp.inf, F32)
    l_sc[...] = jnp.zeros(l_sc.shape, F32)
    acc_sc[...] = jnp.zeros(acc_sc.shape, F32)

    def scores(ki, st_ref):
        start = pl.multiple_of(ki * tq, tq)
        st_ref[...] = jnp.dot(k_ref[pl.ds(start, tq), :], q2_sc[...], preferred_element_type=F32)

    def consume(ki, st_ref, masked):
        st = st_ref[...]
        if masked:
            key = lax.broadcasted_iota(jnp.int32, st.shape, 0)
            qry = lax.broadcasted_iota(jnp.int32, st.shape, 1)
            qry = jnp.where(qry >= tq, qry - tq, qry)
            st = jnp.where(key <= qry, st, -jnp.inf)
        m_old = m_sc[...]
        m_new = jnp.maximum(m_old, jnp.max(st, axis=0, keepdims=True))
        alpha = jnp.exp2(m_old - m_new)
        p = jnp.exp2(st - m_new).astype(BF16)
        start = pl.multiple_of(ki * tq, tq)
        vt = vt_ref[:, pl.ds(start, tq)]
        vt1 = jnp.concatenate([vt, jnp.ones((16, tq), BF16)], axis=0)
        r = jnp.dot(vt1, p, preferred_element_type=F32)
        l_sc[...] = alpha * l_sc[...] + r[V_DIM:V_DIM + 1]
        acc_sc[...] = alpha * acc_sc[...] + r[0:V_DIM]
        m_sc[...] = m_new

    scores(0, sta_sc)

    def pair(pi, carry):
        k0 = 2 * pi
        scores(k0 + 1, stb_sc)
        consume(k0, sta_sc, False)
        scores(k0 + 2, sta_sc)
        consume(k0 + 1, stb_sc, False)
        return carry

    lax.fori_loop(0, qi // 2, pair, 0)

    @pl.when(qi % 2 == 0)
    def _():
        consume(qi, sta_sc, True)

    @pl.when(qi % 2 == 1)
    def _():
        scores(qi, stb_sc)
        consume(qi - 1, sta_sc, False)
        consume(qi, stb_sc, True)

    lam = _lambda(lam_ref, lam_init)
    inv_l = 1.0 / l_sc[...]
    ot = acc_sc[:, 0:tq] * inv_l[:, 0:tq] - lam * (acc_sc[:, tq:2 * tq] * inv_l[:, tq:2 * tq])
    ms = jnp.mean(ot * ot, axis=0, keepdims=True)
    ot = ot * lax.rsqrt(ms + SUBLN_EPS) * g_ref[...] * (1.0 - lam_init)
    o_ref[...] = ot.T.astype(o_ref.dtype)


def _attn_prompt(qt_bf, k_bf, vt_bf, lam_p, subln_g, lam_init, tq):
    b, d, s = qt_bf.shape
    nh = d // V_DIM
    return pl.pallas_call(
        functools.partial(_attn_kernel, tq=tq, lam_init=lam_init),
        grid=(b, nh, s // tq),
        in_specs=[pl.BlockSpec(lam_p.shape, lambda bi, h, qi: (0, 0)),
                  pl.BlockSpec((None, V_DIM, tq), lambda bi, h, qi: (bi, h, qi)),
                  pl.BlockSpec((None, s, V_DIM), lambda bi, h, qi: (bi, 0, h)),
                  pl.BlockSpec((None, V_DIM, s), lambda bi, h, qi: (bi, h, 0)),
                  pl.BlockSpec((V_DIM, 1), lambda bi, h, qi: (0, 0))],
        out_specs=pl.BlockSpec((None, tq, V_DIM), lambda bi, h, qi: (bi, qi, h)),
        out_shape=jax.ShapeDtypeStruct((b, s, d), BF16),
        scratch_shapes=[pltpu.VMEM((V_DIM, 2 * tq), BF16),
                        pltpu.VMEM((tq, 2 * tq), F32),
                        pltpu.VMEM((tq, 2 * tq), F32),
                        pltpu.VMEM((1, 2 * tq), F32),
                        pltpu.VMEM((1, 2 * tq), F32),
                        pltpu.VMEM((V_DIM, 2 * tq), F32)],
        compiler_params=_params("arbitrary", "arbitrary", "arbitrary"),
        name="attn_prompt",
    )(lam_p, qt_bf, k_bf, vt_bf, subln_g.reshape(V_DIM, 1))


def _attn_sample_kernel(pt_ref, lam_ref, qt_ref, sn_ref, vn_ref, g_ref, *refs, n_pages, lam_init):
    del pt_ref
    kt_refs, v_refs = refs[:n_pages], refs[n_pages:2 * n_pages]
    o_ref, s_sc = refs[2 * n_pages:]
    b = pl.program_id(0)
    d, n_tok = qt_ref.shape
    page = kt_refs[0].shape[1]

    onehot = jnp.where(lax.broadcasted_iota(jnp.int32, (n_tok, page), 0) == b, 1.0, 0.0)
    qb = jnp.dot(qt_ref[...], onehot, preferred_element_type=F32,
                 precision=lax.Precision.HIGHEST)
    for p in range(n_pages):
        prod = (kt_refs[p][...] * qb).reshape(N_HEADS, V_DIM, page)
        s_sc[0, :, p * page:(p + 1) * page] = jnp.sum(prod[:, 0:HEAD_DIM, :], axis=1)
        s_sc[1, :, p * page:(p + 1) * page] = jnp.sum(prod[:, HEAD_DIM:V_DIM, :], axis=1)

    own = lax.broadcasted_iota(jnp.int32, (N_HEADS, n_tok), 1) == b

    def softmax_parts(which):
        s_new = jnp.max(jnp.where(own, sn_ref[which], -jnp.inf), axis=1, keepdims=True)
        s = s_sc[which]
        m = jnp.maximum(jnp.max(s, axis=1, keepdims=True), s_new)
        p = jnp.exp(s - m)
        p_new = jnp.exp(s_new - m)
        inv_l = 1.0 / (jnp.sum(p, axis=1, keepdims=True) + p_new)
        return p * inv_l, p_new * inv_l

    lam = _lambda(lam_ref, lam_init)
    w1, wn1 = softmax_parts(0)
    w2, wn2 = softmax_parts(1)
    w = w1 - lam * w2
    acc = (wn1 - lam * wn2) * vn_ref[...]
    for p in range(n_pages):
        for t in range(page):
            c = p * page + t
            acc = acc + w[:, c:c + 1] * v_refs[p][t]
    ms = jnp.mean(acc * acc, axis=1, keepdims=True)
    o_ref[...] = acc * lax.rsqrt(ms + SUBLN_EPS) * g_ref[...] * (1.0 - lam_init)


def _attn_sample(qt, s_new, v_new, cache_k, cache_v, page_table, lam_p, subln_g, lam_init):
    d, db = qt.shape
    n_pages = page_table.shape[1]
    n_phys, page = cache_k.shape[0], cache_k.shape[1]
    ck = jnp.transpose(cache_k, (0, 2, 3, 4, 1)).reshape(n_phys, d, page)
    pt = page_table.reshape(-1)

    def page_spec(p, shape):
        zeros = (0,) * len(shape)
        return pl.BlockSpec((None,) + shape, lambda b, pt_ref: (pt_ref[b * n_pages + p],) + zeros)

    const = lambda shape: pl.BlockSpec(shape, lambda b, pt_ref: (0,) * len(shape))
    head_spec = pl.BlockSpec((None, N_HEADS, V_DIM), lambda b, pt_ref: (b, 0, 0))
    grid_spec = pltpu.PrefetchScalarGridSpec(
        num_scalar_prefetch=1,
        grid=(db,),
        in_specs=[const(lam_p.shape), const((d, db)), const((2, N_HEADS, db)), head_spec, const((1, V_DIM))]
                 + [page_spec(p, (d, page)) for p in range(n_pages)]
                 + [page_spec(p, (page, N_HEADS, V_DIM)) for p in range(n_pages)],
        out_specs=head_spec,
        scratch_shapes=[pltpu.VMEM((2, N_HEADS, n_pages * page), F32)],
    )
    out = pl.pallas_call(
        functools.partial(_attn_sample_kernel, n_pages=n_pages, lam_init=lam_init),
        grid_spec=grid_spec,
        out_shape=jax.ShapeDtypeStruct((db, N_HEADS, V_DIM), F32),
        compiler_params=_params("arbitrary"),
        name="attn_sample",
    )(pt, lam_p, qt, s_new, v_new.reshape(db, N_HEADS, V_DIM), subln_g.reshape(1, V_DIM),
      *([ck] * n_pages), *([cache_v] * n_pages))
    return out.reshape(db, d)


POOL_HALO = 16


def _pool_tile(u_ref, halo_ref, ext_sc, seq_tile):
    tm = u_ref.shape[0]
    hb = halo_ref.shape[0]

    @pl.when(seq_tile == 0)
    def _():
        ext_sc[0:hb, :] = jnp.zeros((hb, ext_sc.shape[1]), F32)

    @pl.when(seq_tile > 0)
    def _():
        ext_sc[0:hb, :] = halo_ref[...]

    ext_sc[hb:hb + tm, :] = u_ref[...]
    pos = seq_tile * tm + lax.broadcasted_iota(jnp.int32, (tm, POOL_GROUP_DIM), 0)
    parts = []
    for g, win in enumerate(POOL_WINDOWS):
        sl = slice(g * POOL_GROUP_DIM, (g + 1) * POOL_GROUP_DIM)
        cur = ext_sc[hb:hb + tm, sl]
        tot = cur
        for j in range(1, win):
            tot = tot + ext_sc[hb - j:hb - j + tm, sl]
        cnt = jnp.minimum(win, pos + 1).astype(F32)
        parts.append(tot / cnt - cur)
    return jnp.concatenate(parts, axis=1)


def _pool_sample_kernel(st_ref, u_ref, o_ref, np_ref, *, start_pos):
    nb = st_ref.shape[0]
    u = u_ref[...]
    for g, win in enumerate(POOL_WINDOWS):
        sl = slice(g * POOL_GROUP_DIM, (g + 1) * POOL_GROUP_DIM)
        cur = u[:, sl]
        tot = cur
        for j in range(1, win):
            tot = tot + st_ref[nb - j, :, sl]
        o_ref[:, sl] = tot / float(min(win, start_pos + 1)) - cur
    for j in range(nb - 1):
        np_ref[j] = st_ref[j + 1]
    np_ref[nb - 1] = u


def _pool_sample(state, u, start_pos):
    db, nb, pw = state.shape
    full = lambda shape: pl.BlockSpec(shape, lambda i: (0,) * len(shape))
    pooled, new_pool = pl.pallas_call(
        functools.partial(_pool_sample_kernel, start_pos=start_pos),
        grid=(1,),
        in_specs=[full((nb, db, pw)), full((db, pw))],
        out_specs=[full((db, pw)), full((nb, db, pw))],
        out_shape=[jax.ShapeDtypeStruct((db, pw), F32), jax.ShapeDtypeStruct((nb, db, pw), F32)],
        compiler_params=_params("arbitrary"),
        name="pool_sample",
    )(jnp.transpose(state, (1, 0, 2)), u)
    return pooled, jnp.transpose(new_pool, (1, 0, 2))


def _route(lt):
    g = [lt[j:j + 1] for j in range(N_GROUPS)]
    gmax = functools.reduce(jnp.maximum, g)
    gidx = jnp.full(gmax.shape, N_GROUPS - 1, jnp.int32)
    for j in range(N_GROUPS - 2, -1, -1):
        gidx = jnp.where(g[j] == gmax, j, gidx)
    p_g = 1.0 / functools.reduce(jnp.add, [jnp.exp(gj - gmax) for gj in g])
    e = []
    for k in range(EXPERTS_PER_GROUP):
        sel = lt[N_GROUPS + k:N_GROUPS + k + 1]
        for j in range(1, N_GROUPS):
            r = N_GROUPS + j * EXPERTS_PER_GROUP + k
            sel = jnp.where(gidx == j, lt[r:r + 1], sel)
        e.append(sel)

    def first_argmax(vals):
        vmax = functools.reduce(jnp.maximum, vals)
        idx = jnp.full(vmax.shape, len(vals) - 1, jnp.int32)
        for k in range(len(vals) - 2, -1, -1):
            idx = jnp.where(vals[k] == vmax, k, idx)
        return vmax, idx

    v1, i1 = first_argmax(e)
    rest = [jnp.where(i1 == k, -jnp.inf, e[k]) for k in range(EXPERTS_PER_GROUP)]
    v2, i2 = first_argmax(rest)
    t = jnp.exp(v2 - v1)
    w1 = p_g / (1.0 + t)
    w2 = p_g * t / (1.0 + t)
    within = [jnp.where(i1 == k, w1, 0.0) + jnp.where(i2 == k, w2, 0.0) for k in range(EXPERTS_PER_GROUP)]
    rows = [jnp.where(gidx == j, within[k], 0.0) for j in range(N_GROUPS) for k in range(EXPERTS_PER_GROUP)]
    a, b = jnp.minimum(i1, i2), jnp.maximum(i1, i2)
    pair = jnp.where(a == 0, b - 1, jnp.where(a == 1, b + 1, 5))
    return jnp.concatenate(rows, axis=0), gidx * PAIRS_PER_GROUP + pair


def _pack_bf16_pair(a, b):
    a_bits = lax.bitcast_convert_type(a.astype(BF16).astype(F32), jnp.uint32)
    b_bits = lax.bitcast_convert_type(b.astype(BF16).astype(F32), jnp.uint32)
    return (a_bits >> 16) | (b_bits & jnp.uint32(0xFFFF0000))


def _unpack_bf16_pair(w):
    lo = lax.bitcast_convert_type(w << 16, F32)
    hi = lax.bitcast_convert_type(w & jnp.uint32(0xFFFF0000), F32)
    return lo, hi


def _merge_kernel(*refs, alpha, tiles_per_seq):
    if tiles_per_seq:
        u_ref, halo_ref, *refs, ext_sc = refs
        pooled = _pool_tile(u_ref, halo_ref, ext_sc, pl.program_id(0) % tiles_per_seq)
    else:
        pooled_ref, *refs = refs
        pooled = pooled_ref[...]
    (gates_ref, attn_ref, x_ref, g1_ref, sc2_ref, sh2_ref, wpool_ref, pscale_ref, wout_ref, lng_ref, lnb_ref,
     wr_ref, br_ref, x1_ref, *route_refs) = refs
    d = x_ref.shape[1]
    tm = x_ref.shape[0]
    mm_dtype = wout_ref.dtype
    dot = functools.partial(jnp.dot, preferred_element_type=F32,
                            precision=lax.Precision.HIGHEST if mm_dtype == F32 else None)
    pooled = pooled.astype(mm_dtype)
    parts = [dot(pooled[:, g * POOL_GROUP_DIM:(g + 1) * POOL_GROUP_DIM], wpool_ref[g])
             for g in range(len(POOL_WINDOWS))]
    pool_out = jnp.concatenate(parts, axis=1) * pscale_ref[...]
    g_dtype = gates_ref.dtype
    merged = (gates_ref[:, 0:d] * pool_out.astype(g_dtype)
              + gates_ref[:, d:2 * d] * attn_ref[...].astype(g_dtype))
    y = dot(merged.astype(mm_dtype), wout_ref[...])
    x1 = _layer_norm(alpha * x_ref[...] + (1.0 + g1_ref[...]) * y, lng_ref[...], lnb_ref[...])
    x1_ref[...] = x1
    h2 = x1 * (1.0 + sc2_ref[...]) + sh2_ref[...]
    lt = lax.dot_general(wr_ref[...], h2, NT_DIMS, preferred_element_type=F32,
                         precision=lax.Precision.HIGHEST) + br_ref[...]
    comb_t, cls = _route(lt)
    comb = jnp.concatenate([comb_t, jnp.zeros((LANES - N_EXPERTS, tm), F32)], axis=0).T
    if tiles_per_seq:
        row_ref, cls_ref = route_refs
        row_ref[:, 0:d // 2] = _pack_bf16_pair(h2[:, 0:d // 2], h2[:, d // 2:d])
        row_ref[:, d // 2:d // 2 + LANES] = lax.bitcast_convert_type(comb, jnp.uint32)
        cls_ref[...] = cls
    else:
        h2_ref, comb_ref = route_refs
        h2_ref[...] = h2.astype(BF16)
        comb_ref[...] = comb


def _merge(pool_in, gates, attn, x, mod_specs, mod_arrs, w_pool, pool_scale, w_out,
           ln_g, ln_b, w_r, b_r, alpha, tm, seq_len=None):
    t, d = x.shape
    pw = pool_in.shape[1]
    row = lambda width: pl.BlockSpec((tm, width), lambda i: (i, 0))
    const = lambda shape: pl.BlockSpec(shape, lambda i: (0,) * len(shape))
    if seq_len:
        halo_spec = pl.BlockSpec((POOL_HALO, pw), lambda i: (jnp.maximum(i * (tm // POOL_HALO) - 1, 0), 0))
        pool_specs, pool_args = [row(pw), halo_spec], [pool_in, pool_in]
        scratch = [pltpu.VMEM((POOL_HALO + tm, pw), F32)]
    else:
        pool_specs, pool_args, scratch = [row(pw)], [pool_in], []
    if seq_len:
        route_specs = [row(d // 2 + LANES), pl.BlockSpec((None, 1, tm), lambda i: (i, 0, 0))]
        route_shapes = [jax.ShapeDtypeStruct((t, d // 2 + LANES), jnp.uint32),
                        jax.ShapeDtypeStruct((t // tm, 1, tm), jnp.int32)]
    else:
        route_specs = [row(d), row(LANES)]
        route_shapes = [jax.ShapeDtypeStruct((t, d), BF16), jax.ShapeDtypeStruct((t, LANES), F32)]
    return pl.pallas_call(
        functools.partial(_merge_kernel, alpha=alpha, tiles_per_seq=seq_len // tm if seq_len else None),
        grid=(t // tm,),
        in_specs=[*pool_specs, row(2 * d), row(d), row(d), *mod_specs,
                  const(w_pool.shape), const((1, d)), const((d, d)), const((1, d)), const((1, d)),
                  const(w_r.shape), const(b_r.shape)],
        out_specs=[row(d), *route_specs],
        out_shape=[jax.ShapeDtypeStruct((t, d), F32), *route_shapes],
        scratch_shapes=scratch,
        compiler_params=_params("arbitrary"),
        name="merge",
    )(*pool_args, gates, attn, x, *mod_arrs, w_pool, pool_scale.reshape(1, d), w_out,
      ln_g.reshape(1, d), ln_b.reshape(1, d), w_r, b_r)


def _moe_kernel(h2_ref, comb_ref, wg_ref, wu_ref, wd_ref, x1_ref, g2_ref, lng_ref, lnb_ref,
                o_ref, acc_sc, *, alpha):
    j = pl.program_id(1)
    eb = wg_ref.shape[0]

    @pl.when(j == 0)
    def _():
        acc_sc[...] = jnp.zeros(acc_sc.shape, F32)

    h2 = h2_ref[...]
    comb = comb_ref[...]
    lane = lax.broadcasted_iota(jnp.int32, comb.shape, 1)
    for e in range(eb):
        hg = jnp.dot(h2, wg_ref[e], preferred_element_type=F32)
        hu = jnp.dot(h2, wu_ref[e], preferred_element_type=F32)
        c = jnp.sum(jnp.where(lane == j * eb + e, comb, 0.0), axis=1, keepdims=True)
        act = (hg * _sigmoid(hg)) * hu * c
        acc_sc[...] += jnp.dot(act.astype(BF16), wd_ref[e], preferred_element_type=F32)

    @pl.when(j == pl.num_programs(1) - 1)
    def _():
        y = alpha * x1_ref[...] + (1.0 + g2_ref[...]) * acc_sc[...]
        o_ref[...] = _layer_norm(y, lng_ref[...], lnb_ref[...])


def _moe(h2, comb, wg_bf, wu_bf, wd_bf, x1, g2_spec, g2_arr, ln_g, ln_b, alpha, tm, eb):
    t, d = x1.shape
    ne, _, ff = wg_bf.shape
    row = lambda width: pl.BlockSpec((tm, width), lambda i, j: (i, 0))
    const = lambda shape: pl.BlockSpec(shape, lambda i, j: (0,) * len(shape))
    return pl.pallas_call(
        functools.partial(_moe_kernel, alpha=alpha),
        grid=(t // tm, ne // eb),
        in_specs=[row(d), row(LANES),
                  pl.BlockSpec((eb, d, ff), lambda i, j: (j, 0, 0)),
                  pl.BlockSpec((eb, d, ff), lambda i, j: (j, 0, 0)),
                  pl.BlockSpec((eb, ff, d), lambda i, j: (j, 0, 0)),
                  row(d), g2_spec, const((1, d)), const((1, d))],
        out_specs=row(d),
        out_shape=jax.ShapeDtypeStruct((t, d), F32),
        scratch_shapes=[pltpu.VMEM((tm, d), F32)],
        compiler_params=_params("arbitrary", "arbitrary"),
        name="moe_ffn",
    )(h2, comb, wg_bf, wu_bf, wd_bf, x1, g2_arr, ln_g.reshape(1, d), ln_b.reshape(1, d))


def _move_rows(x, idx, n_out=None):
    n, width = idx.shape[0], x.shape[1]
    scatter = n_out is not None
    info = pltpu.get_tpu_info().sparse_core
    units = info.num_cores * info.num_subcores
    per_unit = n // (units * GATHER_WINDOW)
    assert per_unit * units * GATHER_WINDOW == n

    @pl.kernel(out_type=jax.ShapeDtypeStruct((n_out if scatter else n, width), x.dtype),
               mesh=plsc.VectorSubcoreMesh(core_axis_name="core", subcore_axis_name="subcore"),
               scratch_types=[pltpu.VMEM((1, GATHER_WINDOW), jnp.int32), pltpu.VMEM((GATHER_WINDOW, width), x.dtype)])
    def move_kernel(x_hbm, i_hbm, o_hbm, i_vmem, buf):
        unit = lax.axis_index("core") * info.num_subcores + lax.axis_index("subcore")

        @pl.loop(0, per_unit)
        def _(j):
            base = (unit * per_unit + j) * GATHER_WINDOW
            pltpu.sync_copy(i_hbm.at[:, pl.ds(base, GATHER_WINDOW)], i_vmem)
            if scatter:
                pltpu.sync_copy(x_hbm.at[pl.ds(base, GATHER_WINDOW), :], buf)
                pltpu.sync_copy(buf, o_hbm.at[i_vmem.at[0]])
            else:
                pltpu.sync_copy(x_hbm.at[i_vmem.at[0]], buf)
                pltpu.sync_copy(buf, o_hbm.at[pl.ds(base, GATHER_WINDOW), :])

    return move_kernel(x, idx.reshape(1, n))


def _dispatch_plan(cls, tile, n_slots):
    onehot = (cls[:, None] == jnp.arange(N_CLASSES, dtype=jnp.int32)[None, :]).astype(jnp.int32)
    rank = jnp.cumsum(onehot, axis=0) - onehot
    counts = jnp.sum(onehot, axis=0)
    padded = -(-counts // tile) * tile
    ends = jnp.cumsum(padded)
    starts = ends - padded
    pos = jnp.sum(onehot * (starts[None, :] + rank), axis=1)
    tile_start = jnp.arange(n_slots // tile, dtype=jnp.int32) * tile
    tile_class = jnp.minimum(jnp.sum((tile_start[:, None] >= ends[None, :]).astype(jnp.int32), axis=1),
                             N_CLASSES - 1)
    tile_rows = jnp.clip((starts + counts)[tile_class] - tile_start, 0, tile)
    return pos, tile_class, tile_rows


def _moe_grouped_kernel(te_ref, tr_ref, row_ref, *refs):
    *w_refs, y_ref = refs
    n_rows = tr_ref[pl.program_id(0)]
    half = y_ref.shape[1]

    @pl.when(n_rows == 0)
    def _():
        y_ref[...] = jnp.zeros(y_ref.shape, y_ref.dtype)

    @pl.when(n_rows > 0)
    def _():
        live = lax.broadcasted_iota(jnp.int32, (row_ref.shape[0], 1), 0) < n_rows
        lo, hi = _unpack_bf16_pair(row_ref[:, 0:half])
        h2 = jnp.where(live, jnp.concatenate([lo, hi], axis=1), 0.0).astype(BF16)
        comb = jnp.where(live, lax.bitcast_convert_type(row_ref[:, half:half + LANES], F32), 0.0)
        lane = lax.broadcasted_iota(jnp.int32, comb.shape, 1)
        acc = jnp.zeros((h2.shape[0], 2 * half), F32)
        for k in range(TOP_K_IN_GROUP):
            wg_ref, wu_ref, wd_ref = w_refs[3 * k:3 * k + 3]
            expert = te_ref[k, pl.program_id(0)]
            hg = jnp.dot(h2, wg_ref[...], preferred_element_type=F32)
            hu = jnp.dot(h2, wu_ref[...], preferred_element_type=F32)
            c = jnp.sum(jnp.where(lane == expert, comb, 0.0), axis=1, keepdims=True)
            act = (hg * _sigmoid(hg)) * hu * c
            acc = acc + jnp.dot(act.astype(BF16), wd_ref[...], preferred_element_type=F32)
        y_ref[...] = _pack_bf16_pair(acc[:, 0:half], acc[:, half:2 * half])


def _moe_grouped(rows_sorted, tile_experts, tile_rows, wg_bf, wu_bf, wd_bf, tile):
    n_slots, width = rows_sorted.shape
    ne, d, ff = wg_bf.shape
    expert_w = lambda k, shape: pl.BlockSpec((None,) + shape, lambda i, te, tr: (te[k, i], 0, 0))
    w_specs, w_args = [], []
    for k in range(TOP_K_IN_GROUP):
        w_specs += [expert_w(k, (d, ff)), expert_w(k, (d, ff)), expert_w(k, (ff, d))]
        w_args += [wg_bf, wu_bf, wd_bf]
    grid_spec = pltpu.PrefetchScalarGridSpec(
        num_scalar_prefetch=2,
        grid=(n_slots // tile,),
        in_specs=[pl.BlockSpec((tile, width), lambda i, te, tr: (i, 0)), *w_specs],
        out_specs=pl.BlockSpec((tile, d // 2), lambda i, te, tr: (i, 0)),
    )
    return pl.pallas_call(
        _moe_grouped_kernel,
        grid_spec=grid_spec,
        out_shape=jax.ShapeDtypeStruct((n_slots, d // 2), jnp.uint32),
        compiler_params=_params("arbitrary"),
        name="moe_grouped",
    )(tile_experts, tile_rows, rows_sorted, *w_args)


def _moe_finish_kernel(x1_ref, y_ref, g2_ref, lng_ref, lnb_ref, o_ref, *, alpha):
    lo, hi = _unpack_bf16_pair(y_ref[...])
    y = jnp.concatenate([lo, hi], axis=1)
    o_ref[...] = _layer_norm(alpha * x1_ref[...] + (1.0 + g2_ref[...]) * y, lng_ref[...], lnb_ref[...])


def _moe_finish(x1, y_packed, g2_spec, g2_arr, ln_g, ln_b, alpha, tm):
    t, d = x1.shape
    row = lambda width: pl.BlockSpec((tm, width), lambda i: (i, 0))
    const = lambda shape: pl.BlockSpec(shape, lambda i: (0,) * len(shape))
    return pl.pallas_call(
        functools.partial(_moe_finish_kernel, alpha=alpha),
        grid=(t // tm,),
        in_specs=[row(d), row(d // 2), g2_spec, const((1, d)), const((1, d))],
        out_specs=row(d),
        out_shape=jax.ShapeDtypeStruct((t, d), F32),
        compiler_params=_params("arbitrary"),
        name="moe_finish",
    )(x1, y_packed, g2_arr, ln_g.reshape(1, d), ln_b.reshape(1, d))


def kernel(x_prompt, x_sample, cache_k, cache_v, state_pool, page_table, c_prompt, c_sample, w_ada, b_ada, w_in, lambda_q1, lambda_k1, lambda_q2, lambda_k2, subln_g, w_pool, pool_scale, w_out, ln1_g, ln1_b, w_group, b_group, w_router, b_router, w_e_gate, w_e_up, w_e_down, ln2_g, ln2_b):
    depth = w_in.shape[0]
    bsz, seq, d = x_prompt.shape
    db, dseq, _ = x_sample.shape
    assert dseq == 1, "sample group decodes one token per sequence"
    n_pages, page = page_table.shape[1], cache_k.shape[2]
    past_len = n_pages * page
    alpha = (2.0 * depth) ** 0.25
    tp, ts = bsz * seq, db * dseq

    TM_PROJ, TQ, TM_MERGE, TM_MOE, TM_GROUP, E_BLK = 256, 1024, 512, 1024, 512, 4
    sc_info = pltpu.get_tpu_info().sparse_core
    sc_units = sc_info.num_cores * sc_info.num_subcores
    assert tp % (GATHER_WINDOW * sc_units) == 0

    xp = x_prompt.reshape(tp, d)
    xs = x_sample.reshape(ts, d)
    n_c = bsz + db
    n_c_pad = -(-n_c // 8) * 8
    c_all = jnp.concatenate([c_prompt, c_sample, jnp.zeros((n_c_pad - n_c, d), F32)], axis=0)
    tabs_p = _rope_tables(jnp.arange(seq))
    tabs_s = _rope_tables(jnp.full((ts,), past_len))

    outs = {k: [] for k in ("kp", "vp", "pp", "ks", "vs", "ps")}
    for l in range(depth):
        lam_init = 0.8 - 0.6 * math.exp(-0.3 * l)
        mod = _ada(c_all, w_ada[l], b_ada[l])
        mod_p = mod[:bsz].reshape(bsz * N_MOD, 1, d)
        mod_s = mod[bsz:n_c]
        w_in_bf = w_in[l].astype(BF16)
        w_pool_bf = w_pool[l].astype(BF16)
        w_out_bf = w_out[l].astype(BF16)
        wg_bf, wu_bf, wd_bf = (w[l].astype(BF16) for w in (w_e_gate, w_e_up, w_e_down))
        lam_p = jnp.stack([lambda_q1[l], lambda_k1[l], lambda_q2[l], lambda_k2[l]])
        n_r = N_GROUPS + N_EXPERTS
        w_r = jnp.concatenate([w_group[l], w_router[l]], axis=1).T
        w_r = jnp.concatenate([w_r, jnp.zeros((ROUTER_ROWS - n_r, d), F32)], axis=0)
        b_r = jnp.concatenate([b_group[l], b_router[l], jnp.zeros((ROUTER_ROWS - n_r,), F32)]).reshape(ROUTER_ROWS, 1)

        def pmod(k, tm, two_d=False):
            per_b = seq // tm
            if two_d:
                return pl.BlockSpec((None, 1, d), lambda i, j: ((i // per_b) * N_MOD + k, 0, 0))
            return pl.BlockSpec((None, 1, d), lambda i: ((i // per_b) * N_MOD + k, 0, 0))

        nblk = seq // TM_PROJ
        tab_t_spec = pl.BlockSpec((HEAD_DIM // 2, TM_PROJ), lambda i: (0, i % nblk))
        qt_bf, kt_f, k_bf, v_f, vt_bf, u, gates = _inproj(
            xp, pmod(1, TM_PROJ), pmod(0, TM_PROJ), mod_p, mod_p, w_in_bf,
            tabs_p, tab_t_spec, TM_PROJ, bsz, True)
        attn = _attn_prompt(qt_bf, k_bf.reshape(bsz, seq, d), vt_bf,
                            lam_p, subln_g[l], lam_init, TQ).reshape(tp, d)
        x1, ffn_rows, cls = _merge(u, gates, attn, xp,
                                   [pmod(2, TM_MERGE), pmod(4, TM_MERGE), pmod(3, TM_MERGE)], [mod_p] * 3,
                                   w_pool_bf, pool_scale[l], w_out_bf, ln1_g[l], ln1_b[l], w_r, b_r, alpha,
                                   TM_MERGE, seq_len=seq)
        slot_quantum = GATHER_WINDOW * sc_units
        n_slots = -(-(tp + N_CLASSES * (TM_GROUP - 1)) // slot_quantum) * slot_quantum
        pos, tile_class, tile_rows = _dispatch_plan(cls.reshape(tp), TM_GROUP, n_slots)
        class_experts = jnp.array([[g * EXPERTS_PER_GROUP + pair[k] for g in range(N_GROUPS) for pair in EXPERT_PAIRS]
                                   for k in range(TOP_K_IN_GROUP)], jnp.int32)
        tile_experts = class_experts[:, tile_class]
        rows_sorted = _move_rows(ffn_rows, pos, n_slots)
        outs["kp"].append(jnp.transpose(kt_f.reshape(bsz, N_HEADS, 2, HEAD_DIM, seq), (0, 4, 1, 2, 3)))
        outs["vp"].append(v_f.reshape(bsz, seq, N_HEADS, V_DIM))
        outs["pp"].append(u.reshape(bsz, seq, -1)[:, seq - POOL_BUF:])

        def smod(k, two_d=False):
            if two_d:
                return pl.BlockSpec((ts, d), lambda i, j: (0, k))
            return pl.BlockSpec((ts, d), lambda i: (0, k))

        tab_t_spec_s = pl.BlockSpec((HEAD_DIM // 2, ts), lambda i: (0, 0))
        qt_s, kt_f, s_new, v_f, u, gates = _inproj(
            xs, smod(1), smod(0), mod_s, mod_s, w_in[l],
            tabs_s, tab_t_spec_s, ts, 1, False)
        hb = ts // 2
        attn_half = lambda sl: _attn_sample(qt_s[0][:, sl], s_new[:, :, sl], v_f[sl], cache_k[l], cache_v[l],
                                            page_table[sl], lam_p, subln_g[l], lam_init)
        rows_sorted, attn_a = lax.optimization_barrier((rows_sorted, attn_half(slice(0, hb))))
        y_sorted = _moe_grouped(rows_sorted, tile_experts, tile_rows, wg_bf, wu_bf, wd_bf, TM_GROUP)
        y_tok, attn_b = lax.optimization_barrier((_move_rows(y_sorted, pos), attn_half(slice(hb, ts))))
        xp = _moe_finish(x1, y_tok, pmod(5, TM_MOE), mod_p, ln2_g[l], ln2_b[l], alpha, TM_MOE)
        attn = jnp.concatenate([attn_a, attn_b], axis=0)
        pooled, new_pool = _pool_sample(state_pool[l], u, past_len)
        x1, h2, comb = _merge(pooled, gates, attn, xs, [smod(2), smod(4), smod(3)], [mod_s] * 3,
                              w_pool[l], pool_scale[l], w_out[l], ln1_g[l], ln1_b[l], w_r, b_r, alpha, ts)
        xs = _moe(h2, comb, wg_bf, wu_bf, wd_bf, x1, smod(5, True), mod_s,
                  ln2_g[l], ln2_b[l], alpha, ts, E_BLK)
        outs["ks"].append(jnp.transpose(kt_f.reshape(N_HEADS, 2, HEAD_DIM, db, dseq), (3, 4, 0, 1, 2)))
        outs["vs"].append(v_f.reshape(db, dseq, N_HEADS, V_DIM))
        outs["ps"].append(new_pool)

    return (xp.reshape(bsz, seq, d), xs.reshape(db, dseq, d),
            jnp.stack(outs["kp"]), jnp.stack(outs["vp"]), jnp.stack(outs["pp"]),
            jnp.stack(outs["ks"]), jnp.stack(outs["vs"]), jnp.stack(outs["ps"]))
```

```python
import functools
import math

import jax
import jax.numpy as jnp
from jax import lax
from jax.experimental import pallas as pl
from jax.experimental.pallas import tpu as pltpu
from jax.experimental.pallas import tpu_sc as plsc

F32 = jnp.float32
BF16 = jnp.bfloat16

LANES = 128
VMEM_LIMIT = 56 << 20

N_MOD = 6
N_HEADS = 8
HEAD_DIM = 64
V_DIM = 2 * HEAD_DIM
ROPE_THETA = 10000.0
POOL_WINDOWS = (2, 4, 8, 16)
POOL_GROUP_DIM = 128
POOL_BUF = max(POOL_WINDOWS) - 1
N_GROUPS = 4
EXPERTS_PER_GROUP = 4
N_EXPERTS = N_GROUPS * EXPERTS_PER_GROUP
TOP_K_IN_GROUP = 2
EXPERT_PAIRS = tuple((a, b) for a in range(EXPERTS_PER_GROUP) for b in range(a + 1, EXPERTS_PER_GROUP))
PAIRS_PER_GROUP = len(EXPERT_PAIRS)
N_CLASSES = N_GROUPS * PAIRS_PER_GROUP
LN_EPS = 1e-5
SUBLN_EPS = 1e-5
ROUTER_ROWS = 32
GATHER_WINDOW = 128

TM_PROJ = 256
TQ = 1024
TM_MERGE = 1024
TM_MOE = 1024
TM_GROUP = 512
E_BLK = EXPERTS_PER_GROUP

NT_DIMS = (((1,), (1,)), ((), ()))


def _params(*sem):
    return pltpu.CompilerParams(dimension_semantics=sem, vmem_limit_bytes=VMEM_LIMIT)


def _sigmoid(x):
    return 1.0 / (1.0 + jnp.exp(-x))


def _layer_norm(x, g, b):
    mu = jnp.mean(x, axis=-1, keepdims=True)
    xc = x - mu
    var = jnp.mean(xc * xc, axis=-1, keepdims=True)
    return xc * lax.rsqrt(var + LN_EPS) * g + b


def _lambda(lam_ref, lam_init):
    lp = lam_ref[...]
    a = jnp.sum(lp[0:1] * lp[1:2], axis=1, keepdims=True)
    b = jnp.sum(lp[2:3] * lp[3:4], axis=1, keepdims=True)
    return jnp.exp(a) - jnp.exp(b) + lam_init


def _ada_kernel(c_ref, w_ref, b_ref, o_ref):
    c = c_ref[...]
    o_ref[...] = jnp.dot(c * _sigmoid(c), w_ref[...], preferred_element_type=F32,
                         precision=lax.Precision.HIGHEST) + b_ref[...]


def _ada(c_all, w_ada, b_ada):
    n, d = c_all.shape
    cols = w_ada.shape[1]
    return pl.pallas_call(
        _ada_kernel,
        grid=(cols // d,),
        in_specs=[pl.BlockSpec((n, d), lambda j: (0, 0)),
                  pl.BlockSpec((d, d), lambda j: (0, j)),
                  pl.BlockSpec((1, d), lambda j: (0, j))],
        out_specs=pl.BlockSpec((n, d), lambda j: (0, j)),
        out_shape=jax.ShapeDtypeStruct((n, cols), F32),
        compiler_params=_params("arbitrary"),
        name="ada_mod",
    )(c_all, w_ada, b_ada.reshape(1, cols))


def _inproj_kernel(x_ref, sc_ref, sh_ref, w_ref, cost_ref, sint_ref, *out_refs, d, pw, prompt):
    mm_dtype = w_ref.dtype
    prec = lax.Precision.HIGHEST if mm_dtype == F32 else None
    dot = functools.partial(jnp.dot, preferred_element_type=F32, precision=prec)
    h =(x_ref[...] * (1.0 + sc_ref[...]) + sh_ref[...]).astype(mm_dtype)
    half = HEAD_DIM // 2
    scale = HEAD_DIM ** -0.5 * (math.log2(math.e) if prompt else 1.0)
    cost, sint = cost_ref[...], sint_ref[...]

    def rope_rows(xt):
        out = []
        for g in range(d // HEAD_DIM):
            lo = slice(g * HEAD_DIM, g * HEAD_DIM + half)
            hi = slice(g * HEAD_DIM + half, (g + 1) * HEAD_DIM)
            x1, x2 = xt[lo], xt[hi]
            out.append((lo, x1 * cost - x2 * sint))
            out.append((hi, x2 * cost + x1 * sint))
        return out

    qt = rope_rows(dot(h, w_ref[:, 0:d]).T)
    kt = rope_rows(dot(h, w_ref[:, d:2 * d]).T)
    v = dot(h, w_ref[:, 2 * d:3 * d])
    if prompt:
        qt_ref, ktf_ref, kb_ref, vf_ref, vtb_ref, u_ref, g_ref = out_refs
        for sl, r in qt:
            qt_ref[sl, :] = (r * scale).astype(BF16)
        for sl, r in kt:
            ktf_ref[sl, :] = r
        kb_ref[...] = ktf_ref[...].T.astype(BF16)
        vtb_ref[...] = v.T.astype(BF16)
        g_dtype = BF16
    else:
        g_dtype = F32
        qt_ref, ktf_ref, sn_ref, vf_ref, u_ref, g_ref = out_refs
        for idx, ((sl, rq), (_, rk)) in enumerate(zip(qt, kt)):
            qs = rq * scale
            qt_ref[sl, :] = qs
            ktf_ref[sl, :] = rk
            part = jnp.sum(qs * rk, axis=0, keepdims=True)
            g, is_hi = idx // 2, idx % 2
            head, which = g // 2, g % 2
            if is_hi:
                sn_ref[which, head:head + 1, :] += part
            else:
                sn_ref[which, head:head + 1, :] = part
    vf_ref[...] = v
    u_ref[...] = dot(h, w_ref[:, 3 * d:3 * d + pw])
    g_ref[...] = _sigmoid(dot(h, w_ref[:, 3 * d + pw:])).astype(g_dtype)


def _inproj(x, sc_spec, sh_spec, sc_arr, sh_arr, w_in, tabs_t, tab_t_spec, tm, n_seq, prompt):
    t, d = x.shape
    cols = w_in.shape[1]
    pw = cols - 5 * d
    per_seq = t // n_seq
    nblk = per_seq // tm
    row = lambda width: pl.BlockSpec((tm, width), lambda i: (i, 0))
    ft_spec = pl.BlockSpec((None, d, tm), lambda i: (i // nblk, 0, i % nblk))
    ft_shape = lambda dt: jax.ShapeDtypeStruct((n_seq, d, per_seq), dt)
    tok_shape = lambda width, dt: jax.ShapeDtypeStruct((t, width), dt)
    if prompt:
        out_specs = [ft_spec, ft_spec, row(d), row(d), ft_spec, row(pw), row(2 * d)]
        out_shape = [ft_shape(BF16), ft_shape(F32), tok_shape(d, BF16), tok_shape(d, F32), ft_shape(BF16),
                     tok_shape(pw, F32), tok_shape(2 * d, BF16)]
    else:
        assert n_seq == 1 and nblk == 1
        out_specs = [ft_spec, ft_spec, pl.BlockSpec((2, N_HEADS, tm), lambda i: (0, 0, 0)), row(d),
                     row(pw), row(2 * d)]
        out_shape = [ft_shape(F32), ft_shape(F32), jax.ShapeDtypeStruct((2, N_HEADS, t), F32),
                     tok_shape(d, F32), tok_shape(pw, F32), tok_shape(2 * d, F32)]
    return pl.pallas_call(
        functools.partial(_inproj_kernel, d=d, pw=pw, prompt=prompt),
        grid=(t // tm,),
        in_specs=[row(d), sc_spec, sh_spec,
                  pl.BlockSpec((d, cols), lambda i: (0, 0), pipeline_mode=pl.Buffered(1)),
                  tab_t_spec, tab_t_spec],
        out_specs=out_specs,
        out_shape=out_shape,
        compiler_params=_params("arbitrary"),
        name="in_proj",
    )(x, sc_arr, sh_arr, w_in, *tabs_t)


def _rope_tables(pos):
    half = HEAD_DIM // 2
    inv = ROPE_THETA ** (-jnp.arange(half, dtype=F32) / half)
    ang = inv[:, None] * pos.astype(F32)[None, :]
    return jnp.cos(ang), jnp.sin(ang)


def _attn_kernel(lam_ref, qt_ref, k_ref, vt_ref, g_ref, o_ref, q2_sc, sta_sc, stb_sc, m_sc, l_sc, acc_sc,
                 *, tq, lam_init):
    qi = pl.program_id(2)
    zero = jnp.zeros((HEAD_DIM, tq), BF16)
    q2_sc[0:HEAD_DIM, 0:tq] = qt_ref[0:HEAD_DIM, :]
    q2_sc[HEAD_DIM:V_DIM, 0:tq] = zero
    q2_sc[0:HEAD_DIM, tq:2 * tq] = zero
    q2_sc[HEAD_DIM:V_DIM, tq:2 * tq] = qt_ref[HEAD_DIM:V_DIM, :]
    m_sc[...] = jnp.full(m_sc.shape, -jnp.inf, F32)
    l_sc[...] = jnp.zeros(l_sc.shape, F32)
    acc_sc[...] = jnp.zeros(acc_sc.shape, F32)

    def scores(ki, st_ref):
        start = pl.multiple_of(ki * tq, tq)
        st_ref[...] = jnp.dot(k_ref[pl.ds(start, tq), :], q2_sc[...], preferred_element_type=F32)

    def consume(ki, st_ref, masked):
        st = st_ref[...]
        if masked:
            key = lax.broadcasted_iota(jnp.int32, st.shape, 0)
            qry = lax.broadcasted_iota(jnp.int32, st.shape, 1)
            qry = jnp.where(qry >= tq, qry - tq, qry)
            st = jnp.where(key <= qry, st, -jnp.inf)
        m_old = m_sc[...]
        m_new = jnp.maximum(m_old, jnp.max(st, axis=0, keepdims=True))
        alpha = jnp.exp2(m_old - m_new)
        p = jnp.exp2(st - m_new).astype(BF16)
        start = pl.multiple_of(ki * tq, tq)
        vt = vt_ref[:, pl.ds(start, tq)]
        vt1 = jnp.concatenate([vt, jnp.ones((16, tq), BF16)], axis=0)
        r = jnp.dot(vt1, p, preferred_element_type=F32)
        l_sc[...] = alpha * l_sc[...] + r[V_DIM:V_DIM + 1]
        acc_sc[...] = alpha * acc_sc[...] + r[0:V_DIM]
        m_sc[...] = m_new

    scores(0, sta_sc)

    def pair(pi, carry):
        k0 = 2 * pi
        scores(k0 + 1, stb_sc)
        consume(k0, sta_sc, False)
        scores(k0 + 2, sta_sc)
        consume(k0 + 1, stb_sc, False)
        return carry

    lax.fori_loop(0, qi // 2, pair, 0)

    @pl.when(qi % 2 == 0)
    def _():
        consume(qi, sta_sc, True)

    @pl.when(qi % 2 == 1)
    def _():
        scores(qi, stb_sc)
        consume(qi - 1, sta_sc, False)
        consume(qi, stb_sc, True)

    lam = _lambda(lam_ref, lam_init)
    inv_l = 1.0 / l_sc[...]
    ot = acc_sc[:, 0:tq] * inv_l[:, 0:tq] - lam * (acc_sc[:, tq:2 * tq] * inv_l[:, tq:2 * tq])
    ms = jnp.mean(ot * ot, axis=0, keepdims=True)
    ot = ot * lax.rsqrt(ms + SUBLN_EPS) * g_ref[...] * (1.0 - lam_init)
    o_ref[...] = ot.T.astype(o_ref.dtype)


def _attn_prompt(qt_bf, k_bf, vt_bf, lam_p, subln_g, lam_init, tq):
    b, d, s = qt_bf.shape
    nh = d // V_DIM
    return pl.pallas_call(
        functools.partial(_attn_kernel, tq=tq, lam_init=lam_init),
        grid=(b, nh, s // tq),
        in_specs=[pl.BlockSpec(lam_p.shape, lambda bi, h, qi: (0, 0)),
                  pl.BlockSpec((None, V_DIM, tq), lambda bi, h, qi: (bi, h, qi)),
                  pl.BlockSpec((None, s, V_DIM), lambda bi, h, qi: (bi, 0, h)),
                  pl.BlockSpec((None, V_DIM, s), lambda bi, h, qi: (bi, h, 0)),
                  pl.BlockSpec((V_DIM, 1), lambda bi, h, qi: (0, 0))],
        out_specs=pl.BlockSpec((None, tq, V_DIM), lambda bi, h, qi: (bi, qi, h)),
        out_shape=jax.ShapeDtypeStruct((b, s, d), BF16),
        scratch_shapes=[pltpu.VMEM((V_DIM, 2 * tq), BF16),
                        pltpu.VMEM((tq, 2 * tq), F32),
                        pltpu.VMEM((tq, 2 * tq), F32),
                        pltpu.VMEM((1, 2 * tq), F32),
                        pltpu.VMEM((1, 2 * tq), F32),
                        pltpu.VMEM((V_DIM, 2 * tq), F32)],
        compiler_params=_params("arbitrary", "arbitrary", "arbitrary"),
        name="attn_prompt",
    )(lam_p, qt_bf, k_bf, vt_bf, subln_g.reshape(V_DIM, 1))


def _attn_sample_kernel(pt_ref, lam_ref, qt_ref, sn_ref, vn_ref, g_ref, *refs, n_pages, lam_init):
    del pt_ref
    kt_refs, v_refs = refs[:n_pages], refs[n_pages:2 * n_pages]
    o_ref, s_sc = refs[2 * n_pages:]
    b = pl.program_id(0)
    d, n_tok = qt_ref.shape
    page = kt_refs[0].shape[1]

    onehot = jnp.where(lax.broadcasted_iota(jnp.int32, (n_tok, page), 0) == b, 1.0, 0.0)
    qb = jnp.dot(qt_ref[...], onehot, preferred_element_type=F32,
                 precision=lax.Precision.HIGHEST)
    for p in range(n_pages):
        prod = (kt_refs[p][...] * qb).reshape(N_HEADS, V_DIM, page)
        s_sc[0, :, p * page:(p + 1) * page] = jnp.sum(prod[:, 0:HEAD_DIM, :], axis=1)
        s_sc[1, :, p * page:(p + 1) * page] = jnp.sum(prod[:, HEAD_DIM:V_DIM, :], axis=1)

    own = lax.broadcasted_iota(jnp.int32, (N_HEADS, n_tok), 1) == b

    def softmax_parts(which):
        s_new = jnp.max(jnp.where(own, sn_ref[which], -jnp.inf), axis=1, keepdims=True)
        s = s_sc[which]
        m = jnp.maximum(jnp.max(s, axis=1, keepdims=True), s_new)
        p = jnp.exp(s - m)
        p_new = jnp.exp(s_new - m)
        inv_l = 1.0 / (jnp.sum(p, axis=1, keepdims=True) + p_new)
        return p * inv_l, p_new * inv_l

    lam = _lambda(lam_ref, lam_init)
    w1, wn1 = softmax_parts(0)
    w2, wn2 = softmax_parts(1)
    w = w1 - lam * w2
    acc = (wn1 - lam * wn2) * vn_ref[...]
    for p in range(n_pages):
        for t in range(page):
            c = p * page + t
            acc = acc + w[:, c:c + 1] * v_refs[p][t]
    ms = jnp.mean(acc * acc, axis=1, keepdims=True)
    o_ref[...] = acc * lax.rsqrt(ms + SUBLN_EPS) * g_ref[...] * (1.0 - lam_init)


def _attn_sample(qt, s_new, v_new, cache_k, cache_v, page_table, lam_p, subln_g, lam_init):
    d, db = qt.shape
    n_pages = page_table.shape[1]
    n_phys, page = cache_k.shape[0], cache_k.shape[1]
    ck = jnp.transpose(cache_k, (0, 2, 3, 4, 1)).reshape(n_phys, d, page)
    pt = page_table.reshape(-1)

    def page_spec(p, shape):
        zeros = (0,) * len(shape)
        return pl.BlockSpec((None,) + shape, lambda b, pt_ref: (pt_ref[b * n_pages + p],) + zeros)

    const = lambda shape: pl.BlockSpec(shape, lambda b, pt_ref: (0,) * len(shape))
    head_spec = pl.BlockSpec((None, N_HEADS, V_DIM), lambda b, pt_ref: (b, 0, 0))
    grid_spec = pltpu.PrefetchScalarGridSpec(
        num_scalar_prefetch=1,
        grid=(db,),
        in_specs=[const(lam_p.shape), const((d, db)), const((2, N_HEADS, db)), head_spec, const((1, V_DIM))]
                 + [page_spec(p, (d, page)) for p in range(n_pages)]
                 + [page_spec(p, (page, N_HEADS, V_DIM)) for p in range(n_pages)],
        out_specs=head_spec,
        scratch_shapes=[pltpu.VMEM((2, N_HEADS, n_pages * page), F32)],
    )
    out = pl.pallas_call(
        functools.partial(_attn_sample_kernel, n_pages=n_pages, lam_init=lam_init),
        grid_spec=grid_spec,
        out_shape=jax.ShapeDtypeStruct((db, N_HEADS, V_DIM), F32),
        compiler_params=_params("arbitrary"),
        name="attn_sample",
    )(pt, lam_p, qt, s_new, v_new.reshape(db, N_HEADS, V_DIM), subln_g.reshape(1, V_DIM),
      *([ck] * n_pages), *([cache_v] * n_pages))
    return out.reshape(db, d)


POOL_HALO = 16


def _pool_tile(u_ref, halo_ref, ext_sc, seq_tile):
    tm = u_ref.shape[0]
    hb = halo_ref.shape[0]

    @pl.when(seq_tile == 0)
    def _():
        ext_sc[0:hb, :] = jnp.zeros((hb, ext_sc.shape[1]), F32)

    @pl.when(seq_tile > 0)
    def _():
        ext_sc[0:hb, :] = halo_ref[...]

    ext_sc[hb:hb + tm, :] = u_ref[...]
    pos = seq_tile * tm + lax.broadcasted_iota(jnp.int32, (tm, POOL_GROUP_DIM), 0)
    parts = []
    for g, win in enumerate(POOL_WINDOWS):
        sl = slice(g * POOL_GROUP_DIM, (g + 1) * POOL_GROUP_DIM)
        cur = ext_sc[hb:hb + tm, sl]
        tot = cur
        for j in range(1, win):
            tot = tot + ext_sc[hb - j:hb - j + tm, sl]
        cnt = jnp.minimum(win, pos + 1).astype(F32)
        parts.append(tot / cnt - cur)
    return jnp.concatenate(parts, axis=1)


def _pool_sample_kernel(st_ref, u_ref, o_ref, np_ref, *, start_pos):
    nb = st_ref.shape[0]
    u = u_ref[...]
    for g, win in enumerate(POOL_WINDOWS):
        sl = slice(g * POOL_GROUP_DIM, (g + 1) * POOL_GROUP_DIM)
        cur = u[:, sl]
        tot = cur
        for j in range(1, win):
            tot = tot + st_ref[nb - j, :, sl]
        o_ref[:, sl] = tot / float(min(win, start_pos + 1)) - cur
    for j in range(nb - 1):
        np_ref[j] = st_ref[j + 1]
    np_ref[nb - 1] = u


def _pool_sample(state, u, start_pos):
    db, nb, pw = state.shape
    full = lambda shape: pl.BlockSpec(shape, lambda i: (0,) * len(shape))
    pooled, new_pool = pl.pallas_call(
        functools.partial(_pool_sample_kernel, start_pos=start_pos),
        grid=(1,),
        in_specs=[full((nb, db, pw)), full((db, pw))],
        out_specs=[full((db, pw)), full((nb, db, pw))],
        out_shape=[jax.ShapeDtypeStruct((db, pw), F32), jax.ShapeDtypeStruct((nb, db, pw), F32)],
        compiler_params=_params("arbitrary"),
        name="pool_sample",
    )(jnp.transpose(state, (1, 0, 2)), u)
    return pooled, jnp.transpose(new_pool, (1, 0, 2))


def _route(lt):
    g = [lt[j:j + 1] for j in range(N_GROUPS)]
    gmax = functools.reduce(jnp.maximum, g)
    gidx = jnp.full(gmax.shape, N_GROUPS - 1, jnp.int32)
    for j in range(N_GROUPS - 2, -1, -1):
        gidx = jnp.where(g[j] == gmax, j, gidx)
    p_g = 1.0 / functools.reduce(jnp.add, [jnp.exp(gj - gmax) for gj in g])
    e = []
    for k in range(EXPERTS_PER_GROUP):
        sel = lt[N_GROUPS + k:N_GROUPS + k + 1]
        for j in range(1, N_GROUPS):
            r = N_GROUPS + j * EXPERTS_PER_GROUP + k
            sel = jnp.where(gidx == j, lt[r:r + 1], sel)
        e.append(sel)

    def first_argmax(vals):
        vmax = functools.reduce(jnp.maximum, vals)
        idx = jnp.full(vmax.shape, len(vals) - 1, jnp.int32)
        for k in range(len(vals) - 2, -1, -1):
            idx = jnp.where(vals[k] == vmax, k, idx)
        return vmax, idx

    v1, i1 = first_argmax(e)
    rest = [jnp.where(i1 == k, -jnp.inf, e[k]) for k in range(EXPERTS_PER_GROUP)]
    v2, i2 = first_argmax(rest)
    t = jnp.exp(v2 - v1)
    w1 = p_g / (1.0 + t)
    w2 = p_g * t / (1.0 + t)
    within = [jnp.where(i1 == k, w1, 0.0) + jnp.where(i2 == k, w2, 0.0) for k in range(EXPERTS_PER_GROUP)]
    rows = [jnp.where(gidx == j, within[k], 0.0) for j in range(N_GROUPS) for k in range(EXPERTS_PER_GROUP)]
    a, b = jnp.minimum(i1, i2), jnp.maximum(i1, i2)
    pair = jnp.where(a == 0, b - 1, jnp.where(a == 1, b + 1, 5))
    return jnp.concatenate(rows, axis=0), gidx * PAIRS_PER_GROUP + pair


def _pack_bf16_pair(a, b):
    a_bits = lax.bitcast_convert_type(a.astype(BF16).astype(F32), jnp.uint32)
    b_bits = lax.bitcast_convert_type(b.astype(BF16).astype(F32), jnp.uint32)
    return (a_bits >> 16) | (b_bits & jnp.uint32(0xFFFF0000))


def _unpack_bf16_pair(w):
    lo = lax.bitcast_convert_type(w << 16, F32)
    hi = lax.bitcast_convert_type(w & jnp.uint32(0xFFFF0000), F32)
    return lo, hi


def _merge_kernel(*refs, alpha, tiles_per_seq):
    if tiles_per_seq:
        u_ref, halo_ref, *refs, ext_sc = refs
        pooled = _pool_tile(u_ref, halo_ref, ext_sc, pl.program_id(0) % tiles_per_seq)
    else:
        pooled_ref, *refs = refs
        pooled = pooled_ref[...]
    (gates_ref, attn_ref, x_ref, g1_ref, sc2_ref, sh2_ref, wpool_ref, pscale_ref, wout_ref, lng_ref, lnb_ref,
     wr_ref, br_ref, x1_ref, *route_refs) = refs
    d = x_ref.shape[1]
    tm = x_ref.shape[0]
    mm_dtype = wout_ref.dtype
    dot = functools.partial(jnp.dot, preferred_element_type=F32,
                            precision=lax.Precision.HIGHEST if mm_dtype == F32 else None)
    pooled = pooled.astype(mm_dtype)
    parts = [dot(pooled[:, g * POOL_GROUP_DIM:(g + 1) * POOL_GROUP_DIM], wpool_ref[g])
             for g in range(len(POOL_WINDOWS))]
    pool_out = jnp.concatenate(parts, axis=1) * pscale_ref[...]
    g_dtype = gates_ref.dtype
    merged = (gates_ref[:, 0:d] * pool_out.astype(g_dtype)
              + gates_ref[:, d:2 * d] * attn_ref[...].astype(g_dtype))
    y = dot(merged.astype(mm_dtype), wout_ref[...])
    x1 = _layer_norm(alpha * x_ref[...] + (1.0 + g1_ref[...]) * y, lng_ref[...], lnb_ref[...])
    x1_ref[...] = x1
    h2 = x1 * (1.0 + sc2_ref[...]) + sh2_ref[...]
    lt = lax.dot_general(wr_ref[...], h2, NT_DIMS, preferred_element_type=F32,
                         precision=lax.Precision.HIGHEST) + br_ref[...]
    comb_t, cls = _route(lt)
    comb = jnp.concatenate([comb_t, jnp.zeros((LANES - N_EXPERTS, tm), F32)], axis=0).T
    if tiles_per_seq:
        row_ref, cls_ref = route_refs
        row_ref[:, 0:d // 2] = _pack_bf16_pair(h2[:, 0:d // 2], h2[:, d // 2:d])
        row_ref[:, d // 2:d // 2 + LANES] = lax.bitcast_convert_type(comb, jnp.uint32)
        cls_ref[...] = cls
    else:
        h2_ref, comb_ref = route_refs
        h2_ref[...] = h2.astype(BF16)
        comb_ref[...] = comb


def _merge(pool_in, gates, attn, x, mod_specs, mod_arrs, w_pool, pool_scale, w_out,
           ln_g, ln_b, w_r, b_r, alpha, tm, seq_len=None):
    t, d = x.shape
    pw = pool_in.shape[1]
    row = lambda width: pl.BlockSpec((tm, width), lambda i: (i, 0))
    const = lambda shape: pl.BlockSpec(shape, lambda i: (0,) * len(shape))
    if seq_len:
        halo_spec = pl.BlockSpec((POOL_HALO, pw), lambda i: (jnp.maximum(i * (tm // POOL_HALO) - 1, 0), 0))
        pool_specs, pool_args = [row(pw), halo_spec], [pool_in, pool_in]
        scratch = [pltpu.VMEM((POOL_HALO + tm, pw), F32)]
    else:
        pool_specs, pool_args, scratch = [row(pw)], [pool_in], []
    if seq_len:
        route_specs = [row(d // 2 + LANES), pl.BlockSpec((None, 1, tm), lambda i: (i, 0, 0))]
        route_shapes = [jax.ShapeDtypeStruct((t, d // 2 + LANES), jnp.uint32),
                        jax.ShapeDtypeStruct((t // tm, 1, tm), jnp.int32)]
    else:
        route_specs = [row(d), row(LANES)]
        route_shapes = [jax.ShapeDtypeStruct((t, d), BF16), jax.ShapeDtypeStruct((t, LANES), F32)]
    return pl.pallas_call(
        functools.partial(_merge_kernel, alpha=alpha, tiles_per_seq=seq_len // tm if seq_len else None),
        grid=(t // tm,),
        in_specs=[*pool_specs, row(2 * d), row(d), row(d), *mod_specs,
                  const(w_pool.shape), const((1, d)), const((d, d)), const((1, d)), const((1, d)),
                  const(w_r.shape), const(b_r.shape)],
        out_specs=[row(d), *route_specs],
        out_shape=[jax.ShapeDtypeStruct((t, d), F32), *route_shapes],
        scratch_shapes=scratch,
        compiler_params=_params("arbitrary"),
        name="merge",
    )(*pool_args, gates, attn, x, *mod_arrs, w_pool, pool_scale.reshape(1, d), w_out,
      ln_g.reshape(1, d), ln_b.reshape(1, d), w_r, b_r)


def _moe_kernel(h2_ref, comb_ref, wg_ref, wu_ref, wd_ref, x1_ref, g2_ref, lng_ref, lnb_ref,
                o_ref, acc_sc, *, alpha):
    j = pl.program_id(1)
    eb = wg_ref.shape[0]

    @pl.when(j == 0)
    def _():
        acc_sc[...] = jnp.zeros(acc_sc.shape, F32)

    h2 = h2_ref[...]
    comb = comb_ref[...]
    lane = lax.broadcasted_iota(jnp.int32, comb.shape, 1)
    for e in range(eb):
        hg = jnp.dot(h2, wg_ref[e], preferred_element_type=F32)
        hu = jnp.dot(h2, wu_ref[e], preferred_element_type=F32)
        c = jnp.sum(jnp.where(lane == j * eb + e, comb, 0.0), axis=1, keepdims=True)
        act = (hg * _sigmoid(hg)) * hu * c
        acc_sc[...] += jnp.dot(act.astype(BF16), wd_ref[e], preferred_element_type=F32)

    @pl.when(j == pl.num_programs(1) - 1)
    def _():
        y = alpha * x1_ref[...] + (1.0 + g2_ref[...]) * acc_sc[...]
        o_ref[...] = _layer_norm(y, lng_ref[...], lnb_ref[...])


def _moe(h2, comb, wg_bf, wu_bf, wd_bf, x1, g2_spec, g2_arr, ln_g, ln_b, alpha, tm, eb):
    t, d = x1.shape
    ne, _, ff = wg_bf.shape
    row = lambda width: pl.BlockSpec((tm, width), lambda i, j: (i, 0))
    const = lambda shape: pl.BlockSpec(shape, lambda i, j: (0,) * len(shape))
    return pl.pallas_call(
        functools.partial(_moe_kernel, alpha=alpha),
        grid=(t // tm, ne // eb),
        in_specs=[row(d), row(LANES),
                  pl.BlockSpec((eb, d, ff), lambda i, j: (j, 0, 0)),
                  pl.BlockSpec((eb, d, ff), lambda i, j: (j, 0, 0)),
                  pl.BlockSpec((eb, ff, d), lambda i, j: (j, 0, 0)),
                  row(d), g2_spec, const((1, d)), const((1, d))],
        out_specs=row(d),
        out_shape=jax.ShapeDtypeStruct((t, d), F32),
        scratch_shapes=[pltpu.VMEM((tm, d), F32)],
        compiler_params=_params("arbitrary", "arbitrary"),
        name="moe_ffn",
    )(h2, comb, wg_bf, wu_bf, wd_bf, x1, g2_arr, ln_g.reshape(1, d), ln_b.reshape(1, d))


def _move_rows(x, idx, n_out=None):
    n, width = idx.shape[0], x.shape[1]
    scatter = n_out is not None
    info = pltpu.get_tpu_info().sparse_core
    units = info.num_cores * info.num_subcores
    per_unit = n // (units * GATHER_WINDOW)
    assert per_unit * units * GATHER_WINDOW == n

    @pl.kernel(out_type=jax.ShapeDtypeStruct((n_out if scatter else n, width), x.dtype),
               mesh=plsc.VectorSubcoreMesh(core_axis_name="core", subcore_axis_name="subcore"),
               scratch_types=[pltpu.VMEM((1, GATHER_WINDOW), jnp.int32), pltpu.VMEM((GATHER_WINDOW, width), x.dtype)])
    def move_kernel(x_hbm, i_hbm, o_hbm, i_vmem, buf):
        unit = lax.axis_index("core") * info.num_subcores + lax.axis_index("subcore")

        @pl.loop(0, per_unit)
        def _(j):
            base = (unit * per_unit + j) * GATHER_WINDOW
            pltpu.sync_copy(i_hbm.at[:, pl.ds(base, GATHER_WINDOW)], i_vmem)
            if scatter:
                pltpu.sync_copy(x_hbm.at[pl.ds(base, GATHER_WINDOW), :], buf)
                pltpu.sync_copy(buf, o_hbm.at[i_vmem.at[0]])
            else:
                pltpu.sync_copy(x_hbm.at[i_vmem.at[0]], buf)
                pltpu.sync_copy(buf, o_hbm.at[pl.ds(base, GATHER_WINDOW), :])

    return move_kernel(x, idx.reshape(1, n))


def _dispatch_plan(cls, tile, n_slots):
    onehot = (cls[:, None] == jnp.arange(N_CLASSES, dtype=jnp.int32)[None, :]).astype(jnp.int32)
    rank = jnp.cumsum(onehot, axis=0) - onehot
    counts = jnp.sum(onehot, axis=0)
    padded = -(-counts // tile) * tile
    ends = jnp.cumsum(padded)
    starts = ends - padded
    pos = jnp.sum(onehot * (starts[None, :] + rank), axis=1)
    tile_start = jnp.arange(n_slots // tile, dtype=jnp.int32) * tile
    tile_class = jnp.minimum(jnp.sum((tile_start[:, None] >= ends[None, :]).astype(jnp.int32), axis=1),
                             N_CLASSES - 1)
    tile_rows = jnp.clip((starts + counts)[tile_class] - tile_start, 0, tile)
    return pos, tile_class, tile_rows


def _moe_grouped_kernel(te_ref, tr_ref, row_ref, *refs):
    *w_refs, y_ref = refs
    n_rows = tr_ref[pl.program_id(0)]
    half = y_ref.shape[1]

    @pl.when(n_rows == 0)
    def _():
        y_ref[...] = jnp.zeros(y_ref.shape, y_ref.dtype)

    @pl.when(n_rows > 0)
    def _():
        live = lax.broadcasted_iota(jnp.int32, (row_ref.shape[0], 1), 0) < n_rows
        lo, hi = _unpack_bf16_pair(row_ref[:, 0:half])
        h2 = jnp.where(live, jnp.concatenate([lo, hi], axis=1), 0.0).astype(BF16)
        comb = jnp.where(live, lax.bitcast_convert_type(row_ref[:, half:half + LANES], F32), 0.0)
        lane = lax.broadcasted_iota(jnp.int32, comb.shape, 1)
        acc = jnp.zeros((h2.shape[0], 2 * half), F32)
        for k in range(TOP_K_IN_GROUP):
            wg_ref, wu_ref, wd_ref = w_refs[3 * k:3 * k + 3]
            expert = te_ref[k, pl.program_id(0)]
            hg = jnp.dot(h2, wg_ref[...], preferred_element_type=F32)
            hu = jnp.dot(h2, wu_ref[...], preferred_element_type=F32)
            c = jnp.sum(jnp.where(lane == expert, comb, 0.0), axis=1, keepdims=True)
            act = (hg * _sigmoid(hg)) * hu * c
            acc = acc + jnp.dot(act.astype(BF16), wd_ref[...], preferred_element_type=F32)
        y_ref[...] = _pack_bf16_pair(acc[:, 0:half], acc[:, half:2 * half])


def _moe_grouped(rows_sorted, tile_experts, tile_rows, wg_bf, wu_bf, wd_bf, tile):
    n_slots, width = rows_sorted.shape
    ne, d, ff = wg_bf.shape
    expert_w = lambda k, shape: pl.BlockSpec((None,) + shape, lambda i, te, tr: (te[k, i], 0, 0))
    w_specs, w_args = [], []
    for k in range(TOP_K_IN_GROUP):
        w_specs += [expert_w(k, (d, ff)), expert_w(k, (d, ff)), expert_w(k, (ff, d))]
        w_args += [wg_bf, wu_bf, wd_bf]
    grid_spec = pltpu.PrefetchScalarGridSpec(
        num_scalar_prefetch=2,
        grid=(n_slots // tile,),
        in_specs=[pl.BlockSpec((tile, width), lambda i, te, tr: (i, 0)), *w_specs],
        out_specs=pl.BlockSpec((tile, d // 2), lambda i, te, tr: (i, 0)),
    )
    return pl.pallas_call(
        _moe_grouped_kernel,
        grid_spec=grid_spec,
        out_shape=jax.ShapeDtypeStruct((n_slots, d // 2), jnp.uint32),
        compiler_params=_params("arbitrary"),
        name="moe_grouped",
    )(tile_experts, tile_rows, rows_sorted, *w_args)


def _moe_finish_kernel(x1_ref, y_ref, g2_ref, lng_ref, lnb_ref, o_ref, *, alpha):
    lo, hi = _unpack_bf16_pair(y_ref[...])
    y = jnp.concatenate([lo, hi], axis=1)
    o_ref[...] = _layer_norm(alpha * x1_ref[...] + (1.0 + g2_ref[...]) * y, lng_ref[...], lnb_ref[...])


def _moe_finish(x1, y_packed, g2_spec, g2_arr, ln_g, ln_b, alpha, tm):
    t, d = x1.shape
    row = lambda width: pl.BlockSpec((tm, width), lambda i: (i, 0))
    const = lambda shape: pl.BlockSpec(shape, lambda i: (0,) * len(shape))
    return pl.pallas_call(
        functools.partial(_moe_finish_kernel, alpha=alpha),
        grid=(t // tm,),
        in_specs=[row(d), row(d // 2), g2_spec, const((1, d)), const((1, d))],
        out_specs=row(d),
        out_shape=jax.ShapeDtypeStruct((t, d), F32),
        compiler_params=_params("arbitrary"),
        name="moe_finish",
    )(x1, y_packed, g2_arr, ln_g.reshape(1, d), ln_b.reshape(1, d))


def kernel(x_prompt, x_sample, cache_k, cache_v, state_pool, page_table, c_prompt, c_sample, w_ada, b_ada, w_in, lambda_q1, lambda_k1, lambda_q2, lambda_k2, subln_g, w_pool, pool_scale, w_out, ln1_g, ln1_b, w_group, b_group, w_router, b_router, w_e_gate, w_e_up, w_e_down, ln2_g, ln2_b):
    depth = w_in.shape[0]
    bsz, seq, d = x_prompt.shape
    db, dseq, _ = x_sample.shape
    assert dseq == 1, "sample group decodes one token per sequence"
    n_pages, page = page_table.shape[1], cache_k.shape[2]
    past_len = n_pages * page
    alpha = (2.0 * depth) ** 0.25
    tp, ts = bsz * seq, db * dseq

    sc_info = pltpu.get_tpu_info().sparse_core
    sc_units = sc_info.num_cores * sc_info.num_subcores
    assert tp % (GATHER_WINDOW * sc_units) == 0

    xp = x_prompt.reshape(tp, d)
    xs = x_sample.reshape(ts, d)
    n_c = bsz + db
    n_c_pad = -(-n_c // 8) * 8
    c_all = jnp.concatenate([c_prompt, c_sample, jnp.zeros((n_c_pad - n_c, d), F32)], axis=0)
    tabs_p = _rope_tables(jnp.arange(seq))
    tabs_s = _rope_tables(jnp.full((ts,), past_len))

    outs = {k: [] for k in ("kp", "vp", "pp", "ks", "vs", "ps")}
    for l in range(depth):
        lam_init = 0.8 - 0.6 * math.exp(-0.3 * l)
        mod = _ada(c_all, w_ada[l], b_ada[l])
        mod_p = mod[:bsz].reshape(bsz * N_MOD, 1, d)
        mod_s = mod[bsz:n_c]
        w_in_bf = w_in[l].astype(BF16)
        w_pool_bf = w_pool[l].astype(BF16)
        w_out_bf = w_out[l].astype(BF16)
        wg_bf, wu_bf, wd_bf = (w[l].astype(BF16) for w in (w_e_gate, w_e_up, w_e_down))
        lam_p = jnp.stack([lambda_q1[l], lambda_k1[l], lambda_q2[l], lambda_k2[l]])
        n_r = N_GROUPS + N_EXPERTS
        w_r = jnp.concatenate([w_group[l], w_router[l]], axis=1).T
        w_r = jnp.concatenate([w_r, jnp.zeros((ROUTER_ROWS - n_r, d), F32)], axis=0)
        b_r = jnp.concatenate([b_group[l], b_router[l], jnp.zeros((ROUTER_ROWS - n_r,), F32)]).reshape(ROUTER_ROWS, 1)

        def pmod(k, tm, two_d=False):
            per_b = seq // tm
            if two_d:
                return pl.BlockSpec((None, 1, d), lambda i, j: ((i // per_b) * N_MOD + k, 0, 0))
            return pl.BlockSpec((None, 1, d), lambda i: ((i // per_b) * N_MOD + k, 0, 0))

        nblk = seq // TM_PROJ
        tab_t_spec = pl.BlockSpec((HEAD_DIM // 2, TM_PROJ), lambda i: (0, i % nblk))
        qt_bf, kt_f, k_bf, v_f, vt_bf, u, gates = _inproj(
            xp, pmod(1, TM_PROJ), pmod(0, TM_PROJ), mod_p, mod_p, w_in_bf,
            tabs_p, tab_t_spec, TM_PROJ, bsz, True)
        attn = _attn_prompt(qt_bf, k_bf.reshape(bsz, seq, d), vt_bf,
                            lam_p, subln_g[l], lam_init, TQ).reshape(tp, d)
        x1, ffn_rows, cls = _merge(u, gates, attn, xp,
                                   [pmod(2, TM_MERGE), pmod(4, TM_MERGE), pmod(3, TM_MERGE)], [mod_p] * 3,
                                   w_pool_bf, pool_scale[l], w_out_bf, ln1_g[l], ln1_b[l], w_r, b_r, alpha,
                                   TM_MERGE, seq_len=seq)
        slot_quantum = GATHER_WINDOW * sc_units
        n_slots = -(-(tp + N_CLASSES * (TM_GROUP - 1)) // slot_quantum) * slot_quantum
        pos, tile_class, tile_rows = _dispatch_plan(cls.reshape(tp), TM_GROUP, n_slots)
        class_experts = jnp.array([[g * EXPERTS_PER_GROUP + pair[k] for g in range(N_GROUPS) for pair in EXPERT_PAIRS]
                                   for k in range(TOP_K_IN_GROUP)], jnp.int32)
        tile_experts = class_experts[:, tile_class]
        rows_sorted = _move_rows(ffn_rows, pos, n_slots)
        outs["kp"].append(jnp.transpose(kt_f.reshape(bsz, N_HEADS, 2, HEAD_DIM, seq), (0, 4, 1, 2, 3)))
        outs["vp"].append(v_f.reshape(bsz, seq, N_HEADS, V_DIM))
        outs["pp"].append(u.reshape(bsz, seq, -1)[:, seq - POOL_BUF:])

        def smod(k, two_d=False):
            if two_d:
                return pl.BlockSpec((ts, d), lambda i, j: (0, k))
            return pl.BlockSpec((ts, d), lambda i: (0, k))

        tab_t_spec_s = pl.BlockSpec((HEAD_DIM // 2, ts), lambda i: (0, 0))
        qt_s, kt_f, s_new, v_f, u, gates = _inproj(
            xs, smod(1), smod(0), mod_s, mod_s, w_in[l],
            tabs_s, tab_t_spec_s, ts, 1, False)
        hb = ts // 2
        attn_half = lambda sl: _attn_sample(qt_s[0][:, sl], s_new[:, :, sl], v_f[sl], cache_k[l], cache_v[l],
                                            page_table[sl], lam_p, subln_g[l], lam_init)
        rows_sorted, attn_a = lax.optimization_barrier((rows_sorted, attn_half(slice(0, hb))))
        y_sorted = _moe_grouped(rows_sorted, tile_experts, tile_rows, wg_bf, wu_bf, wd_bf, TM_GROUP)
        y_tok, attn_b = lax.optimization_barrier((_move_rows(y_sorted, pos), attn_half(slice(hb, ts))))
        xp = _moe_finish(x1, y_tok, pmod(5, TM_MOE), mod_p, ln2_g[l], ln2_b[l], alpha, TM_MOE)
        attn = jnp.concatenate([attn_a, attn_b], axis=0)
        pooled, new_pool = _pool_sample(state_pool[l], u, past_len)
        x1, h2, comb = _merge(pooled, gates, attn, xs, [smod(2), smod(4), smod(3)], [mod_s] * 3,
                              w_pool[l], pool_scale[l], w_out[l], ln1_g[l], ln1_b[l], w_r, b_r, alpha, ts)
        xs = _moe(h2, comb, wg_bf, wu_bf, wd_bf, x1, smod(5, True), mod_s,
                  ln2_g[l], ln2_b[l], alpha, ts, E_BLK)
        outs["ks"].append(jnp.transpose(kt_f.reshape(N_HEADS, 2, HEAD_DIM, db, dseq), (3, 4, 0, 1, 2)))
        outs["vs"].append(v_f.reshape(db, dseq, N_HEADS, V_DIM))
        outs["ps"].append(new_pool)

    return (xp.reshape(bsz, seq, d), xs.reshape(db, dseq, d),
            jnp.stack(outs["kp"]), jnp.stack(outs["vp"]), jnp.stack(outs["pp"]),
            jnp.stack(outs["ks"]), jnp.stack(outs["vs"]), jnp.stack(outs["ps"]))
```

```python
import functools
import math

import jax
import jax.numpy as jnp
from jax import lax
from jax.experimental import pallas as pl
from jax.experimental.pallas import tpu as pltpu
from jax.experimental.pallas import tpu_sc as plsc

F32 = jnp.float32
BF16 = jnp.bfloat16

LANES = 128
VMEM_LIMIT = 56 << 20

N_MOD = 6
N_HEADS = 8
HEAD_DIM = 64
V_DIM = 2 * HEAD_DIM
ROPE_THETA = 10000.0
POOL_WINDOWS = (2, 4, 8, 16)
POOL_GROUP_DIM = 128
POOL_BUF = max(POOL_WINDOWS) - 1
N_GROUPS = 4
EXPERTS_PER_GROUP = 4
N_EXPERTS = N_GROUPS * EXPERTS_PER_GROUP
TOP_K_IN_GROUP = 2
EXPERT_PAIRS = tuple((a, b) for a in range(EXPERTS_PER_GROUP) for b in range(a + 1, EXPERTS_PER_GROUP))
PAIRS_PER_GROUP = len(EXPERT_PAIRS)
N_CLASSES = N_GROUPS * PAIRS_PER_GROUP
LN_EPS = 1e-5
SUBLN_EPS = 1e-5
ROUTER_ROWS = 32
GATHER_WINDOW = 128

TM_PROJ = 256
TQ = 1024
TM_MERGE = 1024
TM_MOE = 1024
TM_GROUP = 512
E_BLK = EXPERTS_PER_GROUP

NT_DIMS = (((1,), (1,)), ((), ()))


def _params(*sem):
    return pltpu.CompilerParams(dimension_semantics=sem, vmem_limit_bytes=VMEM_LIMIT)


def _sigmoid(x):
    return 1.0 / (1.0 + jnp.exp(-x))


def _layer_norm(x, g, b):
    mu = jnp.mean(x, axis=-1, keepdims=True)
    xc = x - mu
    var = jnp.mean(xc * xc, axis=-1, keepdims=True)
    return xc * lax.rsqrt(var + LN_EPS) * g + b


def _lambda(lam_ref, lam_init):
    lp = lam_ref[...]
    a = jnp.sum(lp[0:1] * lp[1:2], axis=1, keepdims=True)
    b = jnp.sum(lp[2:3] * lp[3:4], axis=1, keepdims=True)
    return jnp.exp(a) - jnp.exp(b) + lam_init


def _ada_kernel(c_ref, w_ref, b_ref, o_ref):
    c = c_ref[...]
    o_ref[...] = jnp.dot(c * _sigmoid(c), w_ref[...], preferred_element_type=F32,
                         precision=lax.Precision.HIGHEST) + b_ref[...]


def _ada(c_all, w_ada, b_ada):
    n, d = c_all.shape
    cols = w_ada.shape[1]
    return pl.pallas_call(
        _ada_kernel,
        grid=(cols // d,),
        in_specs=[pl.BlockSpec((n, d), lambda j: (0, 0)),
                  pl.BlockSpec((d, d), lambda j: (0, j)),
                  pl.BlockSpec((1, d), lambda j: (0, j))],
        out_specs=pl.BlockSpec((n, d), lambda j: (0, j)),
        out_shape=jax.ShapeDtypeStruct((n, cols), F32),
        compiler_params=_params("arbitrary"),
        name="ada_mod",
    )(c_all, w_ada, b_ada.reshape(1, cols))


def _inproj_kernel(x_ref, sc_ref, sh_ref, w_ref, cost_ref, sint_ref, *out_refs, d, pw, prompt):
    mm_dtype = w_ref.dtype
    prec = lax.Precision.HIGHEST if mm_dtype == F32 else None
    dot = functools.partial(jnp.dot, preferred_element_type=F32, precision=prec)
    h =(x_ref[...] * (1.0 + sc_ref[...]) + sh_ref[...]).astype(mm_dtype)
    half = HEAD_DIM // 2
    scale = HEAD_DIM ** -0.5 * (math.log2(math.e) if prompt else 1.0)
    cost, sint = cost_ref[...], sint_ref[...]

    def rope_rows(xt):
        out = []
        for g in range(d // HEAD_DIM):
            lo = slice(g * HEAD_DIM, g * HEAD_DIM + half)
            hi = slice(g * HEAD_DIM + half, (g + 1) * HEAD_DIM)
            x1, x2 = xt[lo], xt[hi]
            out.append((lo, x1 * cost - x2 * sint))
            out.append((hi, x2 * cost + x1 * sint))
        return out

    qt = rope_rows(dot(h, w_ref[:, 0:d]).T)
    kt = rope_rows(dot(h, w_ref[:, d:2 * d]).T)
    v = dot(h, w_ref[:, 2 * d:3 * d])
    if prompt:
        qt_ref, ktf_ref, kb_ref, vf_ref, vtb_ref, u_ref, g_ref = out_refs
        for sl, r in qt:
            qt_ref[sl, :] = (r * scale).astype(BF16)
        for sl, r in kt:
            ktf_ref[sl, :] = r
        kb_ref[...] = ktf_ref[...].T.astype(BF16)
        vtb_ref[...] = v.T.astype(BF16)
        g_dtype = BF16
    else:
        g_dtype = F32
        qt_ref, ktf_ref, sn_ref, vf_ref, u_ref, g_ref = out_refs
        for idx, ((sl, rq), (_, rk)) in enumerate(zip(qt, kt)):
            qs = rq * scale
            qt_ref[sl, :] = qs
            ktf_ref[sl, :] = rk
            part = jnp.sum(qs * rk, axis=0, keepdims=True)
            g, is_hi = idx // 2, idx % 2
            head, which = g // 2, g % 2
            if is_hi:
                sn_ref[which, head:head + 1, :] += part
            else:
                sn_ref[which, head:head + 1, :] = part
    vf_ref[...] = v
    u_ref[...] = dot(h, w_ref[:, 3 * d:3 * d + pw])
    g_ref[...] = _sigmoid(dot(h, w_ref[:, 3 * d + pw:])).astype(g_dtype)


def _inproj(x, sc_spec, sh_spec, sc_arr, sh_arr, w_in, tabs_t, tab_t_spec, tm, n_seq, prompt):
    t, d = x.shape
    cols = w_in.shape[1]
    pw = cols - 5 * d
    per_seq = t // n_seq
    nblk = per_seq // tm
    row = lambda width: pl.BlockSpec((tm, width), lambda i: (i, 0))
    ft_spec = pl.BlockSpec((None, d, tm), lambda i: (i // nblk, 0, i % nblk))
    ft_shape = lambda dt: jax.ShapeDtypeStruct((n_seq, d, per_seq), dt)
    tok_shape = lambda width, dt: jax.ShapeDtypeStruct((t, width), dt)
    if prompt:
        out_specs = [ft_spec, ft_spec, row(d), row(d), ft_spec, row(pw), row(2 * d)]
        out_shape = [ft_shape(BF16), ft_shape(F32), tok_shape(d, BF16), tok_shape(d, F32), ft_shape(BF16),
                     tok_shape(pw, F32), tok_shape(2 * d, BF16)]
    else:
        assert n_seq == 1 and nblk == 1
        out_specs = [ft_spec, ft_spec, pl.BlockSpec((2, N_HEADS, tm), lambda i: (0, 0, 0)), row(d),
                     row(pw), row(2 * d)]
        out_shape = [ft_shape(F32), ft_shape(F32), jax.ShapeDtypeStruct((2, N_HEADS, t), F32),
                     tok_shape(d, F32), tok_shape(pw, F32), tok_shape(2 * d, F32)]
    return pl.pallas_call(
        functools.partial(_inproj_kernel, d=d, pw=pw, prompt=prompt),
        grid=(t // tm,),
        in_specs=[row(d), sc_spec, sh_spec,
                  pl.BlockSpec((d, cols), lambda i: (0, 0), pipeline_mode=pl.Buffered(1)),
                  tab_t_spec, tab_t_spec],
        out_specs=out_specs,
        out_shape=out_shape,
        compiler_params=_params("arbitrary"),
        name="in_proj",
    )(x, sc_arr, sh_arr, w_in, *tabs_t)


def _rope_tables(pos):
    half = HEAD_DIM // 2
    inv = ROPE_THETA ** (-jnp.arange(half, dtype=F32) / half)
    ang = inv[:, None] * pos.astype(F32)[None, :]
    return jnp.cos(ang), jnp.sin(ang)


def _attn_kernel(lam_ref, qt_ref, k_ref, vt_ref, g_ref, o_ref, q2_sc, sta_sc, stb_sc, m_sc, l_sc, acc_sc,
                 *, tq, lam_init):
    qi = pl.program_id(2)
    zero = jnp.zeros((HEAD_DIM, tq), BF16)
    q2_sc[0:HEAD_DIM, 0:tq] = qt_ref[0:HEAD_DIM, :]
    q2_sc[HEAD_DIM:V_DIM, 0:tq] = zero
    q2_sc[0:HEAD_DIM, tq:2 * tq] = zero
    q2_sc[HEAD_DIM:V_DIM, tq:2 * tq] = qt_ref[HEAD_DIM:V_DIM, :]
    m_sc[...] = jnp.full(m_sc.shape, -jnp.inf, F32)
    l_sc[...] = jnp.zeros(l_sc.shape, F32)
    acc_sc[...] = jnp.zeros(acc_sc.shape, F32)

    def scores(ki, st_ref):
        start = pl.multiple_of(ki * tq, tq)
        st_ref[...] = jnp.dot(k_ref[pl.ds(start, tq), :], q2_sc[...], preferred_element_type=F32)

    def consume(ki, st_ref, masked):
        st = st_ref[...]
        if masked:
            key = lax.broadcasted_iota(jnp.int32, st.shape, 0)
            qry = lax.broadcasted_iota(jnp.int32, st.shape, 1)
            qry = jnp.where(qry >= tq, qry - tq, qry)
            st = jnp.where(key <= qry, st, -jnp.inf)
        m_old = m_sc[...]
        m_new = jnp.maximum(m_old, jnp.max(st, axis=0, keepdims=True))
        alpha = jnp.exp2(m_old - m_new)
        p = jnp.exp2(st - m_new).astype(BF16)
        start = pl.multiple_of(ki * tq, tq)
        vt = vt_ref[:, pl.ds(start, tq)]
        vt1 = jnp.concatenate([vt, jnp.ones((16, tq), BF16)], axis=0)
        r = jnp.dot(vt1, p, preferred_element_type=F32)
        l_sc[...] = alpha * l_sc[...] + r[V_DIM:V_DIM + 1]
        acc_sc[...] = alpha * acc_sc[...] + r[0:V_DIM]
        m_sc[...] = m_new

    scores(0, sta_sc)

    def pair(pi, carry):
        k0 = 2 * pi
        scores(k0 + 1, stb_sc)
        consume(k0, sta_sc, False)
        scores(k0 + 2, sta_sc)
        consume(k0 + 1, stb_sc, False)
        return carry

    lax.fori_loop(0, qi // 2, pair, 0)

    @pl.when(qi % 2 == 0)
    def _():
        consume(qi, sta_sc, True)

    @pl.when(qi % 2 == 1)
    def _():
        scores(qi, stb_sc)
        consume(qi - 1, sta_sc, False)
        consume(qi, stb_sc, True)

    lam = _lambda(lam_ref, lam_init)
    inv_l = 1.0 / l_sc[...]
    ot = acc_sc[:, 0:tq] * inv_l[:, 0:tq] - lam * (acc_sc[:, tq:2 * tq] * inv_l[:, tq:2 * tq])
    ms = jnp.mean(ot * ot, axis=0, keepdims=True)
    ot = ot * lax.rsqrt(ms + SUBLN_EPS) * g_ref[...] * (1.0 - lam_init)
    o_ref[...] = ot.T.astype(o_ref.dtype)


def _attn_prompt(qt_bf, k_bf, vt_bf, lam_p, subln_g, lam_init, tq):
    b, d, s = qt_bf.shape
    nh = d // V_DIM
    return pl.pallas_call(
        functools.partial(_attn_kernel, tq=tq, lam_init=lam_init),
        grid=(b, nh, s // tq),
        in_specs=[pl.BlockSpec(lam_p.shape, lambda bi, h, qi: (0, 0)),
                  pl.BlockSpec((None, V_DIM, tq), lambda bi, h, qi: (bi, h, qi)),
                  pl.BlockSpec((None, s, V_DIM), lambda bi, h, qi: (bi, 0, h)),
                  pl.BlockSpec((None, V_DIM, s), lambda bi, h, qi: (bi, h, 0)),
                  pl.BlockSpec((V_DIM, 1), lambda bi, h, qi: (0, 0))],
        out_specs=pl.BlockSpec((None, tq, V_DIM), lambda bi, h, qi: (bi, qi, h)),
        out_shape=jax.ShapeDtypeStruct((b, s, d), BF16),
        scratch_shapes=[pltpu.VMEM((V_DIM, 2 * tq), BF16),
                        pltpu.VMEM((tq, 2 * tq), F32),
                        pltpu.VMEM((tq, 2 * tq), F32),
                        pltpu.VMEM((1, 2 * tq), F32),
                        pltpu.VMEM((1, 2 * tq), F32),
                        pltpu.VMEM((V_DIM, 2 * tq), F32)],
        compiler_params=_params("arbitrary", "arbitrary", "arbitrary"),
        name="attn_prompt",
    )(lam_p, qt_bf, k_bf, vt_bf, subln_g.reshape(V_DIM, 1))


def _attn_sample_kernel(pt_ref, lam_ref, qt_ref, sn_ref, vn_ref, g_ref, *refs, n_pages, lam_init):
    del pt_ref
    kt_refs, v_refs = refs[:n_pages], refs[n_pages:2 * n_pages]
    o_ref, s_sc = refs[2 * n_pages:]
    b = pl.program_id(0)
    d, n_tok = qt_ref.shape
    page = kt_refs[0].shape[1]

    onehot = jnp.where(lax.broadcasted_iota(jnp.int32, (n_tok, page), 0) == b, 1.0, 0.0)
    qb = jnp.dot(qt_ref[...], onehot, preferred_element_type=F32,
                 precision=lax.Precision.HIGHEST)
    for p in range(n_pages):
        prod = (kt_refs[p][...] * qb).reshape(N_HEADS, V_DIM, page)
        s_sc[0, :, p * page:(p + 1) * page] = jnp.sum(prod[:, 0:HEAD_DIM, :], axis=1)
        s_sc[1, :, p * page:(p + 1) * page] = jnp.sum(prod[:, HEAD_DIM:V_DIM, :], axis=1)

    own = lax.broadcasted_iota(jnp.int32, (N_HEADS, n_tok), 1) == b

    def softmax_parts(which):
        s_new = jnp.max(jnp.where(own, sn_ref[which], -jnp.inf), axis=1, keepdims=True)
        s = s_sc[which]
        m = jnp.maximum(jnp.max(s, axis=1, keepdims=True), s_new)
        p = jnp.exp(s - m)
        p_new = jnp.exp(s_new - m)
        inv_l = 1.0 / (jnp.sum(p, axis=1, keepdims=True) + p_new)
        return p * inv_l, p_new * inv_l

    lam = _lambda(lam_ref, lam_init)
    w1, wn1 = softmax_parts(0)
    w2, wn2 = softmax_parts(1)
    w = w1 - lam * w2
    acc = (wn1 - lam * wn2) * vn_ref[...]
    for p in range(n_pages):
        for t in range(page):
            c = p * page + t
            acc = acc + w[:, c:c + 1] * v_refs[p][t]
    ms = jnp.mean(acc * acc, axis=1, keepdims=True)
    o_ref[...] = acc * lax.rsqrt(ms + SUBLN_EPS) * g_ref[...] * (1.0 - lam_init)


def _attn_sample(qt, s_new, v_new, cache_k, cache_v, page_table, lam_p, subln_g, lam_init):
    d, db = qt.shape
    n_pages = page_table.shape[1]
    n_phys, page = cache_k.shape[0], cache_k.shape[1]
    ck = jnp.transpose(cache_k, (0, 2, 3, 4, 1)).reshape(n_phys, d, page)
    pt = page_table.reshape(-1)

    def page_spec(p, shape):
        zeros = (0,) * len(shape)
        return pl.BlockSpec((None,) + shape, lambda b, pt_ref: (pt_ref[b * n_pages + p],) + zeros)

    const = lambda shape: pl.BlockSpec(shape, lambda b, pt_ref: (0,) * len(shape))
    head_spec = pl.BlockSpec((None, N_HEADS, V_DIM), lambda b, pt_ref: (b, 0, 0))
    grid_spec = pltpu.PrefetchScalarGridSpec(
        num_scalar_prefetch=1,
        grid=(db,),
        in_specs=[const(lam_p.shape), const((d, db)), const((2, N_HEADS, db)), head_spec, const((1, V_DIM))]
                 + [page_spec(p, (d, page)) for p in range(n_pages)]
                 + [page_spec(p, (page, N_HEADS, V_DIM)) for p in range(n_pages)],
        out_specs=head_spec,
        scratch_shapes=[pltpu.VMEM((2, N_HEADS, n_pages * page), F32)],
    )
    out = pl.pallas_call(
        functools.partial(_attn_sample_kernel, n_pages=n_pages, lam_init=lam_init),
        grid_spec=grid_spec,
        out_shape=jax.ShapeDtypeStruct((db, N_HEADS, V_DIM), F32),
        compiler_params=_params("arbitrary"),
        name="attn_sample",
    )(pt, lam_p, qt, s_new, v_new.reshape(db, N_HEADS, V_DIM), subln_g.reshape(1, V_DIM),
      *([ck] * n_pages), *([cache_v] * n_pages))
    return out.reshape(db, d)


POOL_HALO = 16


def _pool_tile(u_ref, halo_ref, ext_sc, seq_tile):
    tm = u_ref.shape[0]
    hb = halo_ref.shape[0]

    @pl.when(seq_tile == 0)
    def _():
        ext_sc[0:hb, :] = jnp.zeros((hb, ext_sc.shape[1]), F32)

    @pl.when(seq_tile > 0)
    def _():
        ext_sc[0:hb, :] = halo_ref[...]

    ext_sc[hb:hb + tm, :] = u_ref[...]
    pos = seq_tile * tm + lax.broadcasted_iota(jnp.int32, (tm, POOL_GROUP_DIM), 0)
    parts = []
    for g, win in enumerate(POOL_WINDOWS):
        sl = slice(g * POOL_GROUP_DIM, (g + 1) * POOL_GROUP_DIM)
        cur = ext_sc[hb:hb + tm, sl]
        tot = cur
        for j in range(1, win):
            tot = tot + ext_sc[hb - j:hb - j + tm, sl]
        cnt = jnp.minimum(win, pos + 1).astype(F32)
        parts.append(tot / cnt - cur)
    return jnp.concatenate(parts, axis=1)


def _pool_sample_kernel(st_ref, u_ref, o_ref, np_ref, *, start_pos):
    nb = st_ref.shape[0]
    u = u_ref[...]
    for g, win in enumerate(POOL_WINDOWS):
        sl = slice(g * POOL_GROUP_DIM, (g + 1) * POOL_GROUP_DIM)
        cur = u[:, sl]
        tot = cur
        for j in range(1, win):
            tot = tot + st_ref[nb - j, :, sl]
        o_ref[:, sl] = tot / float(min(win, start_pos + 1)) - cur
    for j in range(nb - 1):
        np_ref[j] = st_ref[j + 1]
    np_ref[nb - 1] = u


def _pool_sample(state, u, start_pos):
    db, nb, pw = state.shape
    full = lambda shape: pl.BlockSpec(shape, lambda i: (0,) * len(shape))
    pooled, new_pool = pl.pallas_call(
        functools.partial(_pool_sample_kernel, start_pos=start_pos),
        grid=(1,),
        in_specs=[full((nb, db, pw)), full((db, pw))],
        out_specs=[full((db, pw)), full((nb, db, pw))],
        out_shape=[jax.ShapeDtypeStruct((db, pw), F32), jax.ShapeDtypeStruct((nb, db, pw), F32)],
        compiler_params=_params("arbitrary"),
        name="pool_sample",
    )(jnp.transpose(state, (1, 0, 2)), u)
    return pooled, jnp.transpose(new_pool, (1, 0, 2))


def _route(lt):
    g = [lt[j:j + 1] for j in range(N_GROUPS)]
    gmax = functools.reduce(jnp.maximum, g)
    gidx = jnp.full(gmax.shape, N_GROUPS - 1, jnp.int32)
    for j in range(N_GROUPS - 2, -1, -1):
        gidx = jnp.where(g[j] == gmax, j, gidx)
    p_g = 1.0 / functools.reduce(jnp.add, [jnp.exp(gj - gmax) for gj in g])
    e = []
    for k in range(EXPERTS_PER_GROUP):
        sel = lt[N_GROUPS + k:N_GROUPS + k + 1]
        for j in range(1, N_GROUPS):
            r = N_GROUPS + j * EXPERTS_PER_GROUP + k
            sel = jnp.where(gidx == j, lt[r:r + 1], sel)
        e.append(sel)

    def first_argmax(vals):
        vmax = functools.reduce(jnp.maximum, vals)
        idx = jnp.full(vmax.shape, len(vals) - 1, jnp.int32)
        for k in range(len(vals) - 2, -1, -1):
            idx = jnp.where(vals[k] == vmax, k, idx)
        return vmax, idx

    v1, i1 = first_argmax(e)
    rest = [jnp.where(i1 == k, -jnp.inf, e[k]) for k in range(EXPERTS_PER_GROUP)]
    v2, i2 = first_argmax(rest)
    t = jnp.exp(v2 - v1)
    w1 = p_g / (1.0 + t)
    w2 = p_g * t / (1.0 + t)
    within = [jnp.where(i1 == k, w1, 0.0) + jnp.where(i2 == k, w2, 0.0) for k in range(EXPERTS_PER_GROUP)]
    rows = [jnp.where(gidx == j, within[k], 0.0) for j in range(N_GROUPS) for k in range(EXPERTS_PER_GROUP)]
    a, b = jnp.minimum(i1, i2), jnp.maximum(i1, i2)
    pair = jnp.where(a == 0, b - 1, jnp.where(a == 1, b + 1, 5))
    return jnp.concatenate(rows, axis=0), gidx * PAIRS_PER_GROUP + pair


def _pack_bf16_pair(a, b):
    a_bits = lax.bitcast_convert_type(a.astype(BF16).astype(F32), jnp.uint32)
    b_bits = lax.bitcast_convert_type(b.astype(BF16).astype(F32), jnp.uint32)
    return (a_bits >> 16) | (b_bits & jnp.uint32(0xFFFF0000))


def _unpack_bf16_pair(w):
    lo = lax.bitcast_convert_type(w << 16, F32)
    hi = lax.bitcast_convert_type(w & jnp.uint32(0xFFFF0000), F32)
    return lo, hi


def _merge_kernel(*refs, alpha, tiles_per_seq):
    if tiles_per_seq:
        u_ref, halo_ref, *refs, ext_sc = refs
        pooled = _pool_tile(u_ref, halo_ref, ext_sc, pl.program_id(0) % tiles_per_seq)
    else:
        pooled_ref, *refs = refs
        pooled = pooled_ref[...]
    (gates_ref, attn_ref, x_ref, g1_ref, sc2_ref, sh2_ref, wpool_ref, pscale_ref, wout_ref, lng_ref, lnb_ref,
     wr_ref, br_ref, x1_ref, *route_refs) = refs
    d = x_ref.shape[1]
    tm = x_ref.shape[0]
    mm_dtype = wout_ref.dtype
    dot = functools.partial(jnp.dot, preferred_element_type=F32,
                            precision=lax.Precision.HIGHEST if mm_dtype == F32 else None)
    pooled = pooled.astype(mm_dtype)
    parts = [dot(pooled[:, g * POOL_GROUP_DIM:(g + 1) * POOL_GROUP_DIM], wpool_ref[g])
             for g in range(len(POOL_WINDOWS))]
    pool_out = jnp.concatenate(parts, axis=1) * pscale_ref[...]
    g_dtype = gates_ref.dtype
    merged = (gates_ref[:, 0:d] * pool_out.astype(g_dtype)
              + gates_ref[:, d:2 * d] * attn_ref[...].astype(g_dtype))
    y = dot(merged.astype(mm_dtype), wout_ref[...])
    x1 = _layer_norm(alpha * x_ref[...] + (1.0 + g1_ref[...]) * y, lng_ref[...], lnb_ref[...])
    x1_ref[...] = x1
    h2 = x1 * (1.0 + sc2_ref[...]) + sh2_ref[...]
    lt = lax.dot_general(wr_ref[...], h2, NT_DIMS, preferred_element_type=F32,
                         precision=lax.Precision.HIGHEST) + br_ref[...]
    comb_t, cls = _route(lt)
    comb = jnp.concatenate([comb_t, jnp.zeros((LANES - N_EXPERTS, tm), F32)], axis=0).T
    if tiles_per_seq:
        row_ref, cls_ref = route_refs
        row_ref[:, 0:d // 2] = _pack_bf16_pair(h2[:, 0:d // 2], h2[:, d // 2:d])
        row_ref[:, d // 2:d // 2 + LANES] = lax.bitcast_convert_type(comb, jnp.uint32)
        cls_ref[...] = cls
    else:
        h2_ref, comb_ref = route_refs
        h2_ref[...] = h2.astype(BF16)
        comb_ref[...] = comb


def _merge(pool_in, gates, attn, x, mod_specs, mod_arrs, w_pool, pool_scale, w_out,
           ln_g, ln_b, w_r, b_r, alpha, tm, seq_len=None):
    t, d = x.shape
    pw = pool_in.shape[1]
    row = lambda width: pl.BlockSpec((tm, width), lambda i: (i, 0))
    const = lambda shape: pl.BlockSpec(shape, lambda i: (0,) * len(shape))
    if seq_len:
        halo_spec = pl.BlockSpec((POOL_HALO, pw), lambda i: (jnp.maximum(i * (tm // POOL_HALO) - 1, 0), 0))
        pool_specs, pool_args = [row(pw), halo_spec], [pool_in, pool_in]
        scratch = [pltpu.VMEM((POOL_HALO + tm, pw), F32)]
    else:
        pool_specs, pool_args, scratch = [row(pw)], [pool_in], []
    if seq_len:
        route_specs = [row(d // 2 + LANES), pl.BlockSpec((None, 1, tm), lambda i: (i, 0, 0))]
        route_shapes = [jax.ShapeDtypeStruct((t, d // 2 + LANES), jnp.uint32),
                        jax.ShapeDtypeStruct((t // tm, 1, tm), jnp.int32)]
    else:
        route_specs = [row(d), row(LANES)]
        route_shapes = [jax.ShapeDtypeStruct((t, d), BF16), jax.ShapeDtypeStruct((t, LANES), F32)]
    return pl.pallas_call(
        functools.partial(_merge_kernel, alpha=alpha, tiles_per_seq=seq_len // tm if seq_len else None),
        grid=(t // tm,),
        in_specs=[*pool_specs, row(2 * d), row(d), row(d), *mod_specs,
                  const(w_pool.shape), const((1, d)), const((d, d)), const((1, d)), const((1, d)),
                  const(w_r.shape), const(b_r.shape)],
        out_specs=[row(d), *route_specs],
        out_shape=[jax.ShapeDtypeStruct((t, d), F32), *route_shapes],
        scratch_shapes=scratch,
        compiler_params=_params("arbitrary"),
        name="merge",
    )(*pool_args, gates, attn, x, *mod_arrs, w_pool, pool_scale.reshape(1, d), w_out,
      ln_g.reshape(1, d), ln_b.reshape(1, d), w_r, b_r)


def _moe_kernel(h2_ref, comb_ref, wg_ref, wu_ref, wd_ref, x1_ref, g2_ref, lng_ref, lnb_ref,
                o_ref, acc_sc, *, alpha):
    j = pl.program_id(1)
    eb = wg_ref.shape[0]

    @pl.when(j == 0)
    def _():
        acc_sc[...] = jnp.zeros(acc_sc.shape, F32)

    h2 = h2_ref[...]
    comb = comb_ref[...]
    lane = lax.broadcasted_iota(jnp.int32, comb.shape, 1)
    for e in range(eb):
        hg = jnp.dot(h2, wg_ref[e].astype(BF16), preferred_element_type=F32)
        hu = jnp.dot(h2, wu_ref[e].astype(BF16), preferred_element_type=F32)
        c = jnp.sum(jnp.where(lane == j * eb + e, comb, 0.0), axis=1, keepdims=True)
        act = (hg * _sigmoid(hg)) * hu * c
        acc_sc[...] += jnp.dot(act.astype(BF16), wd_ref[e].astype(BF16), preferred_element_type=F32)

    @pl.when(j == pl.num_programs(1) - 1)
    def _():
        y = alpha * x1_ref[...] + (1.0 + g2_ref[...]) * acc_sc[...]
        o_ref[...] = _layer_norm(y, lng_ref[...], lnb_ref[...])


def _moe(h2, comb, w_gate, w_up, w_down, x1, g2_spec, g2_arr, ln_g, ln_b, alpha, tm, eb):
    t, d = x1.shape
    ne, _, ff = w_gate.shape
    row = lambda width: pl.BlockSpec((tm, width), lambda i, j: (i, 0))
    const = lambda shape: pl.BlockSpec(shape, lambda i, j: (0,) * len(shape))
    return pl.pallas_call(
        functools.partial(_moe_kernel, alpha=alpha),
        grid=(t // tm, ne // eb),
        in_specs=[row(d), row(LANES),
                  pl.BlockSpec((eb, d, ff), lambda i, j: (j, 0, 0)),
                  pl.BlockSpec((eb, d, ff), lambda i, j: (j, 0, 0)),
                  pl.BlockSpec((eb, ff, d), lambda i, j: (j, 0, 0)),
                  row(d), g2_spec, const((1, d)), const((1, d))],
        out_specs=row(d),
        out_shape=jax.ShapeDtypeStruct((t, d), F32),
        scratch_shapes=[pltpu.VMEM((tm, d), F32)],
        compiler_params=_params("arbitrary", "arbitrary"),
        name="moe_ffn",
    )(h2, comb, w_gate, w_up, w_down, x1, g2_arr, ln_g.reshape(1, d), ln_b.reshape(1, d))


def _move_rows(x, idx, n_out=None):
    n, width = idx.shape[0], x.shape[1]
    scatter = n_out is not None
    info = pltpu.get_tpu_info().sparse_core
    units = info.num_cores * info.num_subcores
    per_unit = n // (units * GATHER_WINDOW)
    assert per_unit * units * GATHER_WINDOW == n

    @pl.kernel(out_type=jax.ShapeDtypeStruct((n_out if scatter else n, width), x.dtype),
               mesh=plsc.VectorSubcoreMesh(core_axis_name="core", subcore_axis_name="subcore"),
               scratch_types=[pltpu.VMEM((1, GATHER_WINDOW), jnp.int32), pltpu.VMEM((GATHER_WINDOW, width), x.dtype)])
    def move_kernel(x_hbm, i_hbm, o_hbm, i_vmem, buf):
        unit = lax.axis_index("core") * info.num_subcores + lax.axis_index("subcore")

        @pl.loop(0, per_unit)
        def _(j):
            base = (unit * per_unit + j) * GATHER_WINDOW
            pltpu.sync_copy(i_hbm.at[:, pl.ds(base, GATHER_WINDOW)], i_vmem)
            if scatter:
                pltpu.sync_copy(x_hbm.at[pl.ds(base, GATHER_WINDOW), :], buf)
                pltpu.sync_copy(buf, o_hbm.at[i_vmem.at[0]])
            else:
                pltpu.sync_copy(x_hbm.at[i_vmem.at[0]], buf)
                pltpu.sync_copy(buf, o_hbm.at[pl.ds(base, GATHER_WINDOW), :])

    return move_kernel(x, idx.reshape(1, n))


def _dispatch_plan(cls, tile, n_slots):
    onehot = (cls[:, None] == jnp.arange(N_CLASSES, dtype=jnp.int32)[None, :]).astype(jnp.int32)
    rank = jnp.cumsum(onehot, axis=0) - onehot
    counts = jnp.sum(onehot, axis=0)
    padded = -(-counts // tile) * tile
    ends = jnp.cumsum(padded)
    starts = ends - padded
    pos = jnp.sum(onehot * (starts[None, :] + rank), axis=1)
    tile_start = jnp.arange(n_slots // tile, dtype=jnp.int32) * tile
    tile_class = jnp.minimum(jnp.sum((tile_start[:, None] >= ends[None, :]).astype(jnp.int32), axis=1),
                             N_CLASSES - 1)
    tile_rows = jnp.clip((starts + counts)[tile_class] - tile_start, 0, tile)
    return pos, tile_class, tile_rows


def _moe_grouped_kernel(te_ref, tr_ref, row_ref, *refs):
    *w_refs, y_ref = refs
    n_rows = tr_ref[pl.program_id(0)]
    half = y_ref.shape[1]

    @pl.when(n_rows == 0)
    def _():
        y_ref[...] = jnp.zeros(y_ref.shape, y_ref.dtype)

    @pl.when(n_rows > 0)
    def _():
        live = lax.broadcasted_iota(jnp.int32, (row_ref.shape[0], 1), 0) < n_rows
        lo, hi = _unpack_bf16_pair(row_ref[:, 0:half])
        h2 = jnp.where(live, jnp.concatenate([lo, hi], axis=1), 0.0).astype(BF16)
        comb = jnp.where(live, lax.bitcast_convert_type(row_ref[:, half:half + LANES], F32), 0.0)
        lane = lax.broadcasted_iota(jnp.int32, comb.shape, 1)
        acc = jnp.zeros((h2.shape[0], 2 * half), F32)
        for k in range(TOP_K_IN_GROUP):
            wg_ref, wu_ref, wd_ref = w_refs[3 * k:3 * k + 3]
            expert = te_ref[k, pl.program_id(0)]
            hg = jnp.dot(h2, wg_ref[...].astype(BF16), preferred_element_type=F32)
            hu = jnp.dot(h2, wu_ref[...].astype(BF16), preferred_element_type=F32)
            c = jnp.sum(jnp.where(lane == expert, comb, 0.0), axis=1, keepdims=True)
            act = (hg * _sigmoid(hg)) * hu * c
            acc = acc + jnp.dot(act.astype(BF16), wd_ref[...].astype(BF16), preferred_element_type=F32)
        y_ref[...] = _pack_bf16_pair(acc[:, 0:half], acc[:, half:2 * half])


def _moe_grouped(rows_sorted, tile_experts, tile_rows, w_gate, w_up, w_down, tile):
    n_slots, width = rows_sorted.shape
    ne, d, ff = w_gate.shape
    expert_w = lambda k, shape: pl.BlockSpec((None,) + shape, lambda i, te, tr: (te[k, i], 0, 0))
    w_specs, w_args = [], []
    for k in range(TOP_K_IN_GROUP):
        w_specs += [expert_w(k, (d, ff)), expert_w(k, (d, ff)), expert_w(k, (ff, d))]
        w_args += [w_gate, w_up, w_down]
    grid_spec = pltpu.PrefetchScalarGridSpec(
        num_scalar_prefetch=2,
        grid=(n_slots // tile,),
        in_specs=[pl.BlockSpec((tile, width), lambda i, te, tr: (i, 0)), *w_specs],
        out_specs=pl.BlockSpec((tile, d // 2), lambda i, te, tr: (i, 0)),
    )
    return pl.pallas_call(
        _moe_grouped_kernel,
        grid_spec=grid_spec,
        out_shape=jax.ShapeDtypeStruct((n_slots, d // 2), jnp.uint32),
        compiler_params=_params("arbitrary"),
        name="moe_grouped",
    )(tile_experts, tile_rows, rows_sorted, *w_args)


def _moe_finish_kernel(x1_ref, y_ref, g2_ref, lng_ref, lnb_ref, o_ref, *, alpha):
    lo, hi = _unpack_bf16_pair(y_ref[...])
    y = jnp.concatenate([lo, hi], axis=1)
    o_ref[...] = _layer_norm(alpha * x1_ref[...] + (1.0 + g2_ref[...]) * y, lng_ref[...], lnb_ref[...])


def _moe_finish(x1, y_packed, g2_spec, g2_arr, ln_g, ln_b, alpha, tm):
    t, d = x1.shape
    row = lambda width: pl.BlockSpec((tm, width), lambda i: (i, 0))
    const = lambda shape: pl.BlockSpec(shape, lambda i: (0,) * len(shape))
    return pl.pallas_call(
        functools.partial(_moe_finish_kernel, alpha=alpha),
        grid=(t // tm,),
        in_specs=[row(d), row(d // 2), g2_spec, const((1, d)), const((1, d))],
        out_specs=row(d),
        out_shape=jax.ShapeDtypeStruct((t, d), F32),
        compiler_params=_params("arbitrary"),
        name="moe_finish",
    )(x1, y_packed, g2_arr, ln_g.reshape(1, d), ln_b.reshape(1, d))


def kernel(x_prompt, x_sample, cache_k, cache_v, state_pool, page_table, c_prompt, c_sample, w_ada, b_ada, w_in, lambda_q1, lambda_k1, lambda_q2, lambda_k2, subln_g, w_pool, pool_scale, w_out, ln1_g, ln1_b, w_group, b_group, w_router, b_router, w_e_gate, w_e_up, w_e_down, ln2_g, ln2_b):
    depth = w_in.shape[0]
    bsz, seq, d = x_prompt.shape
    db, dseq, _ = x_sample.shape
    assert dseq == 1, "sample group decodes one token per sequence"
    n_pages, page = page_table.shape[1], cache_k.shape[2]
    past_len = n_pages * page
    alpha = (2.0 * depth) ** 0.25
    tp, ts = bsz * seq, db * dseq

    sc_info = pltpu.get_tpu_info().sparse_core
    sc_units = sc_info.num_cores * sc_info.num_subcores
    assert tp % (GATHER_WINDOW * sc_units) == 0

    xp = x_prompt.reshape(tp, d)
    xs = x_sample.reshape(ts, d)
    n_c = bsz + db
    n_c_pad = -(-n_c // 8) * 8
    c_all = jnp.concatenate([c_prompt, c_sample, jnp.zeros((n_c_pad - n_c, d), F32)], axis=0)
    tabs_p = _rope_tables(jnp.arange(seq))
    tabs_s = _rope_tables(jnp.full((ts,), past_len))

    outs = {k: [] for k in ("kp", "vp", "pp", "ks", "vs", "ps")}
    for l in range(depth):
        lam_init = 0.8 - 0.6 * math.exp(-0.3 * l)
        mod = _ada(c_all, w_ada[l], b_ada[l])
        mod_p = mod[:bsz].reshape(bsz * N_MOD, 1, d)
        mod_s = mod[bsz:n_c]
        w_in_bf = w_in[l].astype(BF16)
        w_pool_bf = w_pool[l].astype(BF16)
        w_out_bf = w_out[l].astype(BF16)
        w_gate, w_up, w_down = w_e_gate[l], w_e_up[l], w_e_down[l]
        lam_p = jnp.stack([lambda_q1[l], lambda_k1[l], lambda_q2[l], lambda_k2[l]])
        n_r = N_GROUPS + N_EXPERTS
        w_r = jnp.concatenate([w_group[l], w_router[l]], axis=1).T
        w_r = jnp.concatenate([w_r, jnp.zeros((ROUTER_ROWS - n_r, d), F32)], axis=0)
        b_r = jnp.concatenate([b_group[l], b_router[l], jnp.zeros((ROUTER_ROWS - n_r,), F32)]).reshape(ROUTER_ROWS, 1)

        def pmod(k, tm, two_d=False):
            per_b = seq // tm
            if two_d:
                return pl.BlockSpec((None, 1, d), lambda i, j: ((i // per_b) * N_MOD + k, 0, 0))
            return pl.BlockSpec((None, 1, d), lambda i: ((i // per_b) * N_MOD + k, 0, 0))

        nblk = seq // TM_PROJ
        tab_t_spec = pl.BlockSpec((HEAD_DIM // 2, TM_PROJ), lambda i: (0, i % nblk))
        qt_bf, kt_f, k_bf, v_f, vt_bf, u, gates = _inproj(
            xp, pmod(1, TM_PROJ), pmod(0, TM_PROJ), mod_p, mod_p, w_in_bf,
            tabs_p, tab_t_spec, TM_PROJ, bsz, True)
        attn = _attn_prompt(qt_bf, k_bf.reshape(bsz, seq, d), vt_bf,
                            lam_p, subln_g[l], lam_init, TQ).reshape(tp, d)
        x1, ffn_rows, cls = _merge(u, gates, attn, xp,
                                   [pmod(2, TM_MERGE), pmod(4, TM_MERGE), pmod(3, TM_MERGE)], [mod_p] * 3,
                                   w_pool_bf, pool_scale[l], w_out_bf, ln1_g[l], ln1_b[l], w_r, b_r, alpha,
                                   TM_MERGE, seq_len=seq)
        slot_quantum = GATHER_WINDOW * sc_units
        n_slots = -(-(tp + N_CLASSES * (TM_GROUP - 1)) // slot_quantum) * slot_quantum
        pos, tile_class, tile_rows = _dispatch_plan(cls.reshape(tp), TM_GROUP, n_slots)
        class_experts = jnp.array([[g * EXPERTS_PER_GROUP + pair[k] for g in range(N_GROUPS) for pair in EXPERT_PAIRS]
                                   for k in range(TOP_K_IN_GROUP)], jnp.int32)
        tile_experts = class_experts[:, tile_class]
        rows_sorted = _move_rows(ffn_rows, pos, n_slots)
        outs["kp"].append(jnp.transpose(kt_f.reshape(bsz, N_HEADS, 2, HEAD_DIM, seq), (0, 4, 1, 2, 3)))
        outs["vp"].append(v_f.reshape(bsz, seq, N_HEADS, V_DIM))
        outs["pp"].append(u.reshape(bsz, seq, -1)[:, seq - POOL_BUF:])

        def smod(k, two_d=False):
            if two_d:
                return pl.BlockSpec((ts, d), lambda i, j: (0, k))
            return pl.BlockSpec((ts, d), lambda i: (0, k))

        tab_t_spec_s = pl.BlockSpec((HEAD_DIM // 2, ts), lambda i: (0, 0))
        qt_s, kt_f, s_new, v_f, u, gates = _inproj(
            xs, smod(1), smod(0), mod_s, mod_s, w_in[l],
            tabs_s, tab_t_spec_s, ts, 1, False)
        hb = ts // 2
        attn_half = lambda sl: _attn_sample(qt_s[0][:, sl], s_new[:, :, sl], v_f[sl], cache_k[l], cache_v[l],
                                            page_table[sl], lam_p, subln_g[l], lam_init)
        rows_sorted, attn_a = lax.optimization_barrier((rows_sorted, attn_half(slice(0, hb))))
        y_sorted = _moe_grouped(rows_sorted, tile_experts, tile_rows, w_gate, w_up, w_down, TM_GROUP)
        y_tok, attn_b = lax.optimization_barrier((_move_rows(y_sorted, pos), attn_half(slice(hb, ts))))
        xp = _moe_finish(x1, y_tok, pmod(5, TM_MOE), mod_p, ln2_g[l], ln2_b[l], alpha, TM_MOE)
        attn = jnp.concatenate([attn_a, attn_b], axis=0)
        pooled, new_pool = _pool_sample(state_pool[l], u, past_len)
        x1, h2, comb = _merge(pooled, gates, attn, xs, [smod(2), smod(4), smod(3)], [mod_s] * 3,
                              w_pool[l], pool_scale[l], w_out[l], ln1_g[l], ln1_b[l], w_r, b_r, alpha, ts)
        xs = _moe(h2, comb, w_gate, w_up, w_down, x1, smod(5, True), mod_s,
                  ln2_g[l], ln2_b[l], alpha, ts, E_BLK)
        outs["ks"].append(jnp.transpose(kt_f.reshape(N_HEADS, 2, HEAD_DIM, db, dseq), (3, 4, 0, 1, 2)))
        outs["vs"].append(v_f.reshape(db, dseq, N_HEADS, V_DIM))
        outs["ps"].append(new_pool)

    return (xp.reshape(bsz, seq, d), xs.reshape(db, dseq, d),
            jnp.stack(outs["kp"]), jnp.stack(outs["vp"]), jnp.stack(outs["pp"]),
            jnp.stack(outs["ks"]), jnp.stack(outs["vs"]), jnp.stack(outs["ps"]))
```

```python
import functools
import math

import jax
import jax.numpy as jnp
from jax import lax
from jax.experimental import pallas as pl
from jax.experimental.pallas import tpu as pltpu
from jax.experimental.pallas import tpu_sc as plsc

F32 = jnp.float32
BF16 = jnp.bfloat16

LANES = 128
VMEM_LIMIT = 56 << 20

N_MOD = 6
N_HEADS = 8
HEAD_DIM = 64
V_DIM = 2 * HEAD_DIM
ROPE_THETA = 10000.0
POOL_WINDOWS = (2, 4, 8, 16)
POOL_GROUP_DIM = 128
POOL_BUF = max(POOL_WINDOWS) - 1
N_GROUPS = 4
EXPERTS_PER_GROUP = 4
N_EXPERTS = N_GROUPS * EXPERTS_PER_GROUP
TOP_K_IN_GROUP = 2
EXPERT_PAIRS = tuple((a, b) for a in range(EXPERTS_PER_GROUP) for b in range(a + 1, EXPERTS_PER_GROUP))
PAIRS_PER_GROUP = len(EXPERT_PAIRS)
N_CLASSES = N_GROUPS * PAIRS_PER_GROUP
LN_EPS = 1e-5
SUBLN_EPS = 1e-5
ROUTER_ROWS = 32
GATHER_WINDOW = 128

TM_PROJ = 256
TQ = 1024
TM_MERGE = 1024
TM_MOE = 1024
TM_GROUP = 512
E_BLK = EXPERTS_PER_GROUP

NT_DIMS = (((1,), (1,)), ((), ()))


def _params(*sem):
    return pltpu.CompilerParams(dimension_semantics=sem, vmem_limit_bytes=VMEM_LIMIT)


def _sigmoid(x):
    return 1.0 / (1.0 + jnp.exp(-x))


def _layer_norm(x, g, b):
    mu = jnp.mean(x, axis=-1, keepdims=True)
    xc = x - mu
    var = jnp.mean(xc * xc, axis=-1, keepdims=True)
    return xc * lax.rsqrt(var + LN_EPS) * g + b


def _lambda(lam_ref, lam_init):
    lp = lam_ref[...]
    a = jnp.sum(lp[0:1] * lp[1:2], axis=1, keepdims=True)
    b = jnp.sum(lp[2:3] * lp[3:4], axis=1, keepdims=True)
    return jnp.exp(a) - jnp.exp(b) + lam_init


def _dot_split3(a, b):
    a_hi, b_hi = a.astype(BF16), b.astype(BF16)
    a_lo = (a - a_hi.astype(F32)).astype(BF16)
    b_lo = (b - b_hi.astype(F32)).astype(BF16)
    mm = functools.partial(jnp.dot, preferred_element_type=F32)
    return mm(a_hi, b_hi) + (mm(a_hi, b_lo) + mm(a_lo, b_hi))


def _ada_kernel(c_ref, w_ref, b_ref, o_ref):
    c = c_ref[...]
    o_ref[...] = _dot_split3(c * _sigmoid(c), w_ref[...]) + b_ref[...]


def _ada(c_all, w_ada, b_ada):
    n, d = c_all.shape
    cols = w_ada.shape[1]
    return pl.pallas_call(
        _ada_kernel,
        grid=(cols // d,),
        in_specs=[pl.BlockSpec((n, d), lambda j: (0, 0)),
                  pl.BlockSpec((d, d), lambda j: (0, j)),
                  pl.BlockSpec((1, d), lambda j: (0, j))],
        out_specs=pl.BlockSpec((n, d), lambda j: (0, j)),
        out_shape=jax.ShapeDtypeStruct((n, cols), F32),
        compiler_params=_params("arbitrary"),
        name="ada_mod",
    )(c_all, w_ada, b_ada.reshape(1, cols))


def _inproj_kernel(x_ref, sc_ref, sh_ref, w_ref, cost_ref, sint_ref, *out_refs, d, pw, prompt):
    mm_dtype = w_ref.dtype
    dot = _dot_split3 if mm_dtype == F32 else functools.partial(jnp.dot, preferred_element_type=F32)
    h =(x_ref[...] * (1.0 + sc_ref[...]) + sh_ref[...]).astype(mm_dtype)
    half = HEAD_DIM // 2
    scale = HEAD_DIM ** -0.5 * (math.log2(math.e) if prompt else 1.0)
    cost, sint = cost_ref[...], sint_ref[...]

    def rope_rows(xt):
        out = []
        for g in range(d // HEAD_DIM):
            lo = slice(g * HEAD_DIM, g * HEAD_DIM + half)
            hi = slice(g * HEAD_DIM + half, (g + 1) * HEAD_DIM)
            x1, x2 = xt[lo], xt[hi]
            out.append((lo, x1 * cost - x2 * sint))
            out.append((hi, x2 * cost + x1 * sint))
        return out

    qt = rope_rows(dot(h, w_ref[:, 0:d]).T)
    kt = rope_rows(dot(h, w_ref[:, d:2 * d]).T)
    v = dot(h, w_ref[:, 2 * d:3 * d])
    if prompt:
        qt_ref, ktf_ref, kb_ref, vf_ref, vtb_ref, u_ref, g_ref = out_refs
        for sl, r in qt:
            qt_ref[sl, :] = (r * scale).astype(BF16)
        for sl, r in kt:
            ktf_ref[sl, :] = r
        kb_ref[...] = ktf_ref[...].T.astype(BF16)
        vtb_ref[...] = v.T.astype(BF16)
        g_dtype = BF16
    else:
        g_dtype = F32
        qt_ref, ktf_ref, sn_ref, vf_ref, u_ref, g_ref = out_refs
        for idx, ((sl, rq), (_, rk)) in enumerate(zip(qt, kt)):
            qs = rq * scale
            qt_ref[sl, :] = qs
            ktf_ref[sl, :] = rk
            part = jnp.sum(qs * rk, axis=0, keepdims=True)
            g, is_hi = idx // 2, idx % 2
            head, which = g // 2, g % 2
            if is_hi:
                sn_ref[which, head:head + 1, :] += part
            else:
                sn_ref[which, head:head + 1, :] = part
    vf_ref[...] = v
    u_ref[...] = dot(h, w_ref[:, 3 * d:3 * d + pw])
    g_ref[...] = _sigmoid(dot(h, w_ref[:, 3 * d + pw:])).astype(g_dtype)


def _inproj(x, sc_spec, sh_spec, sc_arr, sh_arr, w_in, tabs_t, tab_t_spec, tm, n_seq, prompt):
    t, d = x.shape
    cols = w_in.shape[1]
    pw = cols - 5 * d
    per_seq = t // n_seq
    nblk = per_seq // tm
    row = lambda width: pl.BlockSpec((tm, width), lambda i: (i, 0))
    ft_spec = pl.BlockSpec((None, d, tm), lambda i: (i // nblk, 0, i % nblk))
    ft_shape = lambda dt: jax.ShapeDtypeStruct((n_seq, d, per_seq), dt)
    tok_shape = lambda width, dt: jax.ShapeDtypeStruct((t, width), dt)
    if prompt:
        out_specs = [ft_spec, ft_spec, row(d), row(d), ft_spec, row(pw), row(2 * d)]
        out_shape = [ft_shape(BF16), ft_shape(F32), tok_shape(d, BF16), tok_shape(d, F32), ft_shape(BF16),
                     tok_shape(pw, F32), tok_shape(2 * d, BF16)]
    else:
        assert n_seq == 1 and nblk == 1
        out_specs = [ft_spec, ft_spec, pl.BlockSpec((2, N_HEADS, tm), lambda i: (0, 0, 0)), row(d),
                     row(pw), row(2 * d)]
        out_shape = [ft_shape(F32), ft_shape(F32), jax.ShapeDtypeStruct((2, N_HEADS, t), F32),
                     tok_shape(d, F32), tok_shape(pw, F32), tok_shape(2 * d, F32)]
    return pl.pallas_call(
        functools.partial(_inproj_kernel, d=d, pw=pw, prompt=prompt),
        grid=(t // tm,),
        in_specs=[row(d), sc_spec, sh_spec,
                  pl.BlockSpec((d, cols), lambda i: (0, 0), pipeline_mode=pl.Buffered(1)),
                  tab_t_spec, tab_t_spec],
        out_specs=out_specs,
        out_shape=out_shape,
        compiler_params=_params("arbitrary"),
        name="in_proj",
    )(x, sc_arr, sh_arr, w_in, *tabs_t)


def _rope_tables(pos):
    half = HEAD_DIM // 2
    inv = ROPE_THETA ** (-jnp.arange(half, dtype=F32) / half)
    ang = inv[:, None] * pos.astype(F32)[None, :]
    return jnp.cos(ang), jnp.sin(ang)


def _attn_kernel(lam_ref, qt_ref, k_ref, vt_ref, g_ref, o_ref, q2_sc, sta_sc, stb_sc, m_sc, l_sc, acc_sc,
                 *, tq, lam_init):
    qi = pl.program_id(2)
    zero = jnp.zeros((HEAD_DIM, tq), BF16)
    q2_sc[0:HEAD_DIM, 0:tq] = qt_ref[0:HEAD_DIM, :]
    q2_sc[HEAD_DIM:V_DIM, 0:tq] = zero
    q2_sc[0:HEAD_DIM, tq:2 * tq] = zero
    q2_sc[HEAD_DIM:V_DIM, tq:2 * tq] = qt_ref[HEAD_DIM:V_DIM, :]
    m_sc[...] = jnp.full(m_sc.shape, -jnp.inf, F32)
    l_sc[...] = jnp.zeros(l_sc.shape, F32)
    acc_sc[...] = jnp.zeros(acc_sc.shape, F32)

    def scores(ki, st_ref):
        start = pl.multiple_of(ki * tq, tq)
        st_ref[...] = jnp.dot(k_ref[pl.ds(start, tq), :], q2_sc[...], preferred_element_type=F32)

    def consume(ki, st_ref, masked):
        st = st_ref[...]
        if masked:
            key = lax.broadcasted_iota(jnp.int32, st.shape, 0)
            qry = lax.broadcasted_iota(jnp.int32, st.shape, 1)
            qry = jnp.where(qry >= tq, qry - tq, qry)
            st = jnp.where(key <= qry, st, -jnp.inf)
        m_old = m_sc[...]
        m_new = jnp.maximum(m_old, jnp.max(st, axis=0, keepdims=True))
        alpha = jnp.exp2(m_old - m_new)
        p = jnp.exp2(st - m_new).astype(BF16)
        start = pl.multiple_of(ki * tq, tq)
        vt = vt_ref[:, pl.ds(start, tq)]
        vt1 = jnp.concatenate([vt, jnp.ones((16, tq), BF16)], axis=0)
        r = jnp.dot(vt1, p, preferred_element_type=F32)
        l_sc[...] = alpha * l_sc[...] + r[V_DIM:V_DIM + 1]
        acc_sc[...] = alpha * acc_sc[...] + r[0:V_DIM]
        m_sc[...] = m_new

    scores(0, sta_sc)

    def pair(pi, carry):
        k0 = 2 * pi
        scores(k0 + 1, stb_sc)
        consume(k0, sta_sc, False)
        scores(k0 + 2, sta_sc)
        consume(k0 + 1, stb_sc, False)
        return carry

    lax.fori_loop(0, qi // 2, pair, 0)

    @pl.when(qi % 2 == 0)
    def _():
        consume(qi, sta_sc, True)

    @pl.when(qi % 2 == 1)
    def _():
        scores(qi, stb_sc)
        consume(qi - 1, sta_sc, False)
        consume(qi, stb_sc, True)

    lam = _lambda(lam_ref, lam_init)
    inv_l = 1.0 / l_sc[...]
    ot = acc_sc[:, 0:tq] * inv_l[:, 0:tq] - lam * (acc_sc[:, tq:2 * tq] * inv_l[:, tq:2 * tq])
    ms = jnp.mean(ot * ot, axis=0, keepdims=True)
    ot = ot * lax.rsqrt(ms + SUBLN_EPS) * g_ref[...] * (1.0 - lam_init)
    o_ref[...] = ot.T.astype(o_ref.dtype)


def _attn_prompt(qt_bf, k_bf, vt_bf, lam_p, subln_g, lam_init, tq):
    b, d, s = qt_bf.shape
    nh = d // V_DIM
    return pl.pallas_call(
        functools.partial(_attn_kernel, tq=tq, lam_init=lam_init),
        grid=(b, nh, s // tq),
        in_specs=[pl.BlockSpec(lam_p.shape, lambda bi, h, qi: (0, 0)),
                  pl.BlockSpec((None, V_DIM, tq), lambda bi, h, qi: (bi, h, qi)),
                  pl.BlockSpec((None, s, V_DIM), lambda bi, h, qi: (bi, 0, h)),
                  pl.BlockSpec((None, V_DIM, s), lambda bi, h, qi: (bi, h, 0)),
                  pl.BlockSpec((V_DIM, 1), lambda bi, h, qi: (0, 0))],
        out_specs=pl.BlockSpec((None, tq, V_DIM), lambda bi, h, qi: (bi, qi, h)),
        out_shape=jax.ShapeDtypeStruct((b, s, d), BF16),
        scratch_shapes=[pltpu.VMEM((V_DIM, 2 * tq), BF16),
                        pltpu.VMEM((tq, 2 * tq), F32),
                        pltpu.VMEM((tq, 2 * tq), F32),
                        pltpu.VMEM((1, 2 * tq), F32),
                        pltpu.VMEM((1, 2 * tq), F32),
                        pltpu.VMEM((V_DIM, 2 * tq), F32)],
        compiler_params=_params("arbitrary", "arbitrary", "arbitrary"),
        name="attn_prompt",
    )(lam_p, qt_bf, k_bf, vt_bf, subln_g.reshape(V_DIM, 1))


def _attn_sample_kernel(pt_ref, lam_ref, qt_ref, sn_ref, vn_ref, g_ref, *refs, n_pages, lam_init):
    del pt_ref
    kt_refs, v_refs = refs[:n_pages], refs[n_pages:2 * n_pages]
    o_ref, s_sc = refs[2 * n_pages:]
    b = pl.program_id(0)
    d, n_tok = qt_ref.shape
    page = kt_refs[0].shape[1]

    onehot = jnp.where(lax.broadcasted_iota(jnp.int32, (n_tok, page), 0) == b, 1.0, 0.0)
    qb = jnp.dot(qt_ref[...], onehot, preferred_element_type=F32,
                 precision=lax.Precision.HIGHEST)
    for p in range(n_pages):
        prod = (kt_refs[p][...] * qb).reshape(N_HEADS, V_DIM, page)
        s_sc[0, :, p * page:(p + 1) * page] = jnp.sum(prod[:, 0:HEAD_DIM, :], axis=1)
        s_sc[1, :, p * page:(p + 1) * page] = jnp.sum(prod[:, HEAD_DIM:V_DIM, :], axis=1)

    own = lax.broadcasted_iota(jnp.int32, (N_HEADS, n_tok), 1) == b

    def softmax_parts(which):
        s_new = jnp.max(jnp.where(own, sn_ref[which], -jnp.inf), axis=1, keepdims=True)
        s = s_sc[which]
        m = jnp.maximum(jnp.max(s, axis=1, keepdims=True), s_new)
        p = jnp.exp(s - m)
        p_new = jnp.exp(s_new - m)
        inv_l = 1.0 / (jnp.sum(p, axis=1, keepdims=True) + p_new)
        return p * inv_l, p_new * inv_l

    lam = _lambda(lam_ref, lam_init)
    w1, wn1 = softmax_parts(0)
    w2, wn2 = softmax_parts(1)
    w = w1 - lam * w2
    acc = (wn1 - lam * wn2) * vn_ref[...]
    for p in range(n_pages):
        for t in range(page):
            c = p * page + t
            acc = acc + w[:, c:c + 1] * v_refs[p][t]
    ms = jnp.mean(acc * acc, axis=1, keepdims=True)
    o_ref[...] = acc * lax.rsqrt(ms + SUBLN_EPS) * g_ref[...] * (1.0 - lam_init)


def _attn_sample(qt, s_new, v_new, cache_k, cache_v, page_table, lam_p, subln_g, lam_init):
    d, db = qt.shape
    n_pages = page_table.shape[1]
    n_phys, page = cache_k.shape[0], cache_k.shape[1]
    ck = jnp.transpose(cache_k, (0, 2, 3, 4, 1)).reshape(n_phys, d, page)
    pt = page_table.reshape(-1)

    def page_spec(p, shape):
        zeros = (0,) * len(shape)
        return pl.BlockSpec((None,) + shape, lambda b, pt_ref: (pt_ref[b * n_pages + p],) + zeros)

    const = lambda shape: pl.BlockSpec(shape, lambda b, pt_ref: (0,) * len(shape))
    head_spec = pl.BlockSpec((None, N_HEADS, V_DIM), lambda b, pt_ref: (b, 0, 0))
    grid_spec = pltpu.PrefetchScalarGridSpec(
        num_scalar_prefetch=1,
        grid=(db,),
        in_specs=[const(lam_p.shape), const((d, db)), const((2, N_HEADS, db)), head_spec, const((1, V_DIM))]
                 + [page_spec(p, (d, page)) for p in range(n_pages)]
                 + [page_spec(p, (page, N_HEADS, V_DIM)) for p in range(n_pages)],
        out_specs=head_spec,
        scratch_shapes=[pltpu.VMEM((2, N_HEADS, n_pages * page), F32)],
    )
    out = pl.pallas_call(
        functools.partial(_attn_sample_kernel, n_pages=n_pages, lam_init=lam_init),
        grid_spec=grid_spec,
        out_shape=jax.ShapeDtypeStruct((db, N_HEADS, V_DIM), F32),
        compiler_params=_params("arbitrary"),
        name="attn_sample",
    )(pt, lam_p, qt, s_new, v_new.reshape(db, N_HEADS, V_DIM), subln_g.reshape(1, V_DIM),
      *([ck] * n_pages), *([cache_v] * n_pages))
    return out.reshape(db, d)


POOL_HALO = 16


def _pool_tile(u_ref, halo_ref, ext_sc, seq_tile):
    tm = u_ref.shape[0]
    hb = halo_ref.shape[0]

    @pl.when(seq_tile == 0)
    def _():
        ext_sc[0:hb, :] = jnp.zeros((hb, ext_sc.shape[1]), F32)

    @pl.when(seq_tile > 0)
    def _():
        ext_sc[0:hb, :] = halo_ref[...]

    ext_sc[hb:hb + tm, :] = u_ref[...]
    pos = seq_tile * tm + lax.broadcasted_iota(jnp.int32, (tm, POOL_GROUP_DIM), 0)
    parts = []
    for g, win in enumerate(POOL_WINDOWS):
        sl = slice(g * POOL_GROUP_DIM, (g + 1) * POOL_GROUP_DIM)
        cur = ext_sc[hb:hb + tm, sl]
        tot = cur
        for j in range(1, win):
            tot = tot + ext_sc[hb - j:hb - j + tm, sl]
        cnt = jnp.minimum(win, pos + 1).astype(F32)
        parts.append(tot / cnt - cur)
    return jnp.concatenate(parts, axis=1)


def _pool_sample_kernel(st_ref, u_ref, o_ref, np_ref, *, start_pos):
    nb = st_ref.shape[0]
    u = u_ref[...]
    for g, win in enumerate(POOL_WINDOWS):
        sl = slice(g * POOL_GROUP_DIM, (g + 1) * POOL_GROUP_DIM)
        cur = u[:, sl]
        tot = cur
        for j in range(1, win):
            tot = tot + st_ref[nb - j, :, sl]
        o_ref[:, sl] = tot / float(min(win, start_pos + 1)) - cur
    for j in range(nb - 1):
        np_ref[j] = st_ref[j + 1]
    np_ref[nb - 1] = u


def _pool_sample(state, u, start_pos):
    db, nb, pw = state.shape
    full = lambda shape: pl.BlockSpec(shape, lambda i: (0,) * len(shape))
    pooled, new_pool = pl.pallas_call(
        functools.partial(_pool_sample_kernel, start_pos=start_pos),
        grid=(1,),
        in_specs=[full((nb, db, pw)), full((db, pw))],
        out_specs=[full((db, pw)), full((nb, db, pw))],
        out_shape=[jax.ShapeDtypeStruct((db, pw), F32), jax.ShapeDtypeStruct((nb, db, pw), F32)],
        compiler_params=_params("arbitrary"),
        name="pool_sample",
    )(jnp.transpose(state, (1, 0, 2)), u)
    return pooled, jnp.transpose(new_pool, (1, 0, 2))


def _route(lt):
    g = [lt[j:j + 1] for j in range(N_GROUPS)]
    gmax = functools.reduce(jnp.maximum, g)
    gidx = jnp.full(gmax.shape, N_GROUPS - 1, jnp.int32)
    for j in range(N_GROUPS - 2, -1, -1):
        gidx = jnp.where(g[j] == gmax, j, gidx)
    p_g = 1.0 / functools.reduce(jnp.add, [jnp.exp(gj - gmax) for gj in g])
    e = []
    for k in range(EXPERTS_PER_GROUP):
        sel = lt[N_GROUPS + k:N_GROUPS + k + 1]
        for j in range(1, N_GROUPS):
            r = N_GROUPS + j * EXPERTS_PER_GROUP + k
            sel = jnp.where(gidx == j, lt[r:r + 1], sel)
        e.append(sel)

    def first_argmax(vals):
        vmax = functools.reduce(jnp.maximum, vals)
        idx = jnp.full(vmax.shape, len(vals) - 1, jnp.int32)
        for k in range(len(vals) - 2, -1, -1):
            idx = jnp.where(vals[k] == vmax, k, idx)
        return vmax, idx

    v1, i1 = first_argmax(e)
    rest = [jnp.where(i1 == k, -jnp.inf, e[k]) for k in range(EXPERTS_PER_GROUP)]
    v2, i2 = first_argmax(rest)
    t = jnp.exp(v2 - v1)
    w1 = p_g / (1.0 + t)
    w2 = p_g * t / (1.0 + t)
    within = [jnp.where(i1 == k, w1, 0.0) + jnp.where(i2 == k, w2, 0.0) for k in range(EXPERTS_PER_GROUP)]
    rows = [jnp.where(gidx == j, within[k], 0.0) for j in range(N_GROUPS) for k in range(EXPERTS_PER_GROUP)]
    a, b = jnp.minimum(i1, i2), jnp.maximum(i1, i2)
    pair = jnp.where(a == 0, b - 1, jnp.where(a == 1, b + 1, 5))
    return jnp.concatenate(rows, axis=0), gidx * PAIRS_PER_GROUP + pair


def _pack_bf16_pair(a, b):
    a_bits = lax.bitcast_convert_type(a.astype(BF16).astype(F32), jnp.uint32)
    b_bits = lax.bitcast_convert_type(b.astype(BF16).astype(F32), jnp.uint32)
    return (a_bits >> 16) | (b_bits & jnp.uint32(0xFFFF0000))


def _unpack_bf16_pair(w):
    lo = lax.bitcast_convert_type(w << 16, F32)
    hi = lax.bitcast_convert_type(w & jnp.uint32(0xFFFF0000), F32)
    return lo, hi


def _merge_kernel(*refs, alpha, tiles_per_seq):
    if tiles_per_seq:
        u_ref, halo_ref, *refs, ext_sc = refs
        pooled = _pool_tile(u_ref, halo_ref, ext_sc, pl.program_id(0) % tiles_per_seq)
    else:
        pooled_ref, *refs = refs
        pooled = pooled_ref[...]
    (gates_ref, attn_ref, x_ref, g1_ref, sc2_ref, sh2_ref, wpool_ref, pscale_ref, wout_ref, lng_ref, lnb_ref,
     wr_ref, br_ref, x1_ref, *route_refs) = refs
    d = x_ref.shape[1]
    tm = x_ref.shape[0]
    mm_dtype = wout_ref.dtype
    dot = functools.partial(jnp.dot, preferred_element_type=F32,
                            precision=lax.Precision.HIGHEST if mm_dtype == F32 else None)
    pooled = pooled.astype(mm_dtype)
    parts = [dot(pooled[:, g * POOL_GROUP_DIM:(g + 1) * POOL_GROUP_DIM], wpool_ref[g])
             for g in range(len(POOL_WINDOWS))]
    pool_out = jnp.concatenate(parts, axis=1) * pscale_ref[...]
    g_dtype = gates_ref.dtype
    merged = (gates_ref[:, 0:d] * pool_out.astype(g_dtype)
              + gates_ref[:, d:2 * d] * attn_ref[...].astype(g_dtype))
    y = dot(merged.astype(mm_dtype), wout_ref[...])
    x1 = _layer_norm(alpha * x_ref[...] + (1.0 + g1_ref[...]) * y, lng_ref[...], lnb_ref[...])
    x1_ref[...] = x1
    h2 = x1 * (1.0 + sc2_ref[...]) + sh2_ref[...]
    lt = lax.dot_general(wr_ref[...], h2, NT_DIMS, preferred_element_type=F32,
                         precision=lax.Precision.HIGHEST) + br_ref[...]
    comb_t, cls = _route(lt)
    comb = jnp.concatenate([comb_t, jnp.zeros((LANES - N_EXPERTS, tm), F32)], axis=0).T
    if tiles_per_seq:
        row_ref, cls_ref = route_refs
        row_ref[:, 0:d // 2] = _pack_bf16_pair(h2[:, 0:d // 2], h2[:, d // 2:d])
        row_ref[:, d // 2:d // 2 + LANES] = lax.bitcast_convert_type(comb, jnp.uint32)
        cls_ref[...] = cls
    else:
        h2_ref, comb_ref = route_refs
        h2_ref[...] = h2.astype(BF16)
        comb_ref[...] = comb


def _merge(pool_in, gates, attn, x, mod_specs, mod_arrs, w_pool, pool_scale, w_out,
           ln_g, ln_b, w_r, b_r, alpha, tm, seq_len=None):
    t, d = x.shape
    pw = pool_in.shape[1]
    row = lambda width: pl.BlockSpec((tm, width), lambda i: (i, 0))
    const = lambda shape: pl.BlockSpec(shape, lambda i: (0,) * len(shape))
    if seq_len:
        halo_spec = pl.BlockSpec((POOL_HALO, pw), lambda i: (jnp.maximum(i * (tm // POOL_HALO) - 1, 0), 0))
        pool_specs, pool_args = [row(pw), halo_spec], [pool_in, pool_in]
        scratch = [pltpu.VMEM((POOL_HALO + tm, pw), F32)]
    else:
        pool_specs, pool_args, scratch = [row(pw)], [pool_in], []
    if seq_len:
        route_specs = [row(d // 2 + LANES), pl.BlockSpec((None, 1, tm), lambda i: (i, 0, 0))]
        route_shapes = [jax.ShapeDtypeStruct((t, d // 2 + LANES), jnp.uint32),
                        jax.ShapeDtypeStruct((t // tm, 1, tm), jnp.int32)]
    else:
        route_specs = [row(d), row(LANES)]
        route_shapes = [jax.ShapeDtypeStruct((t, d), BF16), jax.ShapeDtypeStruct((t, LANES), F32)]
    return pl.pallas_call(
        functools.partial(_merge_kernel, alpha=alpha, tiles_per_seq=seq_len // tm if seq_len else None),
        grid=(t // tm,),
        in_specs=[*pool_specs, row(2 * d), row(d), row(d), *mod_specs,
                  const(w_pool.shape), const((1, d)), const((d, d)), const((1, d)), const((1, d)),
                  const(w_r.shape), const(b_r.shape)],
        out_specs=[row(d), *route_specs],
        out_shape=[jax.ShapeDtypeStruct((t, d), F32), *route_shapes],
        scratch_shapes=scratch,
        compiler_params=_params("arbitrary"),
        name="merge",
    )(*pool_args, gates, attn, x, *mod_arrs, w_pool, pool_scale.reshape(1, d), w_out,
      ln_g.reshape(1, d), ln_b.reshape(1, d), w_r, b_r)


def _moe_kernel(h2_ref, comb_ref, wg_ref, wu_ref, wd_ref, x1_ref, g2_ref, lng_ref, lnb_ref,
                o_ref, acc_sc, *, alpha):
    j = pl.program_id(1)
    eb = wg_ref.shape[0]

    @pl.when(j == 0)
    def _():
        acc_sc[...] = jnp.zeros(acc_sc.shape, F32)

    h2 = h2_ref[...]
    comb = comb_ref[...]
    lane = lax.broadcasted_iota(jnp.int32, comb.shape, 1)
    for e in range(eb):
        hg = jnp.dot(h2, wg_ref[e], preferred_element_type=F32)
        hu = jnp.dot(h2, wu_ref[e], preferred_element_type=F32)
        c = jnp.sum(jnp.where(lane == j * eb + e, comb, 0.0), axis=1, keepdims=True)
        act = (hg * _sigmoid(hg)) * hu * c
        acc_sc[...] += jnp.dot(act.astype(BF16), wd_ref[e], preferred_element_type=F32)

    @pl.when(j == pl.num_programs(1) - 1)
    def _():
        y = alpha * x1_ref[...] + (1.0 + g2_ref[...]) * acc_sc[...]
        o_ref[...] = _layer_norm(y, lng_ref[...], lnb_ref[...])


def _moe(h2, comb, wg_bf, wu_bf, wd_bf, x1, g2_spec, g2_arr, ln_g, ln_b, alpha, tm, eb):
    t, d = x1.shape
    ne, _, ff = wg_bf.shape
    row = lambda width: pl.BlockSpec((tm, width), lambda i, j: (i, 0))
    const = lambda shape: pl.BlockSpec(shape, lambda i, j: (0,) * len(shape))
    return pl.pallas_call(
        functools.partial(_moe_kernel, alpha=alpha),
        grid=(t // tm, ne // eb),
        in_specs=[row(d), row(LANES),
                  pl.BlockSpec((eb, d, ff), lambda i, j: (j, 0, 0)),
                  pl.BlockSpec((eb, d, ff), lambda i, j: (j, 0, 0)),
                  pl.BlockSpec((eb, ff, d), lambda i, j: (j, 0, 0)),
                  row(d), g2_spec, const((1, d)), const((1, d))],
        out_specs=row(d),
        out_shape=jax.ShapeDtypeStruct((t, d), F32),
        scratch_shapes=[pltpu.VMEM((tm, d), F32)],
        compiler_params=_params("arbitrary", "arbitrary"),
        name="moe_ffn",
    )(h2, comb, wg_bf, wu_bf, wd_bf, x1, g2_arr, ln_g.reshape(1, d), ln_b.reshape(1, d))


def _move_rows(x, idx, n_out=None):
    n, width = idx.shape[0], x.shape[1]
    scatter = n_out is not None
    info = pltpu.get_tpu_info().sparse_core
    units = info.num_cores * info.num_subcores
    per_unit = n // (units * GATHER_WINDOW)
    assert per_unit * units * GATHER_WINDOW == n

    @pl.kernel(out_type=jax.ShapeDtypeStruct((n_out if scatter else n, width), x.dtype),
               mesh=plsc.VectorSubcoreMesh(core_axis_name="core", subcore_axis_name="subcore"),
               scratch_types=[pltpu.VMEM((1, GATHER_WINDOW), jnp.int32), pltpu.VMEM((GATHER_WINDOW, width), x.dtype)])
    def move_kernel(x_hbm, i_hbm, o_hbm, i_vmem, buf):
        unit = lax.axis_index("core") * info.num_subcores + lax.axis_index("subcore")

        @pl.loop(0, per_unit)
        def _(j):
            base = (unit * per_unit + j) * GATHER_WINDOW
            pltpu.sync_copy(i_hbm.at[:, pl.ds(base, GATHER_WINDOW)], i_vmem)
            if scatter:
                pltpu.sync_copy(x_hbm.at[pl.ds(base, GATHER_WINDOW), :], buf)
                pltpu.sync_copy(buf, o_hbm.at[i_vmem.at[0]])
            else:
                pltpu.sync_copy(x_hbm.at[i_vmem.at[0]], buf)
                pltpu.sync_copy(buf, o_hbm.at[pl.ds(base, GATHER_WINDOW), :])

    return move_kernel(x, idx.reshape(1, n))


def _dispatch_plan(cls, tile, n_slots):
    onehot = (cls[:, None] == jnp.arange(N_CLASSES, dtype=jnp.int32)[None, :]).astype(jnp.int32)
    rank = jnp.cumsum(onehot, axis=0) - onehot
    counts = jnp.sum(onehot, axis=0)
    padded = -(-counts // tile) * tile
    ends = jnp.cumsum(padded)
    starts = ends - padded
    pos = jnp.sum(onehot * (starts[None, :] + rank), axis=1)
    tile_start = jnp.arange(n_slots // tile, dtype=jnp.int32) * tile
    tile_class = jnp.minimum(jnp.sum((tile_start[:, None] >= ends[None, :]).astype(jnp.int32), axis=1),
                             N_CLASSES - 1)
    tile_rows = jnp.clip((starts + counts)[tile_class] - tile_start, 0, tile)
    return pos, tile_class, tile_rows


def _moe_grouped_kernel(te_ref, tr_ref, row_ref, *refs):
    *w_refs, y_ref = refs
    n_rows = tr_ref[pl.program_id(0)]
    half = y_ref.shape[1]

    @pl.when(n_rows == 0)
    def _():
        y_ref[...] = jnp.zeros(y_ref.shape, y_ref.dtype)

    @pl.when(n_rows > 0)
    def _():
        live = lax.broadcasted_iota(jnp.int32, (row_ref.shape[0], 1), 0) < n_rows
        lo, hi = _unpack_bf16_pair(row_ref[:, 0:half])
        h2 = jnp.where(live, jnp.concatenate([lo, hi], axis=1), 0.0).astype(BF16)
        comb = jnp.where(live, lax.bitcast_convert_type(row_ref[:, half:half + LANES], F32), 0.0)
        lane = lax.broadcasted_iota(jnp.int32, comb.shape, 1)
        acc = jnp.zeros((h2.shape[0], 2 * half), F32)
        for k in range(TOP_K_IN_GROUP):
            wg_ref, wu_ref, wd_ref = w_refs[3 * k:3 * k + 3]
            expert = te_ref[k, pl.program_id(0)]
            hg = jnp.dot(h2, wg_ref[...], preferred_element_type=F32)
            hu = jnp.dot(h2, wu_ref[...], preferred_element_type=F32)
            c = jnp.sum(jnp.where(lane == expert, comb, 0.0), axis=1, keepdims=True)
            act = (hg * _sigmoid(hg)) * hu * c
            acc = acc + jnp.dot(act.astype(BF16), wd_ref[...], preferred_element_type=F32)
        y_ref[...] = _pack_bf16_pair(acc[:, 0:half], acc[:, half:2 * half])


def _moe_grouped(rows_sorted, tile_experts, tile_rows, wg_bf, wu_bf, wd_bf, tile):
    n_slots, width = rows_sorted.shape
    ne, d, ff = wg_bf.shape
    expert_w = lambda k, shape: pl.BlockSpec((None,) + shape, lambda i, te, tr: (te[k, i], 0, 0))
    w_specs, w_args = [], []
    for k in range(TOP_K_IN_GROUP):
        w_specs += [expert_w(k, (d, ff)), expert_w(k, (d, ff)), expert_w(k, (ff, d))]
        w_args += [wg_bf, wu_bf, wd_bf]
    grid_spec = pltpu.PrefetchScalarGridSpec(
        num_scalar_prefetch=2,
        grid=(n_slots // tile,),
        in_specs=[pl.BlockSpec((tile, width), lambda i, te, tr: (i, 0)), *w_specs],
        out_specs=pl.BlockSpec((tile, d // 2), lambda i, te, tr: (i, 0)),
    )
    return pl.pallas_call(
        _moe_grouped_kernel,
        grid_spec=grid_spec,
        out_shape=jax.ShapeDtypeStruct((n_slots, d // 2), jnp.uint32),
        compiler_params=_params("arbitrary"),
        name="moe_grouped",
    )(tile_experts, tile_rows, rows_sorted, *w_args)


def _moe_finish_kernel(x1_ref, y_ref, g2_ref, lng_ref, lnb_ref, o_ref, *, alpha):
    lo, hi = _unpack_bf16_pair(y_ref[...])
    y = jnp.concatenate([lo, hi], axis=1)
    o_ref[...] = _layer_norm(alpha * x1_ref[...] + (1.0 + g2_ref[...]) * y, lng_ref[...], lnb_ref[...])


def _moe_finish(x1, y_packed, g2_spec, g2_arr, ln_g, ln_b, alpha, tm):
    t, d = x1.shape
    row = lambda width: pl.BlockSpec((tm, width), lambda i: (i, 0))
    const = lambda shape: pl.BlockSpec(shape, lambda i: (0,) * len(shape))
    return pl.pallas_call(
        functools.partial(_moe_finish_kernel, alpha=alpha),
        grid=(t // tm,),
        in_specs=[row(d), row(d // 2), g2_spec, const((1, d)), const((1, d))],
        out_specs=row(d),
        out_shape=jax.ShapeDtypeStruct((t, d), F32),
        compiler_params=_params("arbitrary"),
        name="moe_finish",
    )(x1, y_packed, g2_arr, ln_g.reshape(1, d), ln_b.reshape(1, d))


def kernel(x_prompt, x_sample, cache_k, cache_v, state_pool, page_table, c_prompt, c_sample, w_ada, b_ada, w_in, lambda_q1, lambda_k1, lambda_q2, lambda_k2, subln_g, w_pool, pool_scale, w_out, ln1_g, ln1_b, w_group, b_group, w_router, b_router, w_e_gate, w_e_up, w_e_down, ln2_g, ln2_b):
    depth = w_in.shape[0]
    bsz, seq, d = x_prompt.shape
    db, dseq, _ = x_sample.shape
    assert dseq == 1, "sample group decodes one token per sequence"
    n_pages, page = page_table.shape[1], cache_k.shape[2]
    past_len = n_pages * page
    alpha = (2.0 * depth) ** 0.25
    tp, ts = bsz * seq, db * dseq

    sc_info = pltpu.get_tpu_info().sparse_core
    sc_units = sc_info.num_cores * sc_info.num_subcores
    assert tp % (GATHER_WINDOW * sc_units) == 0

    xp = x_prompt.reshape(tp, d)
    xs = x_sample.reshape(ts, d)
    n_c = bsz + db
    n_c_pad = -(-n_c // 8) * 8
    c_all = jnp.concatenate([c_prompt, c_sample, jnp.zeros((n_c_pad - n_c, d), F32)], axis=0)
    tabs_p = _rope_tables(jnp.arange(seq))
    tabs_s = _rope_tables(jnp.full((ts,), past_len))

    outs = {k: [] for k in ("kp", "vp", "pp", "ks", "vs", "ps")}
    for l in range(depth):
        lam_init = 0.8 - 0.6 * math.exp(-0.3 * l)
        mod = _ada(c_all, w_ada[l], b_ada[l])
        mod_p = mod[:bsz].reshape(bsz * N_MOD, 1, d)
        mod_s = mod[bsz:n_c]
        w_in_bf = w_in[l].astype(BF16)
        w_pool_bf = w_pool[l].astype(BF16)
        w_out_bf = w_out[l].astype(BF16)
        wg_bf, wu_bf, wd_bf = (w[l].astype(BF16) for w in (w_e_gate, w_e_up, w_e_down))
        lam_p = jnp.stack([lambda_q1[l], lambda_k1[l], lambda_q2[l], lambda_k2[l]])
        n_r = N_GROUPS + N_EXPERTS
        w_r = jnp.concatenate([w_group[l], w_router[l]], axis=1).T
        w_r = jnp.concatenate([w_r, jnp.zeros((ROUTER_ROWS - n_r, d), F32)], axis=0)
        b_r = jnp.concatenate([b_group[l], b_router[l], jnp.zeros((ROUTER_ROWS - n_r,), F32)]).reshape(ROUTER_ROWS, 1)

        def pmod(k, tm, two_d=False):
            per_b = seq // tm
            if two_d:
                return pl.BlockSpec((None, 1, d), lambda i, j: ((i // per_b) * N_MOD + k, 0, 0))
            return pl.BlockSpec((None, 1, d), lambda i: ((i // per_b) * N_MOD + k, 0, 0))

        nblk = seq // TM_PROJ
        tab_t_spec = pl.BlockSpec((HEAD_DIM // 2, TM_PROJ), lambda i: (0, i % nblk))
        qt_bf, kt_f, k_bf, v_f, vt_bf, u, gates = _inproj(
            xp, pmod(1, TM_PROJ), pmod(0, TM_PROJ), mod_p, mod_p, w_in_bf,
            tabs_p, tab_t_spec, TM_PROJ, bsz, True)
        attn = _attn_prompt(qt_bf, k_bf.reshape(bsz, seq, d), vt_bf,
                            lam_p, subln_g[l], lam_init, TQ).reshape(tp, d)
        x1, ffn_rows, cls = _merge(u, gates, attn, xp,
                                   [pmod(2, TM_MERGE), pmod(4, TM_MERGE), pmod(3, TM_MERGE)], [mod_p] * 3,
                                   w_pool_bf, pool_scale[l], w_out_bf, ln1_g[l], ln1_b[l], w_r, b_r, alpha,
                                   TM_MERGE, seq_len=seq)
        slot_quantum = GATHER_WINDOW * sc_units
        n_slots = -(-(tp + N_CLASSES * (TM_GROUP - 1)) // slot_quantum) * slot_quantum
        pos, tile_class, tile_rows = _dispatch_plan(cls.reshape(tp), TM_GROUP, n_slots)
        class_experts = jnp.array([[g * EXPERTS_PER_GROUP + pair[k] for g in range(N_GROUPS) for pair in EXPERT_PAIRS]
                                   for k in range(TOP_K_IN_GROUP)], jnp.int32)
        tile_experts = class_experts[:, tile_class]
        rows_sorted = _move_rows(ffn_rows, pos, n_slots)
        outs["kp"].append(jnp.transpose(kt_f.reshape(bsz, N_HEADS, 2, HEAD_DIM, seq), (0, 4, 1, 2, 3)))
        outs["vp"].append(v_f.reshape(bsz, seq, N_HEADS, V_DIM))
        outs["pp"].append(u.reshape(bsz, seq, -1)[:, seq - POOL_BUF:])

        def smod(k, two_d=False):
            if two_d:
                return pl.BlockSpec((ts, d), lambda i, j: (0, k))
            return pl.BlockSpec((ts, d), lambda i: (0, k))

        tab_t_spec_s = pl.BlockSpec((HEAD_DIM // 2, ts), lambda i: (0, 0))
        qt_s, kt_f, s_new, v_f, u, gates = _inproj(
            xs, smod(1), smod(0), mod_s, mod_s, w_in[l],
            tabs_s, tab_t_spec_s, ts, 1, False)
        hb = ts // 2
        attn_half = lambda sl: _attn_sample(qt_s[0][:, sl], s_new[:, :, sl], v_f[sl], cache_k[l], cache_v[l],
                                            page_table[sl], lam_p, subln_g[l], lam_init)
        rows_sorted, attn_a = lax.optimization_barrier((rows_sorted, attn_half(slice(0, hb))))
        y_sorted = _moe_grouped(rows_sorted, tile_experts, tile_rows, wg_bf, wu_bf, wd_bf, TM_GROUP)
        y_tok, attn_b = lax.optimization_barrier((_move_rows(y_sorted, pos), attn_half(slice(hb, ts))))
        xp = _moe_finish(x1, y_tok, pmod(5, TM_MOE), mod_p, ln2_g[l], ln2_b[l], alpha, TM_MOE)
        attn = jnp.concatenate([attn_a, attn_b], axis=0)
        pooled, new_pool = _pool_sample(state_pool[l], u, past_len)
        x1, h2, comb = _merge(pooled, gates, attn, xs, [smod(2), smod(4), smod(3)], [mod_s] * 3,
                              w_pool[l], pool_scale[l], w_out[l], ln1_g[l], ln1_b[l], w_r, b_r, alpha, ts)
        xs = _moe(h2, comb, wg_bf, wu_bf, wd_bf, x1, smod(5, True), mod_s,
                  ln2_g[l], ln2_b[l], alpha, ts, E_BLK)
        outs["ks"].append(jnp.transpose(kt_f.reshape(N_HEADS, 2, HEAD_DIM, db, dseq), (3, 4, 0, 1, 2)))
        outs["vs"].append(v_f.reshape(db, dseq, N_HEADS, V_DIM))
        outs["ps"].append(new_pool)

    return (xp.reshape(bsz, seq, d), xs.reshape(db, dseq, d),
            jnp.stack(outs["kp"]), jnp.stack(outs["vp"]), jnp.stack(outs["pp"]),
            jnp.stack(outs["ks"]), jnp.stack(outs["vs"]), jnp.stack(outs["ps"]))
```

```python
import functools
import math

import jax
import jax.numpy as jnp
from jax import lax
from jax.experimental import pallas as pl
from jax.experimental.pallas import tpu as pltpu
from jax.experimental.pallas import tpu_sc as plsc

F32 = jnp.float32
BF16 = jnp.bfloat16

LANES = 128
VMEM_LIMIT = 56 << 20

N_MOD = 6
N_HEADS = 8
HEAD_DIM = 64
V_DIM = 2 * HEAD_DIM
ROPE_THETA = 10000.0
POOL_WINDOWS = (2, 4, 8, 16)
POOL_GROUP_DIM = 128
POOL_BUF = max(POOL_WINDOWS) - 1
N_GROUPS = 4
EXPERTS_PER_GROUP = 4
N_EXPERTS = N_GROUPS * EXPERTS_PER_GROUP
TOP_K_IN_GROUP = 2
EXPERT_PAIRS = tuple((a, b) for a in range(EXPERTS_PER_GROUP) for b in range(a + 1, EXPERTS_PER_GROUP))
PAIRS_PER_GROUP = len(EXPERT_PAIRS)
N_CLASSES = N_GROUPS * PAIRS_PER_GROUP
LN_EPS = 1e-5
SUBLN_EPS = 1e-5
ROUTER_ROWS = 32
GATHER_WINDOW = 128

TM_PROJ = 256
TQ = 1024
TM_MERGE = 1024
TM_MOE = 1024
TM_GROUP = 512
E_BLK = EXPERTS_PER_GROUP

NT_DIMS = (((1,), (1,)), ((), ()))


def _params(*sem):
    return pltpu.CompilerParams(dimension_semantics=sem, vmem_limit_bytes=VMEM_LIMIT)


def _sigmoid(x):
    return 1.0 / (1.0 + jnp.exp(-x))


def _layer_norm(x, g, b):
    mu = jnp.mean(x, axis=-1, keepdims=True)
    xc = x - mu
    var = jnp.mean(xc * xc, axis=-1, keepdims=True)
    return xc * lax.rsqrt(var + LN_EPS) * g + b


def _lambda(lam_ref, lam_init):
    lp = lam_ref[...]
    a = jnp.sum(lp[0:1] * lp[1:2], axis=1, keepdims=True)
    b = jnp.sum(lp[2:3] * lp[3:4], axis=1, keepdims=True)
    return jnp.exp(a) - jnp.exp(b) + lam_init


def _dot_split3(a, b, nt=False):
    a_hi, b_hi = a.astype(BF16), b.astype(BF16)
    a_lo = (a - a_hi.astype(F32)).astype(BF16)
    b_lo = (b - b_hi.astype(F32)).astype(BF16)
    if nt:
        mm = functools.partial(lax.dot_general, dimension_numbers=NT_DIMS, preferred_element_type=F32)
    else:
        mm = functools.partial(jnp.dot, preferred_element_type=F32)
    return mm(a_hi, b_hi) + (mm(a_hi, b_lo) + mm(a_lo, b_hi))


def _ada_kernel(c_ref, w_ref, b_ref, o_ref):
    c = c_ref[...]
    o_ref[...] = _dot_split3(c * _sigmoid(c), w_ref[...]) + b_ref[...]


def _ada(c_all, w_ada, b_ada):
    n, d = c_all.shape
    cols = w_ada.shape[1]
    return pl.pallas_call(
        _ada_kernel,
        grid=(cols // d,),
        in_specs=[pl.BlockSpec((n, d), lambda j: (0, 0)),
                  pl.BlockSpec((d, d), lambda j: (0, j)),
                  pl.BlockSpec((1, d), lambda j: (0, j))],
        out_specs=pl.BlockSpec((n, d), lambda j: (0, j)),
        out_shape=jax.ShapeDtypeStruct((n, cols), F32),
        compiler_params=_params("arbitrary"),
        name="ada_mod",
    )(c_all, w_ada, b_ada.reshape(1, cols))


def _inproj_kernel(x_ref, sc_ref, sh_ref, w_ref, cost_ref, sint_ref, *out_refs, d, pw, prompt):
    mm_dtype = w_ref.dtype
    dot = _dot_split3 if mm_dtype == F32 else functools.partial(jnp.dot, preferred_element_type=F32)
    h =(x_ref[...] * (1.0 + sc_ref[...]) + sh_ref[...]).astype(mm_dtype)
    half = HEAD_DIM // 2
    scale = HEAD_DIM ** -0.5 * (math.log2(math.e) if prompt else 1.0)
    cost, sint = cost_ref[...], sint_ref[...]

    def rope_rows(xt):
        out = []
        for g in range(d // HEAD_DIM):
            lo = slice(g * HEAD_DIM, g * HEAD_DIM + half)
            hi = slice(g * HEAD_DIM + half, (g + 1) * HEAD_DIM)
            x1, x2 = xt[lo], xt[hi]
            out.append((lo, x1 * cost - x2 * sint))
            out.append((hi, x2 * cost + x1 * sint))
        return out

    qt = rope_rows(dot(h, w_ref[:, 0:d]).T)
    kt = rope_rows(dot(h, w_ref[:, d:2 * d]).T)
    v = dot(h, w_ref[:, 2 * d:3 * d])
    if prompt:
        qt_ref, ktf_ref, kb_ref, vf_ref, vtb_ref, u_ref, g_ref = out_refs
        for sl, r in qt:
            qt_ref[sl, :] = (r * scale).astype(BF16)
        for sl, r in kt:
            ktf_ref[sl, :] = r
        kb_ref[...] = ktf_ref[...].T.astype(BF16)
        vtb_ref[...] = v.T.astype(BF16)
        g_dtype = BF16
    else:
        g_dtype = F32
        qt_ref, ktf_ref, sn_ref, vf_ref, u_ref, g_ref = out_refs
        for idx, ((sl, rq), (_, rk)) in enumerate(zip(qt, kt)):
            qs = rq * scale
            qt_ref[sl, :] = qs
            ktf_ref[sl, :] = rk
            part = jnp.sum(qs * rk, axis=0, keepdims=True)
            g, is_hi = idx // 2, idx % 2
            head, which = g // 2, g % 2
            if is_hi:
                sn_ref[which, head:head + 1, :] += part
            else:
                sn_ref[which, head:head + 1, :] = part
    vf_ref[...] = v
    u_ref[...] = dot(h, w_ref[:, 3 * d:3 * d + pw])
    g_ref[...] = _sigmoid(dot(h, w_ref[:, 3 * d + pw:])).astype(g_dtype)


def _inproj(x, sc_spec, sh_spec, sc_arr, sh_arr, w_in, tabs_t, tab_t_spec, tm, n_seq, prompt):
    t, d = x.shape
    cols = w_in.shape[1]
    pw = cols - 5 * d
    per_seq = t // n_seq
    nblk = per_seq // tm
    row = lambda width: pl.BlockSpec((tm, width), lambda i: (i, 0))
    ft_spec = pl.BlockSpec((None, d, tm), lambda i: (i // nblk, 0, i % nblk))
    ft_shape = lambda dt: jax.ShapeDtypeStruct((n_seq, d, per_seq), dt)
    tok_shape = lambda width, dt: jax.ShapeDtypeStruct((t, width), dt)
    if prompt:
        out_specs = [ft_spec, ft_spec, row(d), row(d), ft_spec, row(pw), row(2 * d)]
        out_shape = [ft_shape(BF16), ft_shape(F32), tok_shape(d, BF16), tok_shape(d, F32), ft_shape(BF16),
                     tok_shape(pw, F32), tok_shape(2 * d, BF16)]
    else:
        assert n_seq == 1 and nblk == 1
        out_specs = [ft_spec, ft_spec, pl.BlockSpec((2, N_HEADS, tm), lambda i: (0, 0, 0)), row(d),
                     row(pw), row(2 * d)]
        out_shape = [ft_shape(F32), ft_shape(F32), jax.ShapeDtypeStruct((2, N_HEADS, t), F32),
                     tok_shape(d, F32), tok_shape(pw, F32), tok_shape(2 * d, F32)]
    return pl.pallas_call(
        functools.partial(_inproj_kernel, d=d, pw=pw, prompt=prompt),
        grid=(t // tm,),
        in_specs=[row(d), sc_spec, sh_spec,
                  pl.BlockSpec((d, cols), lambda i: (0, 0), pipeline_mode=pl.Buffered(1)),
                  tab_t_spec, tab_t_spec],
        out_specs=out_specs,
        out_shape=out_shape,
        compiler_params=_params("arbitrary"),
        name="in_proj",
    )(x, sc_arr, sh_arr, w_in, *tabs_t)


def _rope_tables(pos):
    half = HEAD_DIM // 2
    inv = ROPE_THETA ** (-jnp.arange(half, dtype=F32) / half)
    ang = inv[:, None] * pos.astype(F32)[None, :]
    return jnp.cos(ang), jnp.sin(ang)


def _attn_kernel(lam_ref, qt_ref, k_ref, vt_ref, g_ref, o_ref, q2_sc, sta_sc, stb_sc, m_sc, l_sc, acc_sc,
                 *, tq, lam_init):
    qi = pl.program_id(2)
    zero = jnp.zeros((HEAD_DIM, tq), BF16)
    q2_sc[0:HEAD_DIM, 0:tq] = qt_ref[0:HEAD_DIM, :]
    q2_sc[HEAD_DIM:V_DIM, 0:tq] = zero
    q2_sc[0:HEAD_DIM, tq:2 * tq] = zero
    q2_sc[HEAD_DIM:V_DIM, tq:2 * tq] = qt_ref[HEAD_DIM:V_DIM, :]
    m_sc[...] = jnp.full(m_sc.shape, -jnp.inf, F32)
    l_sc[...] = jnp.zeros(l_sc.shape, F32)
    acc_sc[...] = jnp.zeros(acc_sc.shape, F32)

    def scores(ki, st_ref):
        start = pl.multiple_of(ki * tq, tq)
        st_ref[...] = jnp.dot(k_ref[pl.ds(start, tq), :], q2_sc[...], preferred_element_type=F32)

    def consume(ki, st_ref, masked):
        st = st_ref[...]
        if masked:
            key = lax.broadcasted_iota(jnp.int32, st.shape, 0)
            qry = lax.broadcasted_iota(jnp.int32, st.shape, 1)
            qry = jnp.where(qry >= tq, qry - tq, qry)
            st = jnp.where(key <= qry, st, -jnp.inf)
        m_old = m_sc[...]
        m_new = jnp.maximum(m_old, jnp.max(st, axis=0, keepdims=True))
        alpha = jnp.exp2(m_old - m_new)
        p = jnp.exp2(st - m_new).astype(BF16)
        start = pl.multiple_of(ki * tq, tq)
        vt = vt_ref[:, pl.ds(start, tq)]
        vt1 = jnp.concatenate([vt, jnp.ones((16, tq), BF16)], axis=0)
        r = jnp.dot(vt1, p, preferred_element_type=F32)
        l_sc[...] = alpha * l_sc[...] + r[V_DIM:V_DIM + 1]
        acc_sc[...] = alpha * acc_sc[...] + r[0:V_DIM]
        m_sc[...] = m_new

    scores(0, sta_sc)

    def pair(pi, carry):
        k0 = 2 * pi
        scores(k0 + 1, stb_sc)
        consume(k0, sta_sc, False)
        scores(k0 + 2, sta_sc)
        consume(k0 + 1, stb_sc, False)
        return carry

    lax.fori_loop(0, qi // 2, pair, 0)

    @pl.when(qi % 2 == 0)
    def _():
        consume(qi, sta_sc, True)

    @pl.when(qi % 2 == 1)
    def _():
        scores(qi, stb_sc)
        consume(qi - 1, sta_sc, False)
        consume(qi, stb_sc, True)

    lam = _lambda(lam_ref, lam_init)
    inv_l = 1.0 / l_sc[...]
    ot = acc_sc[:, 0:tq] * inv_l[:, 0:tq] - lam * (acc_sc[:, tq:2 * tq] * inv_l[:, tq:2 * tq])
    ms = jnp.mean(ot * ot, axis=0, keepdims=True)
    ot = ot * lax.rsqrt(ms + SUBLN_EPS) * g_ref[...] * (1.0 - lam_init)
    o_ref[...] = ot.T.astype(o_ref.dtype)


def _attn_prompt(qt_bf, k_bf, vt_bf, lam_p, subln_g, lam_init, tq):
    b, d, s = qt_bf.shape
    nh = d // V_DIM
    return pl.pallas_call(
        functools.partial(_attn_kernel, tq=tq, lam_init=lam_init),
        grid=(b, nh, s // tq),
        in_specs=[pl.BlockSpec(lam_p.shape, lambda bi, h, qi: (0, 0)),
                  pl.BlockSpec((None, V_DIM, tq), lambda bi, h, qi: (bi, h, qi)),
                  pl.BlockSpec((None, s, V_DIM), lambda bi, h, qi: (bi, 0, h)),
                  pl.BlockSpec((None, V_DIM, s), lambda bi, h, qi: (bi, h, 0)),
                  pl.BlockSpec((V_DIM, 1), lambda bi, h, qi: (0, 0))],
        out_specs=pl.BlockSpec((None, tq, V_DIM), lambda bi, h, qi: (bi, qi, h)),
        out_shape=jax.ShapeDtypeStruct((b, s, d), BF16),
        scratch_shapes=[pltpu.VMEM((V_DIM, 2 * tq), BF16),
                        pltpu.VMEM((tq, 2 * tq), F32),
                        pltpu.VMEM((tq, 2 * tq), F32),
                        pltpu.VMEM((1, 2 * tq), F32),
                        pltpu.VMEM((1, 2 * tq), F32),
                        pltpu.VMEM((V_DIM, 2 * tq), F32)],
        compiler_params=_params("arbitrary", "arbitrary", "arbitrary"),
        name="attn_prompt",
    )(lam_p, qt_bf, k_bf, vt_bf, subln_g.reshape(V_DIM, 1))


def _attn_sample_kernel(pt_ref, lam_ref, qt_ref, sn_ref, vn_ref, g_ref, *refs, n_pages, lam_init):
    del pt_ref
    kt_refs, v_refs = refs[:n_pages], refs[n_pages:2 * n_pages]
    o_ref, s_sc = refs[2 * n_pages:]
    b = pl.program_id(0)
    d, n_tok = qt_ref.shape
    page = kt_refs[0].shape[1]

    onehot = jnp.where(lax.broadcasted_iota(jnp.int32, (n_tok, page), 0) == b, 1.0, 0.0)
    qb = jnp.dot(qt_ref[...], onehot, preferred_element_type=F32,
                 precision=lax.Precision.HIGHEST)
    for p in range(n_pages):
        prod = (kt_refs[p][...] * qb).reshape(N_HEADS, V_DIM, page)
        s_sc[0, :, p * page:(p + 1) * page] = jnp.sum(prod[:, 0:HEAD_DIM, :], axis=1)
        s_sc[1, :, p * page:(p + 1) * page] = jnp.sum(prod[:, HEAD_DIM:V_DIM, :], axis=1)

    own = lax.broadcasted_iota(jnp.int32, (N_HEADS, n_tok), 1) == b

    def softmax_parts(which):
        s_new = jnp.max(jnp.where(own, sn_ref[which], -jnp.inf), axis=1, keepdims=True)
        s = s_sc[which]
        m = jnp.maximum(jnp.max(s, axis=1, keepdims=True), s_new)
        p = jnp.exp(s - m)
        p_new = jnp.exp(s_new - m)
        inv_l = 1.0 / (jnp.sum(p, axis=1, keepdims=True) + p_new)
        return p * inv_l, p_new * inv_l

    lam = _lambda(lam_ref, lam_init)
    w1, wn1 = softmax_parts(0)
    w2, wn2 = softmax_parts(1)
    w = w1 - lam * w2
    acc = (wn1 - lam * wn2) * vn_ref[...]
    for p in range(n_pages):
        for t in range(page):
            c = p * page + t
            acc = acc + w[:, c:c + 1] * v_refs[p][t]
    ms = jnp.mean(acc * acc, axis=1, keepdims=True)
    o_ref[...] = acc * lax.rsqrt(ms + SUBLN_EPS) * g_ref[...] * (1.0 - lam_init)


def _attn_sample(qt, s_new, v_new, cache_k, cache_v, page_table, lam_p, subln_g, lam_init):
    d, db = qt.shape
    n_pages = page_table.shape[1]
    n_phys, page = cache_k.shape[0], cache_k.shape[1]
    ck = jnp.transpose(cache_k, (0, 2, 3, 4, 1)).reshape(n_phys, d, page)
    pt = page_table.reshape(-1)

    def page_spec(p, shape):
        zeros = (0,) * len(shape)
        return pl.BlockSpec((None,) + shape, lambda b, pt_ref: (pt_ref[b * n_pages + p],) + zeros)

    const = lambda shape: pl.BlockSpec(shape, lambda b, pt_ref: (0,) * len(shape))
    head_spec = pl.BlockSpec((None, N_HEADS, V_DIM), lambda b, pt_ref: (b, 0, 0))
    grid_spec = pltpu.PrefetchScalarGridSpec(
        num_scalar_prefetch=1,
        grid=(db,),
        in_specs=[const(lam_p.shape), const((d, db)), const((2, N_HEADS, db)), head_spec, const((1, V_DIM))]
                 + [page_spec(p, (d, page)) for p in range(n_pages)]
                 + [page_spec(p, (page, N_HEADS, V_DIM)) for p in range(n_pages)],
        out_specs=head_spec,
        scratch_shapes=[pltpu.VMEM((2, N_HEADS, n_pages * page), F32)],
    )
    out = pl.pallas_call(
        functools.partial(_attn_sample_kernel, n_pages=n_pages, lam_init=lam_init),
        grid_spec=grid_spec,
        out_shape=jax.ShapeDtypeStruct((db, N_HEADS, V_DIM), F32),
        compiler_params=_params("arbitrary"),
        name="attn_sample",
    )(pt, lam_p, qt, s_new, v_new.reshape(db, N_HEADS, V_DIM), subln_g.reshape(1, V_DIM),
      *([ck] * n_pages), *([cache_v] * n_pages))
    return out.reshape(db, d)


POOL_HALO = 16


def _pool_tile(u_ref, halo_ref, ext_sc, seq_tile):
    tm = u_ref.shape[0]
    hb = halo_ref.shape[0]

    @pl.when(seq_tile == 0)
    def _():
        ext_sc[0:hb, :] = jnp.zeros((hb, ext_sc.shape[1]), F32)

    @pl.when(seq_tile > 0)
    def _():
        ext_sc[0:hb, :] = halo_ref[...]

    ext_sc[hb:hb + tm, :] = u_ref[...]
    pos = seq_tile * tm + lax.broadcasted_iota(jnp.int32, (tm, POOL_GROUP_DIM), 0)
    parts = []
    for g, win in enumerate(POOL_WINDOWS):
        sl = slice(g * POOL_GROUP_DIM, (g + 1) * POOL_GROUP_DIM)
        cur = ext_sc[hb:hb + tm, sl]
        tot = cur
        for j in range(1, win):
            tot = tot + ext_sc[hb - j:hb - j + tm, sl]
        cnt = jnp.minimum(win, pos + 1).astype(F32)
        parts.append(tot / cnt - cur)
    return jnp.concatenate(parts, axis=1)


def _pool_sample_kernel(st_ref, u_ref, o_ref, np_ref, *, start_pos):
    nb = st_ref.shape[0]
    u = u_ref[...]
    for g, win in enumerate(POOL_WINDOWS):
        sl = slice(g * POOL_GROUP_DIM, (g + 1) * POOL_GROUP_DIM)
        cur = u[:, sl]
        tot = cur
        for j in range(1, win):
            tot = tot + st_ref[nb - j, :, sl]
        o_ref[:, sl] = tot / float(min(win, start_pos + 1)) - cur
    for j in range(nb - 1):
        np_ref[j] = st_ref[j + 1]
    np_ref[nb - 1] = u


def _pool_sample(state, u, start_pos):
    db, nb, pw = state.shape
    full = lambda shape: pl.BlockSpec(shape, lambda i: (0,) * len(shape))
    pooled, new_pool = pl.pallas_call(
        functools.partial(_pool_sample_kernel, start_pos=start_pos),
        grid=(1,),
        in_specs=[full((nb, db, pw)), full((db, pw))],
        out_specs=[full((db, pw)), full((nb, db, pw))],
        out_shape=[jax.ShapeDtypeStruct((db, pw), F32), jax.ShapeDtypeStruct((nb, db, pw), F32)],
        compiler_params=_params("arbitrary"),
        name="pool_sample",
    )(jnp.transpose(state, (1, 0, 2)), u)
    return pooled, jnp.transpose(new_pool, (1, 0, 2))


def _route(lt):
    g = [lt[j:j + 1] for j in range(N_GROUPS)]
    gmax = functools.reduce(jnp.maximum, g)
    gidx = jnp.full(gmax.shape, N_GROUPS - 1, jnp.int32)
    for j in range(N_GROUPS - 2, -1, -1):
        gidx = jnp.where(g[j] == gmax, j, gidx)
    p_g = 1.0 / functools.reduce(jnp.add, [jnp.exp(gj - gmax) for gj in g])
    e = []
    for k in range(EXPERTS_PER_GROUP):
        sel = lt[N_GROUPS + k:N_GROUPS + k + 1]
        for j in range(1, N_GROUPS):
            r = N_GROUPS + j * EXPERTS_PER_GROUP + k
            sel = jnp.where(gidx == j, lt[r:r + 1], sel)
        e.append(sel)

    def first_argmax(vals):
        vmax = functools.reduce(jnp.maximum, vals)
        idx = jnp.full(vmax.shape, len(vals) - 1, jnp.int32)
        for k in range(len(vals) - 2, -1, -1):
            idx = jnp.where(vals[k] == vmax, k, idx)
        return vmax, idx

    v1, i1 = first_argmax(e)
    rest = [jnp.where(i1 == k, -jnp.inf, e[k]) for k in range(EXPERTS_PER_GROUP)]
    v2, i2 = first_argmax(rest)
    t = jnp.exp(v2 - v1)
    w1 = p_g / (1.0 + t)
    w2 = p_g * t / (1.0 + t)
    within = [jnp.where(i1 == k, w1, 0.0) + jnp.where(i2 == k, w2, 0.0) for k in range(EXPERTS_PER_GROUP)]
    rows = [jnp.where(gidx == j, within[k], 0.0) for j in range(N_GROUPS) for k in range(EXPERTS_PER_GROUP)]
    a, b = jnp.minimum(i1, i2), jnp.maximum(i1, i2)
    pair = jnp.where(a == 0, b - 1, jnp.where(a == 1, b + 1, 5))
    return jnp.concatenate(rows, axis=0), gidx * PAIRS_PER_GROUP + pair


def _pack_bf16_pair(a, b):
    a_bits = lax.bitcast_convert_type(a.astype(BF16).astype(F32), jnp.uint32)
    b_bits = lax.bitcast_convert_type(b.astype(BF16).astype(F32), jnp.uint32)
    return (a_bits >> 16) | (b_bits & jnp.uint32(0xFFFF0000))


def _unpack_bf16_pair(w):
    lo = lax.bitcast_convert_type(w << 16, F32)
    hi = lax.bitcast_convert_type(w & jnp.uint32(0xFFFF0000), F32)
    return lo, hi


def _merge_kernel(*refs, alpha, tiles_per_seq):
    if tiles_per_seq:
        u_ref, halo_ref, *refs, ext_sc = refs
        pooled = _pool_tile(u_ref, halo_ref, ext_sc, pl.program_id(0) % tiles_per_seq)
    else:
        pooled_ref, *refs = refs
        pooled = pooled_ref[...]
    (gates_ref, attn_ref, x_ref, g1_ref, sc2_ref, sh2_ref, wpool_ref, pscale_ref, wout_ref, lng_ref, lnb_ref,
     wr_ref, br_ref, x1_ref, *route_refs) = refs
    d = x_ref.shape[1]
    tm = x_ref.shape[0]
    mm_dtype = wout_ref.dtype
    dot = functools.partial(jnp.dot, preferred_element_type=F32,
                            precision=lax.Precision.HIGHEST if mm_dtype == F32 else None)
    pooled = pooled.astype(mm_dtype)
    parts = [dot(pooled[:, g * POOL_GROUP_DIM:(g + 1) * POOL_GROUP_DIM], wpool_ref[g])
             for g in range(len(POOL_WINDOWS))]
    pool_out = jnp.concatenate(parts, axis=1) * pscale_ref[...]
    g_dtype = gates_ref.dtype
    merged = (gates_ref[:, 0:d] * pool_out.astype(g_dtype)
              + gates_ref[:, d:2 * d] * attn_ref[...].astype(g_dtype))
    y = dot(merged.astype(mm_dtype), wout_ref[...])
    x1 = _layer_norm(alpha * x_ref[...] + (1.0 + g1_ref[...]) * y, lng_ref[...], lnb_ref[...])
    x1_ref[...] = x1
    h2 = x1 * (1.0 + sc2_ref[...]) + sh2_ref[...]
    if tiles_per_seq:
        lt = _dot_split3(wr_ref[...], h2, nt=True) + br_ref[...]
    else:
        lt = lax.dot_general(wr_ref[...], h2, NT_DIMS, preferred_element_type=F32,
                             precision=lax.Precision.HIGHEST) + br_ref[...]
    comb_t, cls = _route(lt)
    comb = jnp.concatenate([comb_t, jnp.zeros((LANES - N_EXPERTS, tm), F32)], axis=0).T
    if tiles_per_seq:
        row_ref, cls_ref = route_refs
        row_ref[:, 0:d // 2] = _pack_bf16_pair(h2[:, 0:d // 2], h2[:, d // 2:d])
        row_ref[:, d // 2:d // 2 + LANES] = lax.bitcast_convert_type(comb, jnp.uint32)
        cls_ref[...] = cls
    else:
        h2_ref, comb_ref = route_refs
        h2_ref[...] = h2.astype(BF16)
        comb_ref[...] = comb


def _merge(pool_in, gates, attn, x, mod_specs, mod_arrs, w_pool, pool_scale, w_out,
           ln_g, ln_b, w_r, b_r, alpha, tm, seq_len=None):
    t, d = x.shape
    pw = pool_in.shape[1]
    row = lambda width: pl.BlockSpec((tm, width), lambda i: (i, 0))
    const = lambda shape: pl.BlockSpec(shape, lambda i: (0,) * len(shape))
    if seq_len:
        halo_spec = pl.BlockSpec((POOL_HALO, pw), lambda i: (jnp.maximum(i * (tm // POOL_HALO) - 1, 0), 0))
        pool_specs, pool_args = [row(pw), halo_spec], [pool_in, pool_in]
        scratch = [pltpu.VMEM((POOL_HALO + tm, pw), F32)]
    else:
        pool_specs, pool_args, scratch = [row(pw)], [pool_in], []
    if seq_len:
        route_specs = [row(d // 2 + LANES), pl.BlockSpec((None, 1, tm), lambda i: (i, 0, 0))]
        route_shapes = [jax.ShapeDtypeStruct((t, d // 2 + LANES), jnp.uint32),
                        jax.ShapeDtypeStruct((t // tm, 1, tm), jnp.int32)]
    else:
        route_specs = [row(d), row(LANES)]
        route_shapes = [jax.ShapeDtypeStruct((t, d), BF16), jax.ShapeDtypeStruct((t, LANES), F32)]
    return pl.pallas_call(
        functools.partial(_merge_kernel, alpha=alpha, tiles_per_seq=seq_len // tm if seq_len else None),
        grid=(t // tm,),
        in_specs=[*pool_specs, row(2 * d), row(d), row(d), *mod_specs,
                  const(w_pool.shape), const((1, d)), const((d, d)), const((1, d)), const((1, d)),
                  const(w_r.shape), const(b_r.shape)],
        out_specs=[row(d), *route_specs],
        out_shape=[jax.ShapeDtypeStruct((t, d), F32), *route_shapes],
        scratch_shapes=scratch,
        compiler_params=_params("arbitrary"),
        name="merge",
    )(*pool_args, gates, attn, x, *mod_arrs, w_pool, pool_scale.reshape(1, d), w_out,
      ln_g.reshape(1, d), ln_b.reshape(1, d), w_r, b_r)


def _moe_kernel(h2_ref, comb_ref, wg_ref, wu_ref, wd_ref, x1_ref, g2_ref, lng_ref, lnb_ref,
                o_ref, acc_sc, *, alpha):
    j = pl.program_id(1)
    eb = wg_ref.shape[0]

    @pl.when(j == 0)
    def _():
        acc_sc[...] = jnp.zeros(acc_sc.shape, F32)

    h2 = h2_ref[...]
    comb = comb_ref[...]
    lane = lax.broadcasted_iota(jnp.int32, comb.shape, 1)
    for e in range(eb):
        hg = jnp.dot(h2, wg_ref[e], preferred_element_type=F32)
        hu = jnp.dot(h2, wu_ref[e], preferred_element_type=F32)
        c = jnp.sum(jnp.where(lane == j * eb + e, comb, 0.0), axis=1, keepdims=True)
        act = (hg * _sigmoid(hg)) * hu * c
        acc_sc[...] += jnp.dot(act.astype(BF16), wd_ref[e], preferred_element_type=F32)

    @pl.when(j == pl.num_programs(1) - 1)
    def _():
        y = alpha * x1_ref[...] + (1.0 + g2_ref[...]) * acc_sc[...]
        o_ref[...] = _layer_norm(y, lng_ref[...], lnb_ref[...])


def _moe(h2, comb, wg_bf, wu_bf, wd_bf, x1, g2_spec, g2_arr, ln_g, ln_b, alpha, tm, eb):
    t, d = x1.shape
    ne, _, ff = wg_bf.shape
    row = lambda width: pl.BlockSpec((tm, width), lambda i, j: (i, 0))
    const = lambda shape: pl.BlockSpec(shape, lambda i, j: (0,) * len(shape))
    return pl.pallas_call(
        functools.partial(_moe_kernel, alpha=alpha),
        grid=(t // tm, ne // eb),
        in_specs=[row(d), row(LANES),
                  pl.BlockSpec((eb, d, ff), lambda i, j: (j, 0, 0)),
                  pl.BlockSpec((eb, d, ff), lambda i, j: (j, 0, 0)),
                  pl.BlockSpec((eb, ff, d), lambda i, j: (j, 0, 0)),
                  row(d), g2_spec, const((1, d)), const((1, d))],
        out_specs=row(d),
        out_shape=jax.ShapeDtypeStruct((t, d), F32),
        scratch_shapes=[pltpu.VMEM((tm, d), F32)],
        compiler_params=_params("arbitrary", "arbitrary"),
        name="moe_ffn",
    )(h2, comb, wg_bf, wu_bf, wd_bf, x1, g2_arr, ln_g.reshape(1, d), ln_b.reshape(1, d))


def _move_rows(x, idx, n_out=None):
    n, width = idx.shape[0], x.shape[1]
    scatter = n_out is not None
    info = pltpu.get_tpu_info().sparse_core
    units = info.num_cores * info.num_subcores
    per_unit = n // (units * GATHER_WINDOW)
    assert per_unit * units * GATHER_WINDOW == n

    @pl.kernel(out_type=jax.ShapeDtypeStruct((n_out if scatter else n, width), x.dtype),
               mesh=plsc.VectorSubcoreMesh(core_axis_name="core", subcore_axis_name="subcore"),
               scratch_types=[pltpu.VMEM((1, GATHER_WINDOW), jnp.int32), pltpu.VMEM((GATHER_WINDOW, width), x.dtype)])
    def move_kernel(x_hbm, i_hbm, o_hbm, i_vmem, buf):
        unit = lax.axis_index("core") * info.num_subcores + lax.axis_index("subcore")

        @pl.loop(0, per_unit)
        def _(j):
            base = (unit * per_unit + j) * GATHER_WINDOW
            pltpu.sync_copy(i_hbm.at[:, pl.ds(base, GATHER_WINDOW)], i_vmem)
            if scatter:
                pltpu.sync_copy(x_hbm.at[pl.ds(base, GATHER_WINDOW), :], buf)
                pltpu.sync_copy(buf, o_hbm.at[i_vmem.at[0]])
            else:
                pltpu.sync_copy(x_hbm.at[i_vmem.at[0]], buf)
                pltpu.sync_copy(buf, o_hbm.at[pl.ds(base, GATHER_WINDOW), :])

    return move_kernel(x, idx.reshape(1, n))


def _dispatch_plan(cls, tile, n_slots):
    onehot = (cls[:, None] == jnp.arange(N_CLASSES, dtype=jnp.int32)[None, :]).astype(jnp.int32)
    rank = jnp.cumsum(onehot, axis=0) - onehot
    counts = jnp.sum(onehot, axis=0)
    padded = -(-counts // tile) * tile
    ends = jnp.cumsum(padded)
    starts = ends - padded
    pos = jnp.sum(onehot * (starts[None, :] + rank), axis=1)
    tile_start = jnp.arange(n_slots // tile, dtype=jnp.int32) * tile
    tile_class = jnp.minimum(jnp.sum((tile_start[:, None] >= ends[None, :]).astype(jnp.int32), axis=1),
                             N_CLASSES - 1)
    tile_rows = jnp.clip((starts + counts)[tile_class] - tile_start, 0, tile)
    return pos, tile_class, tile_rows


def _moe_grouped_kernel(te_ref, tr_ref, row_ref, *refs):
    *w_refs, y_ref = refs
    n_rows = tr_ref[pl.program_id(0)]
    half = y_ref.shape[1]

    @pl.when(n_rows == 0)
    def _():
        y_ref[...] = jnp.zeros(y_ref.shape, y_ref.dtype)

    @pl.when(n_rows > 0)
    def _():
        live = lax.broadcasted_iota(jnp.int32, (row_ref.shape[0], 1), 0) < n_rows
        lo, hi = _unpack_bf16_pair(row_ref[:, 0:half])
        h2 = jnp.where(live, jnp.concatenate([lo, hi], axis=1), 0.0).astype(BF16)
        comb = jnp.where(live, lax.bitcast_convert_type(row_ref[:, half:half + LANES], F32), 0.0)
        lane = lax.broadcasted_iota(jnp.int32, comb.shape, 1)
        acc = jnp.zeros((h2.shape[0], 2 * half), F32)
        for k in range(TOP_K_IN_GROUP):
            wg_ref, wu_ref, wd_ref = w_refs[3 * k:3 * k + 3]
            expert = te_ref[k, pl.program_id(0)]
            hg = jnp.dot(h2, wg_ref[...], preferred_element_type=F32)
            hu = jnp.dot(h2, wu_ref[...], preferred_element_type=F32)
            c = jnp.sum(jnp.where(lane == expert, comb, 0.0), axis=1, keepdims=True)
            act = (hg * _sigmoid(hg)) * hu * c
            acc = acc + jnp.dot(act.astype(BF16), wd_ref[...], preferred_element_type=F32)
        y_ref[...] = _pack_bf16_pair(acc[:, 0:half], acc[:, half:2 * half])


def _moe_grouped(rows_sorted, tile_experts, tile_rows, wg_bf, wu_bf, wd_bf, tile):
    n_slots, width = rows_sorted.shape
    ne, d, ff = wg_bf.shape
    expert_w = lambda k, shape: pl.BlockSpec((None,) + shape, lambda i, te, tr: (te[k, i], 0, 0))
    w_specs, w_args = [], []
    for k in range(TOP_K_IN_GROUP):
        w_specs += [expert_w(k, (d, ff)), expert_w(k, (d, ff)), expert_w(k, (ff, d))]
        w_args += [wg_bf, wu_bf, wd_bf]
    grid_spec = pltpu.PrefetchScalarGridSpec(
        num_scalar_prefetch=2,
        grid=(n_slots // tile,),
        in_specs=[pl.BlockSpec((tile, width), lambda i, te, tr: (i, 0)), *w_specs],
        out_specs=pl.BlockSpec((tile, d // 2), lambda i, te, tr: (i, 0)),
    )
    return pl.pallas_call(
        _moe_grouped_kernel,
        grid_spec=grid_spec,
        out_shape=jax.ShapeDtypeStruct((n_slots, d // 2), jnp.uint32),
        compiler_params=_params("arbitrary"),
        name="moe_grouped",
    )(tile_experts, tile_rows, rows_sorted, *w_args)


def _moe_finish_kernel(x1_ref, y_ref, g2_ref, lng_ref, lnb_ref, o_ref, *, alpha):
    lo, hi = _unpack_bf16_pair(y_ref[...])
    y = jnp.concatenate([lo, hi], axis=1)
    o_ref[...] = _layer_norm(alpha * x1_ref[...] + (1.0 + g2_ref[...]) * y, lng_ref[...], lnb_ref[...])


def _moe_finish(x1, y_packed, g2_spec, g2_arr, ln_g, ln_b, alpha, tm):
    t, d = x1.shape
    row = lambda width: pl.BlockSpec((tm, width), lambda i: (i, 0))
    const = lambda shape: pl.BlockSpec(shape, lambda i: (0,) * len(shape))
    return pl.pallas_call(
        functools.partial(_moe_finish_kernel, alpha=alpha),
        grid=(t // tm,),
        in_specs=[row(d), row(d // 2), g2_spec, const((1, d)), const((1, d))],
        out_specs=row(d),
        out_shape=jax.ShapeDtypeStruct((t, d), F32),
        compiler_params=_params("arbitrary"),
        name="moe_finish",
    )(x1, y_packed, g2_arr, ln_g.reshape(1, d), ln_b.reshape(1, d))


def kernel(x_prompt, x_sample, cache_k, cache_v, state_pool, page_table, c_prompt, c_sample, w_ada, b_ada, w_in, lambda_q1, lambda_k1, lambda_q2, lambda_k2, subln_g, w_pool, pool_scale, w_out, ln1_g, ln1_b, w_group, b_group, w_router, b_router, w_e_gate, w_e_up, w_e_down, ln2_g, ln2_b):
    depth = w_in.shape[0]
    bsz, seq, d = x_prompt.shape
    db, dseq, _ = x_sample.shape
    assert dseq == 1, "sample group decodes one token per sequence"
    n_pages, page = page_table.shape[1], cache_k.shape[2]
    past_len = n_pages * page
    alpha = (2.0 * depth) ** 0.25
    tp, ts = bsz * seq, db * dseq

    sc_info = pltpu.get_tpu_info().sparse_core
    sc_units = sc_info.num_cores * sc_info.num_subcores
    assert tp % (GATHER_WINDOW * sc_units) == 0

    xp = x_prompt.reshape(tp, d)
    xs = x_sample.reshape(ts, d)
    n_c = bsz + db
    n_c_pad = -(-n_c // 8) * 8
    c_all = jnp.concatenate([c_prompt, c_sample, jnp.zeros((n_c_pad - n_c, d), F32)], axis=0)
    tabs_p = _rope_tables(jnp.arange(seq))
    tabs_s = _rope_tables(jnp.full((ts,), past_len))

    outs = {k: [] for k in ("kp", "vp", "pp", "ks", "vs", "ps")}
    for l in range(depth):
        lam_init = 0.8 - 0.6 * math.exp(-0.3 * l)
        mod = _ada(c_all, w_ada[l], b_ada[l])
        mod_p = mod[:bsz].reshape(bsz * N_MOD, 1, d)
        mod_s = mod[bsz:n_c]
        w_in_bf = w_in[l].astype(BF16)
        w_pool_bf = w_pool[l].astype(BF16)
        w_out_bf = w_out[l].astype(BF16)
        wg_bf, wu_bf, wd_bf = (w[l].astype(BF16) for w in (w_e_gate, w_e_up, w_e_down))
        lam_p = jnp.stack([lambda_q1[l], lambda_k1[l], lambda_q2[l], lambda_k2[l]])
        n_r = N_GROUPS + N_EXPERTS
        w_r = jnp.concatenate([w_group[l], w_router[l]], axis=1).T
        w_r = jnp.concatenate([w_r, jnp.zeros((ROUTER_ROWS - n_r, d), F32)], axis=0)
        b_r = jnp.concatenate([b_group[l], b_router[l], jnp.zeros((ROUTER_ROWS - n_r,), F32)]).reshape(ROUTER_ROWS, 1)

        def pmod(k, tm, two_d=False):
            per_b = seq // tm
            if two_d:
                return pl.BlockSpec((None, 1, d), lambda i, j: ((i // per_b) * N_MOD + k, 0, 0))
            return pl.BlockSpec((None, 1, d), lambda i: ((i // per_b) * N_MOD + k, 0, 0))

        nblk = seq // TM_PROJ
        tab_t_spec = pl.BlockSpec((HEAD_DIM // 2, TM_PROJ), lambda i: (0, i % nblk))
        qt_bf, kt_f, k_bf, v_f, vt_bf, u, gates = _inproj(
            xp, pmod(1, TM_PROJ), pmod(0, TM_PROJ), mod_p, mod_p, w_in_bf,
            tabs_p, tab_t_spec, TM_PROJ, bsz, True)
        attn = _attn_prompt(qt_bf, k_bf.reshape(bsz, seq, d), vt_bf,
                            lam_p, subln_g[l], lam_init, TQ).reshape(tp, d)
        x1, ffn_rows, cls = _merge(u, gates, attn, xp,
                                   [pmod(2, TM_MERGE), pmod(4, TM_MERGE), pmod(3, TM_MERGE)], [mod_p] * 3,
                                   w_pool_bf, pool_scale[l], w_out_bf, ln1_g[l], ln1_b[l], w_r, b_r, alpha,
                                   TM_MERGE, seq_len=seq)
        slot_quantum = GATHER_WINDOW * sc_units
        n_slots = -(-(tp + N_CLASSES * (TM_GROUP - 1)) // slot_quantum) * slot_quantum
        pos, tile_class, tile_rows = _dispatch_plan(cls.reshape(tp), TM_GROUP, n_slots)
        class_experts = jnp.array([[g * EXPERTS_PER_GROUP + pair[k] for g in range(N_GROUPS) for pair in EXPERT_PAIRS]
                                   for k in range(TOP_K_IN_GROUP)], jnp.int32)
        tile_experts = class_experts[:, tile_class]
        rows_sorted = _move_rows(ffn_rows, pos, n_slots)
        outs["kp"].append(jnp.transpose(kt_f.reshape(bsz, N_HEADS, 2, HEAD_DIM, seq), (0, 4, 1, 2, 3)))
        outs["vp"].append(v_f.reshape(bsz, seq, N_HEADS, V_DIM))
        outs["pp"].append(u.reshape(bsz, seq, -1)[:, seq - POOL_BUF:])

        def smod(k, two_d=False):
            if two_d:
                return pl.BlockSpec((ts, d), lambda i, j: (0, k))
            return pl.BlockSpec((ts, d), lambda i: (0, k))

        tab_t_spec_s = pl.BlockSpec((HEAD_DIM // 2, ts), lambda i: (0, 0))
        qt_s, kt_f, s_new, v_f, u, gates = _inproj(
            xs, smod(1), smod(0), mod_s, mod_s, w_in[l],
            tabs_s, tab_t_spec_s, ts, 1, False)
        hb = ts // 2
        attn_half = lambda sl: _attn_sample(qt_s[0][:, sl], s_new[:, :, sl], v_f[sl], cache_k[l], cache_v[l],
                                            page_table[sl], lam_p, subln_g[l], lam_init)
        rows_sorted, attn_a = lax.optimization_barrier((rows_sorted, attn_half(slice(0, hb))))
        y_sorted = _moe_grouped(rows_sorted, tile_experts, tile_rows, wg_bf, wu_bf, wd_bf, TM_GROUP)
        y_tok, attn_b = lax.optimization_barrier((_move_rows(y_sorted, pos), attn_half(slice(hb, ts))))
        xp = _moe_finish(x1, y_tok, pmod(5, TM_MOE), mod_p, ln2_g[l], ln2_b[l], alpha, TM_MOE)
        attn = jnp.concatenate([attn_a, attn_b], axis=0)
        pooled, new_pool = _pool_sample(state_pool[l], u, past_len)
        x1, h2, comb = _merge(pooled, gates, attn, xs, [smod(2), smod(4), smod(3)], [mod_s] * 3,
                              w_pool[l], pool_scale[l], w_out[l], ln1_g[l], ln1_b[l], w_r, b_r, alpha, ts)
        xs = _moe(h2, comb, wg_bf, wu_bf, wd_bf, x1, smod(5, True), mod_s,
                  ln2_g[l], ln2_b[l], alpha, ts, E_BLK)
        outs["ks"].append(jnp.transpose(kt_f.reshape(N_HEADS, 2, HEAD_DIM, db, dseq), (3, 4, 0, 1, 2)))
        outs["vs"].append(v_f.reshape(db, dseq, N_HEADS, V_DIM))
        outs["ps"].append(new_pool)

    return (xp.reshape(bsz, seq, d), xs.reshape(db, dseq, d),
            jnp.stack(outs["kp"]), jnp.stack(outs["vp"]), jnp.stack(outs["pp"]),
            jnp.stack(outs["ks"]), jnp.stack(outs["vs"]), jnp.stack(outs["ps"]))
```
